```python
import math
import jax, jax.numpy as jnp
from jax import lax
import numpy as np

D_MODEL = 1024
BATCH = 16
SEQ = 2048
DEPTH = 2

CTX_LEN = 256
GRID_W = 64
EPS = 1e-6

MLSTM_HEADS = 4
MLSTM_HEAD_DIM = D_MODEL // 8
MLSTM_WIDTH = MLSTM_HEADS * MLSTM_HEAD_DIM
MLSTM_CHUNK = 128
FORGET_BIAS = 4.0
SGU_GROUPS = 4
SGU_WIDTH = D_MODEL // 4
SGU_CHUNK = 128
CONV_WIDTH = D_MODEL // 4
CONV_K = 31

MIX_WIDTH = MLSTM_WIDTH + SGU_WIDTH + CONV_WIDTH
FFN_HIDDEN = int(math.ceil(8 * D_MODEL / 3 / 256)) * 256

Q_OFF = 0
K_OFF = MLSTM_WIDTH
V_OFF = 2 * MLSTM_WIDTH
O_OFF = 3 * MLSTM_WIDTH
GATE_OFF = 4 * MLSTM_WIDTH
SGU_OFF = GATE_OFF + 4 * MLSTM_HEADS
CONV_OFF = SGU_OFF + 2 * SGU_WIDTH
IN_COLS = CONV_OFF + 2 * CONV_WIDTH

kernel_name = 'hybrid_mlstm_sgu_conv_dit_block'


def rmsnorm(t, g):
    t32 = t.astype(jnp.float32)
    y = t32 * lax.rsqrt(jnp.mean(t32 * t32, axis=-1, keepdims=True) + EPS)
    return (y * g).astype(t.dtype)


def layernorm(t, g, b):
    t32 = t.astype(jnp.float32)
    mu = jnp.mean(t32, axis=-1, keepdims=True)
    d = t32 - mu
    var = jnp.mean(d * d, axis=-1, keepdims=True)
    return (d * lax.rsqrt(var + EPS) * g + b).astype(t.dtype)


def swiglu(h, w_gu, w_down):
    gate, up = jnp.split(h @ w_gu, 2, axis=-1)
    return (jax.nn.silu(gate) * up) @ w_down


def mlstm_inputs(p):
    B, L, _ = p.shape
    def heads(t):
        return t.reshape(B, L, MLSTM_HEADS, MLSTM_HEAD_DIM).transpose(0, 2, 1, 3).astype(jnp.float32)
    q = heads(p[..., Q_OFF:K_OFF])
    k = heads(p[..., K_OFF:V_OFF]) * (MLSTM_HEAD_DIM ** -0.5)
    v = heads(p[..., V_OFF:O_OFF])
    g = p[..., GATE_OFF:SGU_OFF].astype(jnp.float32).reshape(B, L, 4, MLSTM_HEADS).transpose(2, 0, 3, 1)
    gates = (g[0], jax.nn.log_sigmoid(g[1]), g[2], jax.nn.log_sigmoid(g[3]))
    return q, k, v, gates


def zero_state(batch):
    return (jnp.zeros((batch, MLSTM_HEADS, MLSTM_HEAD_DIM, MLSTM_HEAD_DIM), jnp.float32),
            jnp.zeros((batch, MLSTM_HEADS, MLSTM_HEAD_DIM), jnp.float32),
            jnp.zeros((batch, MLSTM_HEADS), jnp.float32))


def mlstm_scan(q, k, v, logi, logf, state, with_output):
    B, H, L, d = q.shape
    nc = L // MLSTM_CHUNK
    def to_chunks(t):
        t = t.reshape((B, H, nc, MLSTM_CHUNK) + t.shape[3:])
        return jnp.moveaxis(t, 2, 0)
    xs = tuple(to_chunks(t) for t in (q, k, v, logi, logf))
    order = jnp.tril(jnp.ones((MLSTM_CHUNK, MLSTM_CHUNK), bool))

    def step(carry, inp):
        C, n, m = carry
        qc, kc, vc, ic, fc = inp
        b = jnp.cumsum(fc, axis=-1)
        b_last = b[..., -1]
        w_log = b_last[..., None] - b + ic
        m_new = jnp.maximum(b_last + m, jnp.max(w_log, axis=-1))
        decay = jnp.exp(b_last + m - m_new)
        ks = kc * jnp.exp(w_log - m_new[..., None])[..., None]
        C_new = decay[..., None, None] * C + jnp.einsum('bhsk,bhsv->bhkv', ks, vc)
        n_new = decay[..., None] * n + jnp.sum(ks, axis=2)
        if not with_output:
            return (C_new, n_new, m_new), None
        a = b + m[..., None]
        Dlog = b[..., :, None] - b[..., None, :] + ic[..., None, :]
        Dlog = jnp.where(order, Dlog, -jnp.inf)
        m_t = jnp.maximum(a, jnp.max(Dlog, axis=-1))
        w_inter = jnp.exp(a - m_t)
        s = jnp.einsum('bhtk,bhsk->bhts', qc, kc) * jnp.exp(Dlog - m_t[..., None])
        num = w_inter[..., None] * jnp.einsum('bhtk,bhkv->bhtv', qc, C) + jnp.einsum('bhts,bhsv->bhtv', s, vc)
        den = w_inter * jnp.einsum('bhtk,bhk->bht', qc, n) + jnp.sum(s, axis=-1)
        h = num / jnp.maximum(jnp.abs(den), jnp.exp(-m_t))[..., None]
        return (C_new, n_new, m_new), h

    state, hs = lax.scan(step, state, xs)
    if not with_output:
        return None, state
    h = jnp.moveaxis(hs, 0, 2).reshape(B, H, L, d)
    return h, state


def _flip(t):
    return jnp.flip(t, axis=2)


def mlstm_bidir(q, k, v, gates, init_f, init_b, with_output):
    logi_f, logf_f, logi_b, logf_b = gates
    h_f, st_f = mlstm_scan(q, k, v, logi_f, logf_f, init_f, with_output)
    h_b, st_b = mlstm_scan(_flip(q), _flip(k), _flip(v), _flip(logi_b), _flip(logf_b), init_b, with_output)
    h = h_f + _flip(h_b) if with_output else None
    return h, st_f, st_b


def mlstm_post(h, o_pre, g):
    B, H, L, d = h.shape
    h = h * lax.rsqrt(jnp.mean(h * h, axis=-1, keepdims=True) + EPS)
    h = h.transpose(0, 2, 1, 3).reshape(B, L, H * d) * g
    return (h * jax.nn.sigmoid(o_pre.astype(jnp.float32))).astype(o_pre.dtype)


def chunk_sgu(z, ln_g, ln_b, w_s, b_s):
    z = jax.nn.gelu(z)
    u, v = jnp.split(z, 2, axis=-1)
    v = layernorm(v, ln_g, ln_b)
    B, L, C = v.shape
    vg = v.reshape(B, L // SGU_CHUNK, SGU_CHUNK, SGU_GROUPS, C // SGU_GROUPS)
    mixed = jnp.einsum('gpq,bnqgc->bnpgc', w_s, vg) + b_s.T[:, :, None]
    return u * mixed.reshape(B, L, C)


def dwconv1d(y, w, b):
    kw, ch = w.shape
    out = lax.conv_general_dilated(y, w.reshape(kw, 1, ch).astype(y.dtype), window_strides=(1,),
                                   padding=[(kw // 2, kw // 2)], dimension_numbers=('NWC', 'WIO', 'NWC'),
                                   feature_group_count=ch)
    return out + b


def axial_dwconv(y, w, b):
    B, L, C = y.shape
    rows = L // GRID_W
    half = C // 2
    yr = dwconv1d(y[..., :half].reshape(B * rows, GRID_W, half), w[:, :half], b[:half]).reshape(B, L, half)
    yc = y[..., half:].reshape(B, rows, GRID_W, half).transpose(0, 2, 1, 3).reshape(B * GRID_W, rows, half)
    yc = dwconv1d(yc, w[:, half:], b[half:]).reshape(B, GRID_W, rows, half).transpose(0, 2, 1, 3).reshape(B, L, half)
    return jnp.concatenate([yr, yc], axis=-1)


def conformer_conv(z, w_dw, b_dw, ln_g, ln_b, grid):
    a, g = jnp.split(z, 2, axis=-1)
    y = a * jax.nn.sigmoid(g)
    y = axial_dwconv(y, w_dw, b_dw) if grid else dwconv1d(y, w_dw, b_dw)
    return jax.nn.silu(layernorm(y, ln_g, ln_b))


def mix_groups(p, h_mlstm, mlstm_g, sgu_ln_g, sgu_ln_b, sgu_w, sgu_b, conv_w, conv_b, conv_ln_g, conv_ln_b, grid):
    y_a = mlstm_post(h_mlstm, p[..., O_OFF:GATE_OFF], mlstm_g)
    y_b = chunk_sgu(p[..., SGU_OFF:CONV_OFF], sgu_ln_g, sgu_ln_b, sgu_w, sgu_b)
    y_c = conformer_conv(p[..., CONV_OFF:IN_COLS], conv_w, conv_b, conv_ln_g, conv_ln_b, grid)
    return jnp.concatenate([y_a, y_b, y_c], axis=-1)


def setup_inputs(seed: int = 0) -> dict:
    key = jax.random.key(seed)
    ks = jax.random.split(key, 24)
    def nrm(k, shape, scale):
        return jax.random.normal(k, shape, jnp.float32) * scale
    fb = np.zeros((IN_COLS,), np.float32)
    fb[GATE_OFF + MLSTM_HEADS:GATE_OFF + 2 * MLSTM_HEADS] = FORGET_BIAS
    fb[GATE_OFF + 3 * MLSTM_HEADS:GATE_OFF + 4 * MLSTM_HEADS] = FORGET_BIAS
    return {
        'x': nrm(ks[0], (BATCH, SEQ, D_MODEL), 1.0),
        'c': nrm(ks[1], (BATCH, D_MODEL), 1.0),
        'ctx': nrm(ks[2], (BATCH, CTX_LEN, D_MODEL), 1.0),
        'c_ctx': nrm(ks[3], (D_MODEL,), 1.0),
        'w_mod': nrm(ks[4], (DEPTH, D_MODEL, 6 * D_MODEL), 0.5 * D_MODEL ** -0.5),
        'b_mod': nrm(ks[5], (DEPTH, 6 * D_MODEL), 0.02),
        'norm1_g': 1.0 + nrm(ks[6], (DEPTH, D_MODEL), 0.02),
        'w_in': nrm(ks[7], (DEPTH, D_MODEL, IN_COLS), D_MODEL ** -0.5),
        'b_in': nrm(ks[8], (DEPTH, IN_COLS), 0.02) + jnp.asarray(fb),
        'mlstm_g': 1.0 + nrm(ks[9], (DEPTH, MLSTM_WIDTH), 0.02),
        'sgu_ln_g': 1.0 + nrm(ks[10], (DEPTH, SGU_WIDTH), 0.02),
        'sgu_ln_b': nrm(ks[11], (DEPTH, SGU_WIDTH), 0.02),
        'sgu_w': nrm(ks[12], (DEPTH, SGU_GROUPS, SGU_CHUNK, SGU_CHUNK), SGU_CHUNK ** -0.5),
        'sgu_b': nrm(ks[13], (DEPTH, SGU_GROUPS, SGU_CHUNK), 0.02),
        'conv_w': nrm(ks[14], (DEPTH, CONV_K, CONV_WIDTH), CONV_K ** -0.5),
        'conv_b': nrm(ks[15], (DEPTH, CONV_WIDTH), 0.02),
        'conv_ln_g': 1.0 + nrm(ks[16], (DEPTH, CONV_WIDTH), 0.02),
        'conv_ln_b': nrm(ks[17], (DEPTH, CONV_WIDTH), 0.02),
        'w_out': nrm(ks[18], (DEPTH, MIX_WIDTH, D_MODEL), MIX_WIDTH ** -0.5),
        'norm2_g': 1.0 + nrm(ks[19], (DEPTH, D_MODEL), 0.02),
        'w_gu': nrm(ks[20], (DEPTH, D_MODEL, 2 * FFN_HIDDEN), D_MODEL ** -0.5),
        'w_down': nrm(ks[21], (DEPTH, FFN_HIDDEN, D_MODEL), FFN_HIDDEN ** -0.5),
        'final_g': 1.0 + nrm(ks[22], (D_MODEL,), 0.02),
    }


def reference(x, c, ctx, c_ctx, w_mod, b_mod, norm1_g, w_in, b_in, mlstm_g, sgu_ln_g, sgu_ln_b, sgu_w, sgu_b,
              conv_w, conv_b, conv_ln_g, conv_ln_b, w_out, norm2_g, w_gu, w_down, final_g):
    batch = x.shape[0]
    c_act = jax.nn.silu(c)
    cc_act = jax.nn.silu(c_ctx)
    xc = ctx
    for l in range(DEPTH):
        last = l == DEPTH - 1
        mod = (c_act @ w_mod[l] + b_mod[l])[:, None, :]
        modc = cc_act @ w_mod[l] + b_mod[l]
        sh1, sc1, g1, sh2, sc2, g2 = jnp.split(mod, 6, axis=-1)
        shc1, scc1, gc1, shc2, scc2, gc2 = jnp.split(modc, 6, axis=-1)

        px = (rmsnorm(x, norm1_g[l]) * (1.0 + sc1) + sh1) @ w_in[l] + b_in[l]
        pc = (rmsnorm(xc, norm1_g[l]) * (1.0 + scc1) + shc1) @ w_in[l] + b_in[l]
        qx, kx, vx, gtx = mlstm_inputs(px)
        qc, kc, vc, gtc = mlstm_inputs(pc)
        zs = zero_state(batch)
        h_c, st_f, st_b = mlstm_bidir(qc, kc, vc, gtc, zs, zs, not last)
        h_x, _, _ = mlstm_bidir(qx, kx, vx, gtx, st_f, st_b, True)

        y_x = mix_groups(px, h_x, mlstm_g[l], sgu_ln_g[l], sgu_ln_b[l], sgu_w[l], sgu_b[l],
                         conv_w[l], conv_b[l], conv_ln_g[l], conv_ln_b[l], True)
        x = x + g1 * (y_x @ w_out[l])
        x = x + g2 * swiglu(rmsnorm(x, norm2_g[l]) * (1.0 + sc2) + sh2, w_gu[l], w_down[l])

        if not last:
            y_c = mix_groups(pc, h_c, mlstm_g[l], sgu_ln_g[l], sgu_ln_b[l], sgu_w[l], sgu_b[l],
                             conv_w[l], conv_b[l], conv_ln_g[l], conv_ln_b[l], False)
            xc = xc + gc1 * (y_c @ w_out[l])
            xc = xc + gc2 * swiglu(rmsnorm(xc, norm2_g[l]) * (1.0 + scc2) + shc2, w_gu[l], w_down[l])
    return rmsnorm(x, final_g)
```

```python
import functools
import math

import jax
import jax.numpy as jnp
from jax import lax
from jax.experimental import pallas as pl
from jax.experimental.pallas import tpu as pltpu

D_MODEL = 1024
DEPTH = 2
GRID_W = 64
EPS = 1e-6

HEADS = 4
HEAD_DIM = 128
MLSTM_WIDTH = HEADS * HEAD_DIM
CHUNK = 128
SGU_GROUPS = 4
SGU_WIDTH = 256
CONV_WIDTH = 256
CONV_K = 31
CONV_R = CONV_K // 2
FFN_HIDDEN = 2816
FFN_CHUNK = 256

GATE_OFF = 4 * MLSTM_WIDTH
N_GATES = 4 * HEADS
SGU_OFF = GATE_OFF + N_GATES
QKV_COLS = 3 * MLSTM_WIDTH
REST_COLS = MLSTM_WIDTH + 2 * SGU_WIDTH + 2 * CONV_WIDTH
MAIN_COLS = QKV_COLS + REST_COLS
LANES = 128
MOD_ROWS = 24

VMEM_LIMIT = 56 * 1024 * 1024

F32 = jnp.float32
BF16 = jnp.bfloat16
HIGHEST = lax.Precision.HIGHEST


def _sigmoid(t):
    return 1.0 / (1.0 + jnp.exp(-t))


def _silu(t):
    return t * _sigmoid(t)


def _log_sigmoid(t):
    return jnp.minimum(t, 0.0) - jnp.log1p(jnp.exp(-jnp.abs(t)))


def _layernorm(t, g, b):
    mu = jnp.mean(t, axis=-1, keepdims=True)
    d = t - mu
    var = jnp.mean(d * d, axis=-1, keepdims=True)
    return d * lax.rsqrt(var + EPS) * g + b


def _rmsnorm(t, g):
    return t * lax.rsqrt(jnp.mean(t * t, axis=-1, keepdims=True) + EPS) * g


def _params(*sem):
    return pltpu.CompilerParams(dimension_semantics=sem, vmem_limit_bytes=VMEM_LIMIT)


def _mod_kernel(c_ref, w_ref, b_ref, o_ref):
    a = _silu(c_ref[...])
    o_ref[0] = jnp.dot(a, w_ref[0], precision=HIGHEST, preferred_element_type=F32) + b_ref[0]


def _modulation(c_all, w_mod, b_mod):
    tn = 1536
    n = w_mod.shape[-1]
    return pl.pallas_call(
        _mod_kernel,
        grid=(DEPTH, n // tn),
        in_specs=[
            pl.BlockSpec((MOD_ROWS, D_MODEL), lambda l, j: (0, 0)),
            pl.BlockSpec((1, D_MODEL, tn), lambda l, j: (l, 0, j)),
            pl.BlockSpec((1, 1, tn), lambda l, j: (l, 0, j)),
        ],
        out_specs=pl.BlockSpec((1, MOD_ROWS, tn), lambda l, j: (l, 0, j)),
        out_shape=jax.ShapeDtypeStruct((DEPTH, MOD_ROWS, n), F32),
        compiler_params=_params("parallel", "parallel"),
        name="modulation",
    )(c_all, w_mod, b_mod.reshape(DEPTH, 1, n))


def _inproj_kernel(x_ref, mod_ref, ng_ref, w_ref, b_ref, wg_ref, bg_ref,
                   qkv_ref, rest_ref, gcol_ref, grow_ref, *, tm):
    x = x_ref[0]
    xm = _rmsnorm(x, ng_ref[...]) * (1.0 + mod_ref[0, 1:2, :]) + mod_ref[0, 0:1, :]
    p = jnp.dot(xm.astype(BF16), w_ref[...], preferred_element_type=F32) + b_ref[...]
    kscale = HEAD_DIM ** -0.5
    qkv_ref[0, :, 0:MLSTM_WIDTH] = p[:, 0:MLSTM_WIDTH].astype(BF16)
    qkv_ref[0, :, MLSTM_WIDTH:2 * MLSTM_WIDTH] = (p[:, MLSTM_WIDTH:2 * MLSTM_WIDTH] * kscale).astype(BF16)
    qkv_ref[0, :, 2 * MLSTM_WIDTH:QKV_COLS] = p[:, 2 * MLSTM_WIDTH:QKV_COLS].astype(BF16)
    rest_ref[0] = p[:, QKV_COLS:MAIN_COLS]

    g = jnp.dot(xm, wg_ref[...], precision=HIGHEST, preferred_element_type=F32) + bg_ref[...]
    lsg = _log_sigmoid(g)
    r = lax.broadcasted_iota(jnp.int32, (CHUNK, CHUNK), 0)
    c = lax.broadcasted_iota(jnp.int32, (CHUNK, CHUNK), 1)
    tril = (c <= r).astype(F32)
    triu = (c >= r).astype(F32)
    is_ff = (c >= HEADS) & (c < 2 * HEADS)
    is_fb = (c >= 3 * HEADS) & (c < 4 * HEADS)
    for ch in range(tm // CHUNK):
        sl = slice(ch * CHUNK, (ch + 1) * CHUNK)
        pre = jnp.dot(tril, lsg[sl], precision=HIGHEST, preferred_element_type=F32)
        suf = jnp.dot(triu, lsg[sl], precision=HIGHEST, preferred_element_type=F32)
        gc = jnp.where(is_ff, pre, jnp.where(is_fb, suf, g[sl]))
        gcol_ref[0, sl, :] = gc[:, 0:N_GATES]
        grow_ref[0, ch] = gc.T[0:N_GATES, :]


def _inproj(x, mod, mod_row, ng, w_main, b_main, w_gate, b_gate, tm):
    bsz, seq, _ = x.shape
    nch = seq // CHUNK
    const = lambda b, t: (0, 0)
    return pl.pallas_call(
        functools.partial(_inproj_kernel, tm=tm),
        grid=(bsz, seq // tm),
        in_specs=[
            pl.BlockSpec((1, tm, D_MODEL), lambda b, t: (b, t, 0)),
            pl.BlockSpec((1, 6, D_MODEL), mod_row),
            pl.BlockSpec((1, D_MODEL), const),
            pl.BlockSpec((D_MODEL, MAIN_COLS), const),
            pl.BlockSpec((1, MAIN_COLS), const),
            pl.BlockSpec((D_MODEL, LANES), const),
            pl.BlockSpec((1, LANES), const),
        ],
        out_specs=[
            pl.BlockSpec((1, tm, QKV_COLS), lambda b, t: (b, t, 0)),
            pl.BlockSpec((1, tm, REST_COLS), lambda b, t: (b, t, 0)),
            pl.BlockSpec((1, tm, N_GATES), lambda b, t: (b, t, 0)),
            pl.BlockSpec((1, tm // CHUNK, N_GATES, CHUNK), lambda b, t: (b, t, 0, 0)),
        ],
        out_shape=[
            jax.ShapeDtypeStruct((bsz, seq, QKV_COLS), BF16),
            jax.ShapeDtypeStruct((bsz, seq, REST_COLS), F32),
            jax.ShapeDtypeStruct((bsz, seq, N_GATES), F32),
            jax.ShapeDtypeStruct((bsz, nch, N_GATES, CHUNK), F32),
        ],
        compiler_params=_params("parallel", "parallel"),
        name="inproj",
    )(x, mod, ng, w_main, b_main, w_gate, b_gate)


def _mlstm_kernel(qkvc_ref, gcc_ref, grc_ref, qkvx_ref, gcx_ref, grx_ref, *rest, ctx_out, n_ctx, n_x):
    if ctx_out:
        hc_ref, hx_ref, c_ref, m_ref = rest
    else:
        hx_ref, c_ref, m_ref = rest
        hc_ref = None
    c_ref[...] = jnp.zeros(c_ref.shape, F32)
    m_ref[...] = jnp.zeros(m_ref.shape, F32)
    r = lax.broadcasted_iota(jnp.int32, (CHUNK, CHUNK), 0)
    c = lax.broadcasted_iota(jnp.int32, (CHUNK, CHUNK), 1)
    masks = (c <= r, c >= r)
    ones = jnp.ones((CHUNK, HEAD_DIM), BF16)

    def unit(qkv_ref, gc_ref, gr_ref, j, head, direction, out_ref, mode):
        u = 2 * head + direction
        r0 = j * CHUNK if isinstance(j, int) else pl.multiple_of(j * CHUNK, CHUNK)
        rows = pl.ds(r0, CHUNK)
        q = qkv_ref[0, rows, head * HEAD_DIM:(head + 1) * HEAD_DIM]
        k = qkv_ref[0, rows, MLSTM_WIDTH + head * HEAD_DIM:MLSTM_WIDTH + (head + 1) * HEAD_DIM]
        v = qkv_ref[0, rows, 2 * MLSTM_WIDTH + head * HEAD_DIM:2 * MLSTM_WIDTH + (head + 1) * HEAD_DIM]
        gc = gc_ref[0, rows, :]
        gr = gr_ref[0, j]
        ci = 2 * HEADS * direction + head
        cb = ci + HEADS
        i_col, b_col = gc[:, ci:ci + 1], gc[:, cb:cb + 1]
        i_row, b_row = gr[ci:ci + 1, :], gr[cb:cb + 1, :]
        b_last = b_row[:, CHUNK - 1:CHUNK] if direction == 0 else b_row[:, 0:1]
        m_old = m_ref[u][0:1, 0:1]
        c_old = c_ref[u]
        v_aug = jnp.concatenate([v, ones], axis=1)

        wl_col = b_last - b_col + i_col
        wl_row = b_last - b_row + i_row
        m_new = jnp.maximum(b_last + m_old, jnp.max(wl_row, axis=-1, keepdims=True))
        decay = jnp.exp(b_last + m_old - m_new)
        ks = (k.astype(F32) * jnp.exp(wl_col - m_new)).astype(BF16)
        upd = lax.dot_general(ks, v_aug, (((0,), (0,)), ((), ())), preferred_element_type=F32)
        c_ref[u] = decay * c_old + upd
        m_ref[u] = jnp.broadcast_to(m_new, m_ref.shape[1:])

        if mode is None:
            return
        a_col = b_col + m_old
        dlog = jnp.where(masks[direction], b_col - b_row + i_row, -jnp.inf)
        m_t = jnp.maximum(a_col, jnp.max(dlog, axis=-1, keepdims=True))
        w_inter = jnp.exp(a_col - m_t)
        qk = lax.dot_general(q, k, (((1,), (1,)), ((), ())), preferred_element_type=F32)
        s = (qk * jnp.exp(dlog - m_t)).astype(BF16)
        res = w_inter * jnp.dot(q, c_old.astype(BF16), preferred_element_type=F32) \
            + jnp.dot(s, v_aug, preferred_element_type=F32)
        den = jnp.maximum(jnp.abs(res[:, HEAD_DIM:]), jnp.exp(-m_t))
        h = res[:, :HEAD_DIM] / den
        cols = slice(head * HEAD_DIM, (head + 1) * HEAD_DIM)
        if mode == "store":
            out_ref[0, rows, cols] = h
        else:
            out_ref[0, rows, cols] = out_ref[0, rows, cols] + h

    def step(qkv_ref, gc_ref, gr_ref, j, n, out_ref, mode):
        for head in range(HEADS):
            unit(qkv_ref, gc_ref, gr_ref, j, head, 0, out_ref, mode)
            unit(qkv_ref, gc_ref, gr_ref, n - 1 - j, head, 1, out_ref, mode)

    for j in range(n_ctx):
        mode = None if not ctx_out else ("store" if j < n_ctx // 2 else "acc")
        step(qkvc_ref, gcc_ref, grc_ref, j, n_ctx, hc_ref, mode)

    def body(j, carry, mode):
        step(qkvx_ref, gcx_ref, grx_ref, j, n_x, hx_ref, mode)
        return carry

    lax.fori_loop(0, n_x // 2, functools.partial(body, mode="store"), 0)
    lax.fori_loop(n_x // 2, n_x, functools.partial(body, mode="acc"), 0)


def _mlstm(qkv_c, gcol_c, grow_c, qkv_x, gcol_x, grow_x, ctx_out):
    bsz, lc, _ = qkv_c.shape
    lx = qkv_x.shape[1]
    n_ctx, n_x = lc // CHUNK, lx // CHUNK
    i3 = lambda b: (b, 0, 0)
    i4 = lambda b: (b, 0, 0, 0)
    out_specs = [pl.BlockSpec((1, lx, MLSTM_WIDTH), i3)]
    out_shape = [jax.ShapeDtypeStruct((bsz, lx, MLSTM_WIDTH), F32)]
    if ctx_out:
        out_specs.insert(0, pl.BlockSpec((1, lc, MLSTM_WIDTH), i3))
        out_shape.insert(0, jax.ShapeDtypeStruct((bsz, lc, MLSTM_WIDTH), F32))
    return pl.pallas_call(
        functools.partial(_mlstm_kernel, ctx_out=ctx_out, n_ctx=n_ctx, n_x=n_x),
        grid=(bsz,),
        in_specs=[
            pl.BlockSpec((1, lc, QKV_COLS), i3),
            pl.BlockSpec((1, lc, N_GATES), i3),
            pl.BlockSpec((1, n_ctx, N_GATES, CHUNK), i4),
            pl.BlockSpec((1, lx, QKV_COLS), i3),
            pl.BlockSpec((1, lx, N_GATES), i3),
            pl.BlockSpec((1, n_x, N_GATES, CHUNK), i4),
        ],
        out_specs=out_specs,
        out_shape=out_shape,
        scratch_shapes=[
            pltpu.VMEM((2 * HEADS, HEAD_DIM, 2 * HEAD_DIM), F32),
            pltpu.VMEM((2 * HEADS, 8, LANES), F32),
        ],
        compiler_params=_params("parallel"),
        name="mlstm",
    )(qkv_c, gcol_c, grow_c, qkv_x, gcol_x, grow_x)


def _shift_conv(y, w_ref, lo, width, pos):
    n = y.shape[0]
    acc = jnp.zeros(y.shape, F32)
    for j in range(CONV_K):
        d = j - CONV_R
        w = w_ref[j:j + 1, lo:lo + y.shape[1]]
        if d == 0:
            acc = acc + w * y
        else:
            sh = pltpu.roll(y, (-d) % n, axis=0)
            valid = (pos >= -d) if d < 0 else (pos < width - d)
            acc = acc + w * jnp.where(valid, sh, 0.0)
    return acc


def _mix_kernel(x_ref, mod_ref, os_ref, h_ref, z_ref, mg_ref, slg_ref, slb_ref, sw_ref, sb_ref,
                cw_ref, cb_ref, clg_ref, clb_ref, wo_ref, out_ref, *scratch, seq, tm, grid_mode):
    t = pl.program_id(1)
    r0 = pl.multiple_of(t * tm, tm)
    rows = pl.ds(r0, tm)
    half = CONV_WIDTH // 2
    pad = CONV_R * GRID_W

    if grid_mode:
        (ycol_ref,) = scratch

        @pl.when(t == 0)
        def _():
            ycol_ref[0:pad, :] = jnp.zeros((pad, half), F32)
            ycol_ref[pad + seq:pad + seq + pad, :] = jnp.zeros((pad, half), F32)
            ycol_ref[pad:pad + seq, :] = z_ref[0, :, half:CONV_WIDTH] * _sigmoid(
                z_ref[0, :, CONV_WIDTH + half:2 * CONV_WIDTH])

        pos = lax.broadcasted_iota(jnp.int32, (tm, half), 0) & (GRID_W - 1)
        y_row = z_ref[0, rows, 0:half] * _sigmoid(z_ref[0, rows, CONV_WIDTH:CONV_WIDTH + half])
        conv_r = _shift_conv(y_row, cw_ref, 0, GRID_W, pos)
        conv_c = jnp.zeros((tm, half), F32)
        for j in range(CONV_K):
            start = pl.multiple_of(r0 + GRID_W * j, GRID_W)
            conv_c = conv_c + cw_ref[j:j + 1, half:CONV_WIDTH] * ycol_ref[pl.ds(start, tm), :]
        conv = jnp.concatenate([conv_r, conv_c], axis=1)
    else:
        pos = lax.broadcasted_iota(jnp.int32, (tm, CONV_WIDTH), 0)
        y = z_ref[0, :, 0:CONV_WIDTH] * _sigmoid(z_ref[0, :, CONV_WIDTH:2 * CONV_WIDTH])
        conv = _shift_conv(y, cw_ref, 0, tm, pos)
    y_c = _silu(_layernorm(conv + cb_ref[...], clg_ref[...], clb_ref[...]))

    zs = jax.nn.gelu(os_ref[0, :, MLSTM_WIDTH:MLSTM_WIDTH + 2 * SGU_WIDTH])
    u = zs[:, 0:SGU_WIDTH]
    vn = _layernorm(zs[:, SGU_WIDTH:2 * SGU_WIDTH], slg_ref[...], slb_ref[...]).astype(BF16)
    lane_group = lax.broadcasted_iota(jnp.int32, (CHUNK, SGU_WIDTH), 1) // (SGU_WIDTH // SGU_GROUPS)
    mixed = []
    for ch in range(tm // CHUNK):
        vc = vn[ch * CHUNK:(ch + 1) * CHUNK]
        m = sb_ref[...]
        for g in range(SGU_GROUPS):
            m = m + jnp.where(lane_group == g, jnp.dot(sw_ref[g], vc, preferred_element_type=F32), 0.0)
        mixed.append(m)
    y_b = u * jnp.concatenate(mixed, axis=0)

    hs = []
    for head in range(HEADS):
        hh = h_ref[0, :, head * HEAD_DIM:(head + 1) * HEAD_DIM]
        hs.append(hh * lax.rsqrt(jnp.mean(hh * hh, axis=-1, keepdims=True) + EPS))
    y_a = jnp.concatenate(hs, axis=1) * mg_ref[...] * _sigmoid(os_ref[0, :, 0:MLSTM_WIDTH])

    o_a, o_b = MLSTM_WIDTH, MLSTM_WIDTH + SGU_WIDTH
    proj = jnp.dot(y_a.astype(BF16), wo_ref[0:o_a, :], preferred_element_type=F32)
    proj = proj + jnp.dot(y_b.astype(BF16), wo_ref[o_a:o_b, :], preferred_element_type=F32)
    proj = proj + jnp.dot(y_c.astype(BF16), wo_ref[o_b:, :], preferred_element_type=F32)
    out_ref[0] = x_ref[0] + mod_ref[0, 2:3, :] * proj


def _mix(x, mod, mod_row, rest, h, mg, slg, slb, sw, sbias, cw, cb, clg, clb, wo, tm, grid_mode):
    bsz, seq, _ = x.shape
    const2 = lambda b, t: (0, 0)
    const3 = lambda b, t: (0, 0, 0)
    scratch = []
    if grid_mode:
        scratch.append(pltpu.VMEM((seq + 2 * CONV_R * GRID_W, CONV_WIDTH // 2), F32))
    return pl.pallas_call(
        functools.partial(_mix_kernel, seq=seq, tm=tm, grid_mode=grid_mode),
        grid=(bsz, seq // tm),
        in_specs=[
            pl.BlockSpec((1, tm, D_MODEL), lambda b, t: (b, t, 0)),
            pl.BlockSpec((1, 6, D_MODEL), mod_row),
            pl.BlockSpec((1, tm, MLSTM_WIDTH + 2 * SGU_WIDTH), lambda b, t: (b, t, 0)),
            pl.BlockSpec((1, tm, MLSTM_WIDTH), lambda b, t: (b, t, 0)),
            pl.BlockSpec((1, seq, 2 * CONV_WIDTH), lambda b, t: (b, 0, 2)),
            pl.BlockSpec((1, MLSTM_WIDTH), const2),
            pl.BlockSpec((1, SGU_WIDTH), const2),
            pl.BlockSpec((1, SGU_WIDTH), const2),
            pl.BlockSpec((SGU_GROUPS, CHUNK, CHUNK), const3),
            pl.BlockSpec((CHUNK, SGU_WIDTH), const2),
            pl.BlockSpec((CONV_K, CONV_WIDTH), const2),
            pl.BlockSpec((1, CONV_WIDTH), const2),
            pl.BlockSpec((1, CONV_WIDTH), const2),
            pl.BlockSpec((1, CONV_WIDTH), const2),
            pl.BlockSpec((D_MODEL, D_MODEL), const2),
        ],
        out_specs=pl.BlockSpec((1, tm, D_MODEL), lambda b, t: (b, t, 0)),
        out_shape=jax.ShapeDtypeStruct((bsz, seq, D_MODEL), F32),
        scratch_shapes=scratch,
        compiler_params=_params("parallel", "arbitrary"),
        name="mix",
    )(x, mod, rest, h, rest, mg, slg, slb, sw, sbias, cw, cb, clg, clb, wo)


def _ffn_kernel(x_ref, mod_ref, ng_ref, wgu_ref, wd_ref, fg_ref, out_ref, *, final):
    x = x_ref[0]
    xm = (_rmsnorm(x, ng_ref[...]) * (1.0 + mod_ref[0, 4:5, :]) + mod_ref[0, 3:4, :]).astype(BF16)
    acc = jnp.zeros(x.shape, F32)
    for ch in range(FFN_HIDDEN // FFN_CHUNK):
        lo = ch * FFN_CHUNK
        gate = jnp.dot(xm, wgu_ref[:, lo:lo + FFN_CHUNK], preferred_element_type=F32)
        up = jnp.dot(xm, wgu_ref[:, FFN_HIDDEN + lo:FFN_HIDDEN + lo + FFN_CHUNK], preferred_element_type=F32)
        act = (_silu(gate) * up).astype(BF16)
        acc = acc + jnp.dot(act, wd_ref[lo:lo + FFN_CHUNK, :], preferred_element_type=F32)
    y = x + mod_ref[0, 5:6, :] * acc
    if final:
        y = _rmsnorm(y, fg_ref[...])
    out_ref[0] = y


def _ffn(x, mod, mod_row, ng, wgu, wd, fg, tm, final):
    bsz, seq, _ = x.shape
    const = lambda b, t: (0, 0)
    return pl.pallas_call(
        functools.partial(_ffn_kernel, final=final),
        grid=(bsz, seq // tm),
        in_specs=[
            pl.BlockSpec((1, tm, D_MODEL), lambda b, t: (b, t, 0)),
            pl.BlockSpec((1, 6, D_MODEL), mod_row),
            pl.BlockSpec((1, D_MODEL), const),
            pl.BlockSpec((D_MODEL, 2 * FFN_HIDDEN), const),
            pl.BlockSpec((FFN_HIDDEN, D_MODEL), const),
            pl.BlockSpec((1, D_MODEL), const),
        ],
        out_specs=pl.BlockSpec((1, tm, D_MODEL), lambda b, t: (b, t, 0)),
        out_shape=jax.ShapeDtypeStruct((bsz, seq, D_MODEL), F32),
        compiler_params=_params("parallel", "parallel"),
        name="ffn",
    )(x, mod, ng, wgu, wd, fg)


def kernel(x, c, ctx, c_ctx, w_mod, b_mod, norm1_g, w_in, b_in, mlstm_g, sgu_ln_g, sgu_ln_b, sgu_w, sgu_b,
           conv_w, conv_b, conv_ln_g, conv_ln_b, w_out, norm2_g, w_gu, w_down, final_g):
    bsz = x.shape[0]
    lc = ctx.shape[1]
    assert bsz + 1 <= MOD_ROWS
    c_all = jnp.zeros((MOD_ROWS, D_MODEL), F32).at[:bsz].set(c).at[bsz].set(c_ctx)
    mod_all = _modulation(c_all, w_mod, b_mod).reshape(DEPTH, MOD_ROWS, 6, D_MODEL)
    row_x = lambda b, t: (b, 0, 0)
    row_c = lambda b, t: (bsz, 0, 0)
    tm_x, tm_c = 512, lc

    xc = ctx
    for l in range(DEPTH):
        last = l == DEPTH - 1
        mod = mod_all[l]
        w_main = jnp.concatenate([w_in[l][:, :GATE_OFF], w_in[l][:, SGU_OFF:]], axis=1).astype(BF16)
        b_main = jnp.concatenate([b_in[l][:GATE_OFF], b_in[l][SGU_OFF:]])[None, :]
        w_gate = jnp.pad(w_in[l][:, GATE_OFF:SGU_OFF], ((0, 0), (0, LANES - N_GATES)))
        b_gate = jnp.pad(b_in[l][GATE_OFF:SGU_OFF], (0, LANES - N_GATES))[None, :]
        ng1, ng2 = norm1_g[l][None, :], norm2_g[l][None, :]
        mix_w = (mlstm_g[l][None, :], sgu_ln_g[l][None, :], sgu_ln_b[l][None, :], sgu_w[l].astype(BF16),
                 jnp.repeat(sgu_b[l].T, SGU_WIDTH // SGU_GROUPS, axis=1),
                 conv_w[l], conv_b[l][None, :], conv_ln_g[l][None, :], conv_ln_b[l][None, :],
                 w_out[l].astype(BF16))
        wgu, wd = w_gu[l].astype(BF16), w_down[l].astype(BF16)
        fg = final_g[None, :]

        qkv_x, rest_x, gcol_x, grow_x = _inproj(x, mod, row_x, ng1, w_main, b_main, w_gate, b_gate, tm_x)
        qkv_c, rest_c, gcol_c, grow_c = _inproj(xc, mod, row_c, ng1, w_main, b_main, w_gate, b_gate, tm_c)
        hs = _mlstm(qkv_c, gcol_c, grow_c, qkv_x, gcol_x, grow_x, ctx_out=not last)
        h_x = hs[-1]
        x = _mix(x, mod, row_x, rest_x, h_x, *mix_w, tm=tm_x, grid_mode=True)
        x = _ffn(x, mod, row_x, ng2, wgu, wd, fg, tm_x, final=last)
        if not last:
            xc = _mix(xc, mod, row_c, rest_c, hs[0], *mix_w, tm=tm_c, grid_mode=False)
            xc = _ffn(xc, mod, row_c, ng2, wgu, wd, fg, tm_c, final=False)
    return x
```

```python
import functools

import jax
import jax.numpy as jnp
from jax import lax
from jax.experimental import pallas as pl
from jax.experimental.pallas import tpu as pltpu

D_MODEL = 1024
DEPTH = 2
GRID_W = 64
EPS = 1e-6

HEADS = 4
HEAD_DIM = 128
MLSTM_WIDTH = HEADS * HEAD_DIM
CHUNK = 128
UNITS = 2 * HEADS
SGU_GROUPS = 4
SGU_WIDTH = 256
CONV_WIDTH = 256
CONV_K = 31
CONV_R = CONV_K // 2
FFN_HIDDEN = 2816
FFN_CHUNK = 256

GATE_OFF = 4 * MLSTM_WIDTH
N_GATES = 4 * HEADS
SGU_OFF = GATE_OFF + N_GATES
QKV_COLS = 3 * MLSTM_WIDTH
QK_COLS = 2 * MLSTM_WIDTH
REST_COLS = MLSTM_WIDTH + 2 * SGU_WIDTH + 2 * CONV_WIDTH
MAIN_COLS = QKV_COLS + REST_COLS
LANES = 128
SUBLANES = 8
BF16_ROWS = 16
MOD_ROWS = 24
N_ROWQ = 5
AUG = HEAD_DIM + BF16_ROWS

VMEM_LIMIT = 56 * 1024 * 1024

F32 = jnp.float32
BF16 = jnp.bfloat16
HIGHEST = lax.Precision.HIGHEST
NT = (((1,), (1,)), ((), ()))


def _sigmoid(t):
    return 1.0 / (1.0 + jnp.exp(-t))


def _silu(t):
    return t * _sigmoid(t)


def _log_sigmoid(t):
    return jnp.minimum(t, 0.0) - jnp.log1p(jnp.exp(-jnp.abs(t)))


def _layernorm(t, g, b):
    mu = jnp.mean(t, axis=-1, keepdims=True)
    d = t - mu
    var = jnp.mean(d * d, axis=-1, keepdims=True)
    return d * lax.rsqrt(var + EPS) * g + b


def _rmsnorm(t, g):
    return t * lax.rsqrt(jnp.mean(t * t, axis=-1, keepdims=True) + EPS) * g


def _params(*sem):
    return pltpu.CompilerParams(dimension_semantics=sem, vmem_limit_bytes=VMEM_LIMIT)


def _mod_kernel(c_ref, w_ref, b_ref, o_ref):
    a = _silu(c_ref[...])
    o_ref[0] = jnp.dot(a, w_ref[0], precision=HIGHEST, preferred_element_type=F32) + b_ref[0]


def _modulation(c_all, w_mod, b_mod):
    tn = 1536
    n = w_mod.shape[-1]
    return pl.pallas_call(
        _mod_kernel,
        grid=(DEPTH, n // tn),
        in_specs=[
            pl.BlockSpec((MOD_ROWS, D_MODEL), lambda l, j: (0, 0)),
            pl.BlockSpec((1, D_MODEL, tn), lambda l, j: (l, 0, j)),
            pl.BlockSpec((1, 1, tn), lambda l, j: (l, 0, j)),
        ],
        out_specs=pl.BlockSpec((1, MOD_ROWS, tn), lambda l, j: (l, 0, j)),
        out_shape=jax.ShapeDtypeStruct((DEPTH, MOD_ROWS, n), F32),
        compiler_params=_params("parallel", "parallel"),
        name="modulation",
    )(c_all, w_mod, b_mod.reshape(DEPTH, 1, n))


def _inproj_kernel(x_ref, mod_ref, ng_ref, w_ref, b_ref, wgh_ref, wgl_ref, bg_ref,
                   qk_ref, kt_ref, vt_ref, rest_ref, rows_ref, ecol_ref, *, tm):
    x = x_ref[0]
    xm = _rmsnorm(x, ng_ref[...]) * (1.0 + mod_ref[0, 1:2, :]) + mod_ref[0, 0:1, :]
    xh = xm.astype(BF16)
    xl = (xm - xh.astype(F32)).astype(BF16)
    p = jnp.dot(xh, w_ref[...], preferred_element_type=F32) + b_ref[...]
    pk = p[:, MLSTM_WIDTH:QK_COLS] * (HEAD_DIM ** -0.5)
    pv = p[:, QK_COLS:QKV_COLS]
    qk_ref[0, :, 0:MLSTM_WIDTH] = p[:, 0:MLSTM_WIDTH].astype(BF16)
    qk_ref[0, :, MLSTM_WIDTH:QK_COLS] = pk.astype(BF16)
    rest_ref[0] = p[:, QKV_COLS:MAIN_COLS]

    g = (jnp.dot(xh, wgh_ref[...], preferred_element_type=F32)
         + jnp.dot(xh, wgl_ref[...], preferred_element_type=F32)
         + jnp.dot(xl, wgh_ref[...], preferred_element_type=F32)) + bg_ref[...]
    gi = g[:, 0:LANES]
    lsg = _log_sigmoid(g[:, LANES:2 * LANES])
    r = lax.broadcasted_iota(jnp.int32, (CHUNK, CHUNK), 0)
    c = lax.broadcasted_iota(jnp.int32, (CHUNK, CHUNK), 1)
    tril = (c <= r).astype(F32)
    triu = (c >= r).astype(F32)
    lane = lax.broadcasted_iota(jnp.int32, (UNITS, CHUNK), 1)
    fwd_row = lax.broadcasted_iota(jnp.int32, (UNITS, CHUNK), 0) < HEADS
    for ch in range(tm // CHUNK):
        sl = slice(ch * CHUNK, (ch + 1) * CHUNK)
        for head in range(HEADS):
            hs = slice(head * HEAD_DIM, (head + 1) * HEAD_DIM)
            kt_ref[0, ch, hs, :] = pk[sl, hs].T.astype(BF16)
            vt_ref[0, ch, hs, :] = pv[sl, hs].T.astype(BF16)
        pre = jnp.dot(tril, lsg[sl], precision=HIGHEST, preferred_element_type=F32)
        suf = jnp.dot(triu, lsg[sl], precision=HIGHEST, preferred_element_type=F32)
        bmat = jnp.where(c < HEADS, pre, suf)
        emat = gi[sl] - bmat
        ecol_ref[0, sl, :] = emat
        et = emat.T[0:UNITS]
        bt = bmat.T[0:UNITS]
        pmax, smax = et, et
        k = 1
        while k < CHUNK:
            pmax = jnp.where(lane >= k, jnp.maximum(pmax, pltpu.roll(pmax, k, axis=1)), pmax)
            smax = jnp.where(lane < CHUNK - k, jnp.maximum(smax, pltpu.roll(smax, CHUNK - k, axis=1)), smax)
            k *= 2
        cme = jnp.where(fwd_row, pmax, smax)
        b_last = jnp.where(fwd_row, jnp.broadcast_to(bt[:, CHUNK - 1:CHUNK], bt.shape),
                           jnp.broadcast_to(bt[:, 0:1], bt.shape))
        cme_last = jnp.broadcast_to(jnp.max(et, axis=1, keepdims=True), et.shape)
        rows_ref[0, ch] = jnp.concatenate([et, cme, bt, b_last, cme_last], axis=0)


def _inproj(x, mod, mod_row, ng, w_main, b_main, wg_hi, wg_lo, b_gate, tm):
    bsz, seq, _ = x.shape
    nch = seq // CHUNK
    tch = tm // CHUNK
    const = lambda b, t: (0, 0)
    return pl.pallas_call(
        functools.partial(_inproj_kernel, tm=tm),
        grid=(bsz, seq // tm),
        in_specs=[
            pl.BlockSpec((1, tm, D_MODEL), lambda b, t: (b, t, 0)),
            pl.BlockSpec((1, 6, D_MODEL), mod_row),
            pl.BlockSpec((1, D_MODEL), const),
            pl.BlockSpec((D_MODEL, MAIN_COLS), const),
            pl.BlockSpec((1, MAIN_COLS), const),
            pl.BlockSpec((D_MODEL, 2 * LANES), const),
            pl.BlockSpec((D_MODEL, 2 * LANES), const),
            pl.BlockSpec((1, 2 * LANES), const),
        ],
        out_specs=[
            pl.BlockSpec((1, tm, QK_COLS), lambda b, t: (b, t, 0)),
            pl.BlockSpec((1, tch, MLSTM_WIDTH, CHUNK), lambda b, t: (b, t, 0, 0)),
            pl.BlockSpec((1, tch, MLSTM_WIDTH, CHUNK), lambda b, t: (b, t, 0, 0)),
            pl.BlockSpec((1, tm, REST_COLS), lambda b, t: (b, t, 0)),
            pl.BlockSpec((1, tch, N_ROWQ * UNITS, CHUNK), lambda b, t: (b, t, 0, 0)),
            pl.BlockSpec((1, tm, LANES), lambda b, t: (b, t, 0)),
        ],
        out_shape=[
            jax.ShapeDtypeStruct((bsz, seq, QK_COLS), BF16),
            jax.ShapeDtypeStruct((bsz, nch, MLSTM_WIDTH, CHUNK), BF16),
            jax.ShapeDtypeStruct((bsz, nch, MLSTM_WIDTH, CHUNK), BF16),
            jax.ShapeDtypeStruct((bsz, seq, REST_COLS), F32),
            jax.ShapeDtypeStruct((bsz, nch, N_ROWQ * UNITS, CHUNK), F32),
            jax.ShapeDtypeStruct((bsz, seq, LANES), F32),
        ],
        compiler_params=_params("parallel", "parallel"),
        name="inproj",
    )(x, mod, ng, w_main, b_main, wg_hi, wg_lo, b_gate)


def _mlstm_kernel(qkc_ref, ktc_ref, vtc_ref, rowc_ref, ecc_ref,
                  qkx_ref, ktx_ref, vtx_ref, rowx_ref, ecx_ref, *rest, ctx_out, n_ctx, n_x):
    if ctx_out:
        hc_ref, hx_ref, ct_ref, m_ref = rest
    else:
        hx_ref, ct_ref, m_ref = rest
        hc_ref = None
    ct_ref[...] = jnp.zeros(ct_ref.shape, F32)
    m_ref[...] = jnp.zeros(m_ref.shape, F32)
    r = lax.broadcasted_iota(jnp.int32, (CHUNK, CHUNK), 0)
    c = lax.broadcasted_iota(jnp.int32, (CHUNK, CHUNK), 1)
    masks = (r <= c, r >= c)
    ones = jnp.ones((BF16_ROWS, CHUNK), BF16)

    def unit(refs, j, head, direction, out_ref, mode):
        qk_ref, kt_ref, vt_ref, row_ref, ec_ref = refs
        u = direction * HEADS + head
        r0 = j * CHUNK if isinstance(j, int) else pl.multiple_of(j * CHUNK, CHUNK)
        toks = pl.ds(r0, CHUNK)
        hs = slice(head * HEAD_DIM, (head + 1) * HEAD_DIM)

        def row(q):
            return row_ref[0, j, q * UNITS + u:q * UNITS + u + 1, :]

        e, cme, b, b_last, cme_last = (row(q) for q in range(N_ROWQ))
        m_old = m_ref[u, 0:1, :]
        ct_old = ct_ref[u]
        vt_aug = jnp.concatenate([vt_ref[0, j, hs, :], ones], axis=0)

        g_last = jnp.maximum(m_old, cme_last)
        kst = (kt_ref[0, j, hs, :].astype(F32) * jnp.exp(e - cme_last)).astype(BF16)
        upd = lax.dot_general(vt_aug, kst, NT, preferred_element_type=F32)
        ct_ref[u] = jnp.exp(m_old - g_last) * ct_old + jnp.exp(cme_last - g_last) * upd
        m_ref[u] = jnp.broadcast_to(b_last + g_last, m_ref.shape[1:])

        if mode is None:
            return
        q = qk_ref[0, toks, hs]
        k = qk_ref[0, toks, MLSTM_WIDTH + head * HEAD_DIM:MLSTM_WIDTH + (head + 1) * HEAD_DIM]
        qkt = lax.dot_general(k, q, NT, preferred_element_type=F32)
        dl = jnp.where(masks[direction], ec_ref[0, toks, u:u + 1] - cme, -jnp.inf)
        s0t = (qkt * jnp.exp(dl)).astype(BF16)
        intra = jnp.dot(vt_aug, s0t, preferred_element_type=F32)
        inter = lax.dot_general(ct_old.astype(BF16), q, NT, preferred_element_type=F32)
        g = jnp.maximum(m_old, cme)
        res = jnp.exp(cme - g) * intra + jnp.exp(m_old - g) * inter
        den = jnp.maximum(jnp.abs(res[HEAD_DIM:HEAD_DIM + 1, :]), jnp.exp(-(b + g)))
        ht = res[0:HEAD_DIM, :] * (1.0 / den)
        if mode == "first":
            out_ref[0, toks, hs] = ht
        else:
            tot = out_ref[0, toks, hs] + ht
            hn = tot * lax.rsqrt(jnp.mean(tot * tot, axis=0, keepdims=True) + EPS)
            out_ref[0, toks, hs] = hn.T

    def step(refs, j, n, out_ref, mode):
        for head in range(HEADS):
            unit(refs, j, head, 0, out_ref, mode)
            unit(refs, n - 1 - j, head, 1, out_ref, mode)

    crefs = (qkc_ref, ktc_ref, vtc_ref, rowc_ref, ecc_ref)
    xrefs = (qkx_ref, ktx_ref, vtx_ref, rowx_ref, ecx_ref)
    for j in range(n_ctx):
        mode = None if not ctx_out else ("first" if j < n_ctx // 2 else "second")
        step(crefs, j, n_ctx, hc_ref, mode)

    def body(j, carry, mode):
        step(xrefs, j, n_x, hx_ref, mode)
        return carry

    lax.fori_loop(0, n_x // 2, functools.partial(body, mode="first"), 0)
    lax.fori_loop(n_x // 2, n_x, functools.partial(body, mode="second"), 0)


def _mlstm(ctx_in, x_in, ctx_out):
    bsz, lc, _ = ctx_in[0].shape
    lx = x_in[0].shape[1]
    n_ctx, n_x = lc // CHUNK, lx // CHUNK
    assert n_ctx % 2 == 0 and n_x % 2 == 0
    i3 = lambda b: (b, 0, 0)
    i4 = lambda b: (b, 0, 0, 0)

    def specs(seq, nch):
        return [
            pl.BlockSpec((1, seq, QK_COLS), i3),
            pl.BlockSpec((1, nch, MLSTM_WIDTH, CHUNK), i4),
            pl.BlockSpec((1, nch, MLSTM_WIDTH, CHUNK), i4),
            pl.BlockSpec((1, nch, N_ROWQ * UNITS, CHUNK), i4),
            pl.BlockSpec((1, seq, LANES), i3),
        ]

    out_specs = [pl.BlockSpec((1, lx, MLSTM_WIDTH), i3)]
    out_shape = [jax.ShapeDtypeStruct((bsz, lx, MLSTM_WIDTH), F32)]
    if ctx_out:
        out_specs.insert(0, pl.BlockSpec((1, lc, MLSTM_WIDTH), i3))
        out_shape.insert(0, jax.ShapeDtypeStruct((bsz, lc, MLSTM_WIDTH), F32))
    return pl.pallas_call(
        functools.partial(_mlstm_kernel, ctx_out=ctx_out, n_ctx=n_ctx, n_x=n_x),
        grid=(bsz,),
        in_specs=specs(lc, n_ctx) + specs(lx, n_x),
        out_specs=out_specs,
        out_shape=out_shape,
        scratch_shapes=[
            pltpu.VMEM((UNITS, AUG, HEAD_DIM), F32),
            pltpu.VMEM((UNITS, SUBLANES, LANES), F32),
        ],
        compiler_params=_params("parallel"),
        name="mlstm",
    )(*ctx_in, *x_in)


def _shift_conv(y, w_ref, lo, width, pos):
    n = y.shape[0]
    acc = jnp.zeros(y.shape, F32)
    for j in range(CONV_K):
        d = j - CONV_R
        w = w_ref[j:j + 1, lo:lo + y.shape[1]]
        if d == 0:
            acc = acc + w * y
        else:
            sh = pltpu.roll(y, (-d) % n, axis=0)
            valid = (pos >= -d) if d < 0 else (pos < width - d)
            acc = acc + w * jnp.where(valid, sh, 0.0)
    return acc


def _mix_kernel(x_ref, mod_ref, os_ref, h_ref, z_ref, mg_ref, slg_ref, slb_ref, sw_ref, sb_ref,
                cw_ref, cb_ref, clg_ref, clb_ref, wo_ref, out_ref, *scratch, seq, tm, grid_mode):
    t = pl.program_id(1)
    r0 = pl.multiple_of(t * tm, tm)
    rows = pl.ds(r0, tm)
    half = CONV_WIDTH // 2
    pad = CONV_R * GRID_W

    if grid_mode:
        (ycol_ref,) = scratch

        @pl.when(t == 0)
        def _():
            ycol_ref[0:pad, :] = jnp.zeros((pad, half), F32)
            ycol_ref[pad + seq:pad + seq + pad, :] = jnp.zeros((pad, half), F32)
            ycol_ref[pad:pad + seq, :] = z_ref[0, :, half:CONV_WIDTH] * _sigmoid(
                z_ref[0, :, CONV_WIDTH + half:2 * CONV_WIDTH])

        pos = lax.broadcasted_iota(jnp.int32, (tm, half), 0) & (GRID_W - 1)
        y_row = z_ref[0, rows, 0:half] * _sigmoid(z_ref[0, rows, CONV_WIDTH:CONV_WIDTH + half])
        conv_r = _shift_conv(y_row, cw_ref, 0, GRID_W, pos)
        conv_c = jnp.zeros((tm, half), F32)
        for j in range(CONV_K):
            start = pl.multiple_of(r0 + GRID_W * j, GRID_W)
            conv_c = conv_c + cw_ref[j:j + 1, half:CONV_WIDTH] * ycol_ref[pl.ds(start, tm), :]
        conv = jnp.concatenate([conv_r, conv_c], axis=1)
    else:
        pos = lax.broadcasted_iota(jnp.int32, (tm, CONV_WIDTH), 0)
        y = z_ref[0, :, 0:CONV_WIDTH] * _sigmoid(z_ref[0, :, CONV_WIDTH:2 * CONV_WIDTH])
        conv = _shift_conv(y, cw_ref, 0, tm, pos)
    y_c = _silu(_layernorm(conv + cb_ref[...], clg_ref[...], clb_ref[...]))

    zs = jax.nn.gelu(os_ref[0, :, MLSTM_WIDTH:MLSTM_WIDTH + 2 * SGU_WIDTH])
    u = zs[:, 0:SGU_WIDTH]
    vn = _layernorm(zs[:, SGU_WIDTH:2 * SGU_WIDTH], slg_ref[...], slb_ref[...]).astype(BF16)
    lane_group = lax.broadcasted_iota(jnp.int32, (CHUNK, SGU_WIDTH), 1) // (SGU_WIDTH // SGU_GROUPS)
    mixed = []
    for ch in range(tm // CHUNK):
        vc = vn[ch * CHUNK:(ch + 1) * CHUNK]
        m = sb_ref[...]
        for g in range(SGU_GROUPS):
            m = m + jnp.where(lane_group == g, jnp.dot(sw_ref[g], vc, preferred_element_type=F32), 0.0)
        mixed.append(m)
    y_b = u * jnp.concatenate(mixed, axis=0)

    y_a = h_ref[0] * mg_ref[...] * _sigmoid(os_ref[0, :, 0:MLSTM_WIDTH])

    o_a, o_b = MLSTM_WIDTH, MLSTM_WIDTH + SGU_WIDTH
    proj = jnp.dot(y_a.astype(BF16), wo_ref[0:o_a, :], preferred_element_type=F32)
    proj = proj + jnp.dot(y_b.astype(BF16), wo_ref[o_a:o_b, :], preferred_element_type=F32)
    proj = proj + jnp.dot(y_c.astype(BF16), wo_ref[o_b:, :], preferred_element_type=F32)
    out_ref[0] = x_ref[0] + mod_ref[0, 2:3, :] * proj


def _mix(x, mod, mod_row, rest, h, mg, slg, slb, sw, sbias, cw, cb, clg, clb, wo, tm, grid_mode):
    bsz, seq, _ = x.shape
    const2 = lambda b, t: (0, 0)
    const3 = lambda b, t: (0, 0, 0)
    scratch = []
    if grid_mode:
        scratch.append(pltpu.VMEM((seq + 2 * CONV_R * GRID_W, CONV_WIDTH // 2), F32))
    return pl.pallas_call(
        functools.partial(_mix_kernel, seq=seq, tm=tm, grid_mode=grid_mode),
        grid=(bsz, seq // tm),
        in_specs=[
            pl.BlockSpec((1, tm, D_MODEL), lambda b, t: (b, t, 0)),
            pl.BlockSpec((1, 6, D_MODEL), mod_row),
            pl.BlockSpec((1, tm, MLSTM_WIDTH + 2 * SGU_WIDTH), lambda b, t: (b, t, 0)),
            pl.BlockSpec((1, tm, MLSTM_WIDTH), lambda b, t: (b, t, 0)),
            pl.BlockSpec((1, seq, 2 * CONV_WIDTH), lambda b, t: (b, 0, 2)),
            pl.BlockSpec((1, MLSTM_WIDTH), const2),
            pl.BlockSpec((1, SGU_WIDTH), const2),
            pl.BlockSpec((1, SGU_WIDTH), const2),
            pl.BlockSpec((SGU_GROUPS, CHUNK, CHUNK), const3),
            pl.BlockSpec((CHUNK, SGU_WIDTH), const2),
            pl.BlockSpec((CONV_K, CONV_WIDTH), const2),
            pl.BlockSpec((1, CONV_WIDTH), const2),
            pl.BlockSpec((1, CONV_WIDTH), const2),
            pl.BlockSpec((1, CONV_WIDTH), const2),
            pl.BlockSpec((D_MODEL, D_MODEL), const2),
        ],
        out_specs=pl.BlockSpec((1, tm, D_MODEL), lambda b, t: (b, t, 0)),
        out_shape=jax.ShapeDtypeStruct((bsz, seq, D_MODEL), F32),
        scratch_shapes=scratch,
        compiler_params=_params("parallel", "arbitrary"),
        name="mix",
    )(x, mod, rest, h, rest, mg, slg, slb, sw, sbias, cw, cb, clg, clb, wo)


def _ffn_kernel(x_ref, mod_ref, ng_ref, wgu_ref, wd_ref, fg_ref, out_ref, *, final):
    x = x_ref[0]
    xm = (_rmsnorm(x, ng_ref[...]) * (1.0 + mod_ref[0, 4:5, :]) + mod_ref[0, 3:4, :]).astype(BF16)
    acc = jnp.zeros(x.shape, F32)
    for ch in range(FFN_HIDDEN // FFN_CHUNK):
        lo = ch * FFN_CHUNK
        gate = jnp.dot(xm, wgu_ref[:, lo:lo + FFN_CHUNK], preferred_element_type=F32)
        up = jnp.dot(xm, wgu_ref[:, FFN_HIDDEN + lo:FFN_HIDDEN + lo + FFN_CHUNK], preferred_element_type=F32)
        act = (_silu(gate) * up).astype(BF16)
        acc = acc + jnp.dot(act, wd_ref[lo:lo + FFN_CHUNK, :], preferred_element_type=F32)
    y = x + mod_ref[0, 5:6, :] * acc
    if final:
        y = _rmsnorm(y, fg_ref[...])
    out_ref[0] = y


def _ffn(x, mod, mod_row, ng, wgu, wd, fg, tm, final):
    bsz, seq, _ = x.shape
    const = lambda b, t: (0, 0)
    return pl.pallas_call(
        functools.partial(_ffn_kernel, final=final),
        grid=(bsz, seq // tm),
        in_specs=[
            pl.BlockSpec((1, tm, D_MODEL), lambda b, t: (b, t, 0)),
            pl.BlockSpec((1, 6, D_MODEL), mod_row),
            pl.BlockSpec((1, D_MODEL), const),
            pl.BlockSpec((D_MODEL, 2 * FFN_HIDDEN), const),
            pl.BlockSpec((FFN_HIDDEN, D_MODEL), const),
            pl.BlockSpec((1, D_MODEL), const),
        ],
        out_specs=pl.BlockSpec((1, tm, D_MODEL), lambda b, t: (b, t, 0)),
        out_shape=jax.ShapeDtypeStruct((bsz, seq, D_MODEL), F32),
        compiler_params=_params("parallel", "parallel"),
        name="ffn",
    )(x, mod, ng, wgu, wd, fg)


def _gate_weights(w_in_l, b_in_l):
    wg = w_in_l[:, GATE_OFF:SGU_OFF]
    bg = b_in_l[GATE_OFF:SGU_OFF]
    i_cols = jnp.concatenate([wg[:, 0:HEADS], wg[:, 2 * HEADS:3 * HEADS]], axis=1)
    f_cols = jnp.concatenate([wg[:, HEADS:2 * HEADS], wg[:, 3 * HEADS:4 * HEADS]], axis=1)
    padw = ((0, 0), (0, LANES - UNITS))
    w = jnp.concatenate([jnp.pad(i_cols, padw), jnp.pad(f_cols, padw)], axis=1)
    b_i = jnp.concatenate([bg[0:HEADS], bg[2 * HEADS:3 * HEADS]])
    b_f = jnp.concatenate([bg[HEADS:2 * HEADS], bg[3 * HEADS:4 * HEADS]])
    b = jnp.concatenate([jnp.pad(b_i, (0, LANES - UNITS)), jnp.pad(b_f, (0, LANES - UNITS))])[None, :]
    w_hi = w.astype(BF16)
    w_lo = (w - w_hi.astype(F32)).astype(BF16)
    return w_hi, w_lo, b


def kernel(x, c, ctx, c_ctx, w_mod, b_mod, norm1_g, w_in, b_in, mlstm_g, sgu_ln_g, sgu_ln_b, sgu_w, sgu_b,
           conv_w, conv_b, conv_ln_g, conv_ln_b, w_out, norm2_g, w_gu, w_down, final_g):
    bsz = x.shape[0]
    lc = ctx.shape[1]
    assert bsz + 1 <= MOD_ROWS
    c_all = jnp.zeros((MOD_ROWS, D_MODEL), F32).at[:bsz].set(c).at[bsz].set(c_ctx)
    mod_all = _modulation(c_all, w_mod, b_mod).reshape(DEPTH, MOD_ROWS, 6, D_MODEL)
    row_x = lambda b, t: (b, 0, 0)
    row_c = lambda b, t: (bsz, 0, 0)
    tm_x, tm_c = 512, lc

    xc = ctx
    for l in range(DEPTH):
        last = l == DEPTH - 1
        mod = mod_all[l]
        w_main = jnp.concatenate([w_in[l][:, :GATE_OFF], w_in[l][:, SGU_OFF:]], axis=1).astype(BF16)
        b_main = jnp.concatenate([b_in[l][:GATE_OFF], b_in[l][SGU_OFF:]])[None, :]
        wg_hi, wg_lo, b_gate = _gate_weights(w_in[l], b_in[l])
        ng1, ng2 = norm1_g[l][None, :], norm2_g[l][None, :]
        mix_w = (mlstm_g[l][None, :], sgu_ln_g[l][None, :], sgu_ln_b[l][None, :], sgu_w[l].astype(BF16),
                 jnp.repeat(sgu_b[l].T, SGU_WIDTH // SGU_GROUPS, axis=1),
                 conv_w[l], conv_b[l][None, :], conv_ln_g[l][None, :], conv_ln_b[l][None, :],
                 w_out[l].astype(BF16))
        wgu, wd = w_gu[l].astype(BF16), w_down[l].astype(BF16)
        fg = final_g[None, :]

        qk_x, kt_x, vt_x, rest_x, rows_x, ecol_x = _inproj(
            x, mod, row_x, ng1, w_main, b_main, wg_hi, wg_lo, b_gate, tm_x)
        qk_c, kt_c, vt_c, rest_c, rows_c, ecol_c = _inproj(
            xc, mod, row_c, ng1, w_main, b_main, wg_hi, wg_lo, b_gate, tm_c)
        hs = _mlstm((qk_c, kt_c, vt_c, rows_c, ecol_c), (qk_x, kt_x, vt_x, rows_x, ecol_x), ctx_out=not last)
        x = _mix(x, mod, row_x, rest_x, hs[-1], *mix_w, tm=tm_x, grid_mode=True)
        x = _ffn(x, mod, row_x, ng2, wgu, wd, fg, tm_x, final=last)
        if not last:
            xc = _mix(xc, mod, row_c, rest_c, hs[0], *mix_w, tm=tm_c, grid_mode=False)
            xc = _ffn(xc, mod, row_c, ng2, wgu, wd, fg, tm_c, final=False)
    return x
```

```python
import functools

import jax
import jax.numpy as jnp
from jax import lax
from jax.experimental import pallas as pl
from jax.experimental.pallas import tpu as pltpu

D_MODEL = 1024
DEPTH = 2
GRID_W = 64
EPS = 1e-6

HEADS = 4
HEAD_DIM = 128
MLSTM_WIDTH = HEADS * HEAD_DIM
CHUNK = 128
UNITS = 2 * HEADS
SGU_GROUPS = 4
SGU_WIDTH = 256
CONV_WIDTH = 256
CONV_K = 31
CONV_R = CONV_K // 2
FFN_HIDDEN = 2816
FFN_CHUNK = 256

GATE_OFF = 4 * MLSTM_WIDTH
N_GATES = 4 * HEADS
SGU_OFF = GATE_OFF + N_GATES
QKV_COLS = 3 * MLSTM_WIDTH
QK_COLS = 2 * MLSTM_WIDTH
REST_COLS = MLSTM_WIDTH + 2 * SGU_WIDTH + 2 * CONV_WIDTH
MAIN_COLS = QKV_COLS + REST_COLS
LANES = 128
SUBLANES = 8
BF16_ROWS = 16
MOD_ROWS = 24
N_ROWQ = 5
AUG = HEAD_DIM + BF16_ROWS

VMEM_LIMIT = 56 * 1024 * 1024

F32 = jnp.float32
BF16 = jnp.bfloat16
HIGHEST = lax.Precision.HIGHEST
NT = (((1,), (1,)), ((), ()))


def _sigmoid(t):
    return 0.5 * jnp.tanh(0.5 * t) + 0.5


def _silu(t):
    return t * _sigmoid(t)


def _log_sigmoid(t):
    return jnp.minimum(t, 0.0) - jnp.log1p(jnp.exp(-jnp.abs(t)))


def _layernorm(t, g, b):
    mu = jnp.mean(t, axis=-1, keepdims=True)
    d = t - mu
    var = jnp.mean(d * d, axis=-1, keepdims=True)
    return d * lax.rsqrt(var + EPS) * g + b


def _rmsnorm(t, g):
    return t * lax.rsqrt(jnp.mean(t * t, axis=-1, keepdims=True) + EPS) * g


def _params(*sem):
    return pltpu.CompilerParams(dimension_semantics=sem, vmem_limit_bytes=VMEM_LIMIT)


def _mod_kernel(c_ref, w_ref, b_ref, o_ref):
    a = _silu(c_ref[...])
    o_ref[0] = jnp.dot(a, w_ref[0], precision=HIGHEST, preferred_element_type=F32) + b_ref[0]


def _modulation(c_all, w_mod, b_mod):
    tn = 1536
    n = w_mod.shape[-1]
    return pl.pallas_call(
        _mod_kernel,
        grid=(DEPTH, n // tn),
        in_specs=[
            pl.BlockSpec((MOD_ROWS, D_MODEL), lambda l, j: (0, 0)),
            pl.BlockSpec((1, D_MODEL, tn), lambda l, j: (l, 0, j)),
            pl.BlockSpec((1, 1, tn), lambda l, j: (l, 0, j)),
        ],
        out_specs=pl.BlockSpec((1, MOD_ROWS, tn), lambda l, j: (l, 0, j)),
        out_shape=jax.ShapeDtypeStruct((DEPTH, MOD_ROWS, n), F32),
        compiler_params=_params("parallel", "parallel"),
        name="modulation",
    )(c_all, w_mod, b_mod.reshape(DEPTH, 1, n))


def _lane_scan(x, op, lane, forward):
    n = x.shape[1]
    k = 1
    while k < n:
        if forward:
            x = jnp.where(lane >= k, op(x, pltpu.roll(x, k, axis=1)), x)
        else:
            x = jnp.where(lane < n - k, op(x, pltpu.roll(x, n - k, axis=1)), x)
        k *= 2
    return x


def _inproj_kernel(x_ref, mod_ref, ng_ref, w_ref, b_ref, wg_ref, bg_ref,
                   k_ref, qt_ref, kt_ref, vt_ref, rest_ref, rows_ref, ecol_ref, *, tm):
    x = x_ref[0]
    xm = _rmsnorm(x, ng_ref[...]) * (1.0 + mod_ref[0, 1:2, :]) + mod_ref[0, 0:1, :]
    xh = xm.astype(BF16)
    xl = (xm - xh.astype(F32)).astype(BF16)
    p = jnp.dot(xh, w_ref[...], preferred_element_type=F32) + b_ref[...]
    pq = p[:, 0:MLSTM_WIDTH]
    pk = p[:, MLSTM_WIDTH:QK_COLS] * (HEAD_DIM ** -0.5)
    pv = p[:, QK_COLS:QKV_COLS]
    k_ref[0] = pk.astype(BF16)
    rest_ref[0] = p[:, QKV_COLS:MAIN_COLS]

    g_hi = jnp.dot(xh, wg_ref[...], preferred_element_type=F32)
    g_lo = jnp.dot(xl, wg_ref[...], preferred_element_type=F32)
    g = g_hi[:, 0:LANES] + g_hi[:, LANES:2 * LANES] + g_lo[:, 0:LANES] + bg_ref[...]
    lane = lax.broadcasted_iota(jnp.int32, (UNITS, CHUNK), 1)
    fwd_row = lax.broadcasted_iota(jnp.int32, (UNITS, CHUNK), 0) < HEADS
    pad_rows = jnp.zeros((CHUNK - UNITS, CHUNK), F32)
    for ch in range(tm // CHUNK):
        sl = slice(ch * CHUNK, (ch + 1) * CHUNK)
        for head in range(HEADS):
            hs = slice(head * HEAD_DIM, (head + 1) * HEAD_DIM)
            qt_ref[0, ch, hs, :] = pq[sl, hs].T.astype(BF16)
            kt_ref[0, ch, hs, :] = pk[sl, hs].T.astype(BF16)
            vt_ref[0, ch, hs, :] = pv[sl, hs].T.astype(BF16)
        gt = g[sl].T
        it = gt[0:UNITS]
        lsf = _log_sigmoid(gt[UNITS:2 * UNITS])
        bt = jnp.where(fwd_row, _lane_scan(lsf, jnp.add, lane, True), _lane_scan(lsf, jnp.add, lane, False))
        et = it - bt
        cme = jnp.where(fwd_row, _lane_scan(et, jnp.maximum, lane, True),
                        _lane_scan(et, jnp.maximum, lane, False))
        b_last = jnp.where(fwd_row, jnp.broadcast_to(bt[:, CHUNK - 1:CHUNK], bt.shape),
                           jnp.broadcast_to(bt[:, 0:1], bt.shape))
        cme_last = jnp.broadcast_to(jnp.max(et, axis=1, keepdims=True), et.shape)
        rows_ref[0, ch] = jnp.concatenate([et, cme, bt, b_last, cme_last], axis=0)
        ecol_ref[0, sl, :] = jnp.concatenate([et, pad_rows], axis=0).T


def _inproj(x, mod, mod_row, ng, w_main, b_main, w_gate, b_gate, tm):
    bsz, seq, _ = x.shape
    nch = seq // CHUNK
    tch = tm // CHUNK
    const = lambda b, t: (0, 0)
    return pl.pallas_call(
        functools.partial(_inproj_kernel, tm=tm),
        grid=(bsz, seq // tm),
        in_specs=[
            pl.BlockSpec((1, tm, D_MODEL), lambda b, t: (b, t, 0)),
            pl.BlockSpec((1, 6, D_MODEL), mod_row),
            pl.BlockSpec((1, D_MODEL), const),
            pl.BlockSpec((D_MODEL, MAIN_COLS), const),
            pl.BlockSpec((1, MAIN_COLS), const),
            pl.BlockSpec((D_MODEL, 2 * LANES), const),
            pl.BlockSpec((1, LANES), const),
        ],
        out_specs=[
            pl.BlockSpec((1, tm, MLSTM_WIDTH), lambda b, t: (b, t, 0)),
            pl.BlockSpec((1, tch, MLSTM_WIDTH, CHUNK), lambda b, t: (b, t, 0, 0)),
            pl.BlockSpec((1, tch, MLSTM_WIDTH, CHUNK), lambda b, t: (b, t, 0, 0)),
            pl.BlockSpec((1, tch, MLSTM_WIDTH, CHUNK), lambda b, t: (b, t, 0, 0)),
            pl.BlockSpec((1, tm, REST_COLS), lambda b, t: (b, t, 0)),
            pl.BlockSpec((1, tch, N_ROWQ * UNITS, CHUNK), lambda b, t: (b, t, 0, 0)),
            pl.BlockSpec((1, tm, LANES), lambda b, t: (b, t, 0)),
        ],
        out_shape=[
            jax.ShapeDtypeStruct((bsz, seq, MLSTM_WIDTH), BF16),
            jax.ShapeDtypeStruct((bsz, nch, MLSTM_WIDTH, CHUNK), BF16),
            jax.ShapeDtypeStruct((bsz, nch, MLSTM_WIDTH, CHUNK), BF16),
            jax.ShapeDtypeStruct((bsz, nch, MLSTM_WIDTH, CHUNK), BF16),
            jax.ShapeDtypeStruct((bsz, seq, REST_COLS), F32),
            jax.ShapeDtypeStruct((bsz, nch, N_ROWQ * UNITS, CHUNK), F32),
            jax.ShapeDtypeStruct((bsz, seq, LANES), F32),
        ],
        compiler_params=_params("parallel", "parallel"),
        name="inproj",
    )(x, mod, ng, w_main, b_main, w_gate, b_gate)


def _mlstm_kernel(kc_ref, qtc_ref, ktc_ref, vtc_ref, rowc_ref, ecc_ref,
                  kx_ref, qtx_ref, ktx_ref, vtx_ref, rowx_ref, ecx_ref, *rest, ctx_out, n_ctx, n_x):
    if ctx_out:
        hc_ref, hx_ref, ct_ref, m_ref, part_ref = rest
    else:
        hx_ref, ct_ref, m_ref, part_ref = rest
        hc_ref = None
    ct_ref[...] = jnp.zeros(ct_ref.shape, F32)
    m_ref[...] = jnp.zeros(m_ref.shape, F32)
    r = lax.broadcasted_iota(jnp.int32, (CHUNK, CHUNK), 0)
    c = lax.broadcasted_iota(jnp.int32, (CHUNK, CHUNK), 1)
    masks = (r <= c, r >= c)
    ones = jnp.ones((BF16_ROWS, CHUNK), BF16)
    zeros = jnp.zeros((CHUNK, CHUNK), BF16)

    def block_diag(a, b):
        return jnp.concatenate([jnp.concatenate([a, zeros], axis=1),
                                jnp.concatenate([zeros, b], axis=1)], axis=0)

    def pair(refs, j, hp, direction, out_ref, mode, slot):
        k_ref, qt_ref, kt_ref, vt_ref, row_ref, ec_ref = refs
        heads = (2 * hp, 2 * hp + 1)
        units = tuple(direction * HEADS + h for h in heads)
        sp = direction * (HEADS // 2) + hp
        r0 = j * CHUNK if isinstance(j, int) else pl.multiple_of(j * CHUNK, CHUNK)
        toks = pl.ds(r0, CHUNK)
        hsl = [slice(h * HEAD_DIM, (h + 1) * HEAD_DIM) for h in heads]

        def row(q, u):
            return row_ref[0, j, q * UNITS + u:q * UNITS + u + 1, :]

        e, cme, b, b_last, cme_last = ([row(q, u) for u in units] for q in range(N_ROWQ))
        m_old = [m_ref[u, 0:1, :] for u in units]
        ct_old = ct_ref[sp]
        vt_aug = [jnp.concatenate([vt_ref[0, j, s, :], ones], axis=0) for s in hsl]

        g_last = [jnp.maximum(m_old[i], cme_last[i]) for i in range(2)]
        kst = [(kt_ref[0, j, hsl[i], :].astype(F32) * jnp.exp(e[i] - cme_last[i])).astype(BF16) for i in range(2)]
        upd = lax.dot_general(jnp.concatenate(vt_aug, axis=1), block_diag(kst[0], kst[1]), NT,
                              preferred_element_type=F32)
        keep = jnp.concatenate([jnp.exp(m_old[i] - g_last[i]) for i in range(2)], axis=1)
        gain = jnp.concatenate([jnp.exp(cme_last[i] - g_last[i]) for i in range(2)], axis=1)
        ct_ref[sp] = keep * ct_old + gain * upd
        for i in range(2):
            m_ref[units[i]] = jnp.broadcast_to(b_last[i] + g_last[i], m_ref.shape[1:])

        if mode is None:
            return
        qt = [qt_ref[0, j, s, :] for s in hsl]
        k2 = k_ref[0, toks, 2 * hp * HEAD_DIM:(2 * hp + 2) * HEAD_DIM]
        qkt = jnp.dot(k2, block_diag(qt[0], qt[1]), preferred_element_type=F32)
        ct_b = ct_old.astype(BF16)
        for i in range(2):
            u, cols = units[i], hsl[i]
            g = jnp.maximum(m_old[i], cme[i])
            dl = jnp.where(masks[direction], ec_ref[0, toks, u:u + 1] - g, -jnp.inf)
            st = (qkt[:, i * CHUNK:(i + 1) * CHUNK] * jnp.exp(dl)).astype(BF16)
            qs = qt[i] * jnp.exp(m_old[i] - g).astype(BF16)
            res = jnp.dot(jnp.concatenate([vt_aug[i], ct_b[:, i * HEAD_DIM:(i + 1) * HEAD_DIM]], axis=1),
                          jnp.concatenate([st, qs], axis=0), preferred_element_type=F32)
            den = jnp.maximum(jnp.abs(res[HEAD_DIM:HEAD_DIM + 1, :]), jnp.exp(-(b[i] + g)))
            ht = res[0:HEAD_DIM, :] * (1.0 / den)
            if mode == "first":
                part_ref[direction, slot, cols, :] = ht
            else:
                tot = part_ref[1 - direction, slot, cols, :] + ht
                hn = tot * lax.rsqrt(jnp.mean(tot * tot, axis=0, keepdims=True) + EPS)
                out_ref[0, toks, cols] = hn.T

    def step(refs, j, n, out_ref, mode):
        slot = None if mode is None else (j if mode == "first" else n - 1 - j)
        for hp in range(HEADS // 2):
            pair(refs, j, hp, 0, out_ref, mode, slot)
            pair(refs, n - 1 - j, hp, 1, out_ref, mode, slot)

    crefs = (kc_ref, qtc_ref, ktc_ref, vtc_ref, rowc_ref, ecc_ref)
    xrefs = (kx_ref, qtx_ref, ktx_ref, vtx_ref, rowx_ref, ecx_ref)
    for j in range(n_ctx):
        mode = None if not ctx_out else ("first" if j < n_ctx // 2 else "second")
        step(crefs, j, n_ctx, hc_ref, mode)

    def body(j, carry, mode):
        step(xrefs, j, n_x, hx_ref, mode)
        return carry

    lax.fori_loop(0, n_x // 2, functools.partial(body, mode="first"), 0)
    lax.fori_loop(n_x // 2, n_x, functools.partial(body, mode="second"), 0)


def _mlstm(ctx_in, x_in, ctx_out):
    bsz, lc, _ = ctx_in[0].shape
    lx = x_in[0].shape[1]
    n_ctx, n_x = lc // CHUNK, lx // CHUNK
    assert n_ctx % 2 == 0 and n_x % 2 == 0
    i3 = lambda b: (b, 0, 0)
    i4 = lambda b: (b, 0, 0, 0)

    def specs(seq, nch):
        return [
            pl.BlockSpec((1, seq, MLSTM_WIDTH), i3),
            pl.BlockSpec((1, nch, MLSTM_WIDTH, CHUNK), i4),
            pl.BlockSpec((1, nch, MLSTM_WIDTH, CHUNK), i4),
            pl.BlockSpec((1, nch, MLSTM_WIDTH, CHUNK), i4),
            pl.BlockSpec((1, nch, N_ROWQ * UNITS, CHUNK), i4),
            pl.BlockSpec((1, seq, LANES), i3),
        ]

    out_specs = [pl.BlockSpec((1, lx, MLSTM_WIDTH), i3)]
    out_shape = [jax.ShapeDtypeStruct((bsz, lx, MLSTM_WIDTH), F32)]
    if ctx_out:
        out_specs.insert(0, pl.BlockSpec((1, lc, MLSTM_WIDTH), i3))
        out_shape.insert(0, jax.ShapeDtypeStruct((bsz, lc, MLSTM_WIDTH), F32))
    return pl.pallas_call(
        functools.partial(_mlstm_kernel, ctx_out=ctx_out, n_ctx=n_ctx, n_x=n_x),
        grid=(bsz,),
        in_specs=specs(lc, n_ctx) + specs(lx, n_x),
        out_specs=out_specs,
        out_shape=out_shape,
        scratch_shapes=[
            pltpu.VMEM((UNITS // 2, AUG, 2 * HEAD_DIM), F32),
            pltpu.VMEM((UNITS, SUBLANES, LANES), F32),
            pltpu.VMEM((2, max(n_ctx, n_x) // 2, MLSTM_WIDTH, CHUNK), F32),
        ],
        compiler_params=_params("parallel"),
        name="mlstm",
    )(*ctx_in, *x_in)


def _shift_conv(y, w_ref, lo, width, pos):
    n = y.shape[0]
    acc = jnp.zeros(y.shape, F32)
    for j in range(CONV_K):
        d = j - CONV_R
        w = w_ref[j:j + 1, lo:lo + y.shape[1]]
        if d == 0:
            acc = acc + w * y
        else:
            sh = pltpu.roll(y, (-d) % n, axis=0)
            valid = (pos >= -d) if d < 0 else (pos < width - d)
            acc = acc + w * jnp.where(valid, sh, 0.0)
    return acc


def _mix_kernel(x_ref, mod_ref, os_ref, h_ref, z_ref, mg_ref, slg_ref, slb_ref, sw_ref, sb_ref,
                cw_ref, cb_ref, clg_ref, clb_ref, wo_ref, out_ref, *scratch, seq, tm, grid_mode):
    t = pl.program_id(1)
    r0 = pl.multiple_of(t * tm, tm)
    rows = pl.ds(r0, tm)
    half = CONV_WIDTH // 2
    pad = CONV_R * GRID_W

    if grid_mode:
        (ycol_ref,) = scratch

        @pl.when(t == 0)
        def _():
            ycol_ref[0:pad, :] = jnp.zeros((pad, half), F32)
            ycol_ref[pad + seq:pad + seq + pad, :] = jnp.zeros((pad, half), F32)
            ycol_ref[pad:pad + seq, :] = z_ref[0, :, half:CONV_WIDTH] * _sigmoid(
                z_ref[0, :, CONV_WIDTH + half:2 * CONV_WIDTH])

        pos = lax.broadcasted_iota(jnp.int32, (tm, half), 0) & (GRID_W - 1)
        y_row = z_ref[0, rows, 0:half] * _sigmoid(z_ref[0, rows, CONV_WIDTH:CONV_WIDTH + half])
        conv_r = _shift_conv(y_row, cw_ref, 0, GRID_W, pos)
        conv_c = jnp.zeros((tm, half), F32)
        for j in range(CONV_K):
            start = pl.multiple_of(r0 + GRID_W * j, GRID_W)
            conv_c = conv_c + cw_ref[j:j + 1, half:CONV_WIDTH] * ycol_ref[pl.ds(start, tm), :]
        conv = jnp.concatenate([conv_r, conv_c], axis=1)
    else:
        pos = lax.broadcasted_iota(jnp.int32, (tm, CONV_WIDTH), 0)
        y = z_ref[0, :, 0:CONV_WIDTH] * _sigmoid(z_ref[0, :, CONV_WIDTH:2 * CONV_WIDTH])
        conv = _shift_conv(y, cw_ref, 0, tm, pos)
    y_c = _silu(_layernorm(conv + cb_ref[...], clg_ref[...], clb_ref[...]))

    zs = jax.nn.gelu(os_ref[0, :, MLSTM_WIDTH:MLSTM_WIDTH + 2 * SGU_WIDTH])
    u = zs[:, 0:SGU_WIDTH]
    vn = _layernorm(zs[:, SGU_WIDTH:2 * SGU_WIDTH], slg_ref[...], slb_ref[...]).astype(BF16)
    lane_group = lax.broadcasted_iota(jnp.int32, (CHUNK, SGU_WIDTH), 1) // (SGU_WIDTH // SGU_GROUPS)
    mixed = []
    for ch in range(tm // CHUNK):
        vc = vn[ch * CHUNK:(ch + 1) * CHUNK]
        m = sb_ref[...]
        for g in range(SGU_GROUPS):
            m = m + jnp.where(lane_group == g, jnp.dot(sw_ref[g], vc, preferred_element_type=F32), 0.0)
        mixed.append(m)
    y_b = u * jnp.concatenate(mixed, axis=0)

    y_a = h_ref[0] * mg_ref[...] * _sigmoid(os_ref[0, :, 0:MLSTM_WIDTH])

    o_a, o_b = MLSTM_WIDTH, MLSTM_WIDTH + SGU_WIDTH
    proj = jnp.dot(y_a.astype(BF16), wo_ref[0:o_a, :], preferred_element_type=F32)
    proj = proj + jnp.dot(y_b.astype(BF16), wo_ref[o_a:o_b, :], preferred_element_type=F32)
    proj = proj + jnp.dot(y_c.astype(BF16), wo_ref[o_b:, :], preferred_element_type=F32)
    out_ref[0] = x_ref[0] + mod_ref[0, 2:3, :] * proj


def _mix(x, mod, mod_row, rest, h, mg, slg, slb, sw, sbias, cw, cb, clg, clb, wo, tm, grid_mode):
    bsz, seq, _ = x.shape
    const2 = lambda b, t: (0, 0)
    const3 = lambda b, t: (0, 0, 0)
    scratch = []
    if grid_mode:
        scratch.append(pltpu.VMEM((seq + 2 * CONV_R * GRID_W, CONV_WIDTH // 2), F32))
    return pl.pallas_call(
        functools.partial(_mix_kernel, seq=seq, tm=tm, grid_mode=grid_mode),
        grid=(bsz, seq // tm),
        in_specs=[
            pl.BlockSpec((1, tm, D_MODEL), lambda b, t: (b, t, 0)),
            pl.BlockSpec((1, 6, D_MODEL), mod_row),
            pl.BlockSpec((1, tm, MLSTM_WIDTH + 2 * SGU_WIDTH), lambda b, t: (b, t, 0)),
            pl.BlockSpec((1, tm, MLSTM_WIDTH), lambda b, t: (b, t, 0)),
            pl.BlockSpec((1, seq, 2 * CONV_WIDTH), lambda b, t: (b, 0, 2)),
            pl.BlockSpec((1, MLSTM_WIDTH), const2),
            pl.BlockSpec((1, SGU_WIDTH), const2),
            pl.BlockSpec((1, SGU_WIDTH), const2),
            pl.BlockSpec((SGU_GROUPS, CHUNK, CHUNK), const3),
            pl.BlockSpec((CHUNK, SGU_WIDTH), const2),
            pl.BlockSpec((CONV_K, CONV_WIDTH), const2),
            pl.BlockSpec((1, CONV_WIDTH), const2),
            pl.BlockSpec((1, CONV_WIDTH), const2),
            pl.BlockSpec((1, CONV_WIDTH), const2),
            pl.BlockSpec((D_MODEL, D_MODEL), const2),
        ],
        out_specs=pl.BlockSpec((1, tm, D_MODEL), lambda b, t: (b, t, 0)),
        out_shape=jax.ShapeDtypeStruct((bsz, seq, D_MODEL), F32),
        scratch_shapes=scratch,
        compiler_params=_params("parallel", "arbitrary"),
        name="mix",
    )(x, mod, rest, h, rest, mg, slg, slb, sw, sbias, cw, cb, clg, clb, wo)


def _ffn_kernel(x_ref, mod_ref, ng_ref, wgu_ref, wd_ref, fg_ref, out_ref, *, final):
    x = x_ref[0]
    xm = (_rmsnorm(x, ng_ref[...]) * (1.0 + mod_ref[0, 4:5, :]) + mod_ref[0, 3:4, :]).astype(BF16)
    acc = jnp.zeros(x.shape, F32)
    for ch in range(FFN_HIDDEN // FFN_CHUNK):
        lo = ch * FFN_CHUNK
        gate = jnp.dot(xm, wgu_ref[:, lo:lo + FFN_CHUNK], preferred_element_type=F32)
        up = jnp.dot(xm, wgu_ref[:, FFN_HIDDEN + lo:FFN_HIDDEN + lo + FFN_CHUNK], preferred_element_type=F32)
        act = (_silu(gate) * up).astype(BF16)
        acc = acc + jnp.dot(act, wd_ref[lo:lo + FFN_CHUNK, :], preferred_element_type=F32)
    y = x + mod_ref[0, 5:6, :] * acc
    if final:
        y = _rmsnorm(y, fg_ref[...])
    out_ref[0] = y


def _ffn(x, mod, mod_row, ng, wgu, wd, fg, tm, final):
    bsz, seq, _ = x.shape
    const = lambda b, t: (0, 0)
    return pl.pallas_call(
        functools.partial(_ffn_kernel, final=final),
        grid=(bsz, seq // tm),
        in_specs=[
            pl.BlockSpec((1, tm, D_MODEL), lambda b, t: (b, t, 0)),
            pl.BlockSpec((1, 6, D_MODEL), mod_row),
            pl.BlockSpec((1, D_MODEL), const),
            pl.BlockSpec((D_MODEL, 2 * FFN_HIDDEN), const),
            pl.BlockSpec((FFN_HIDDEN, D_MODEL), const),
            pl.BlockSpec((1, D_MODEL), const),
        ],
        out_specs=pl.BlockSpec((1, tm, D_MODEL), lambda b, t: (b, t, 0)),
        out_shape=jax.ShapeDtypeStruct((bsz, seq, D_MODEL), F32),
        compiler_params=_params("parallel", "parallel"),
        name="ffn",
    )(x, mod, ng, wgu, wd, fg)


def _gate_weights(w_in_l, b_in_l):
    wg = w_in_l[:, GATE_OFF:SGU_OFF]
    bg = b_in_l[GATE_OFF:SGU_OFF]
    order = [wg[:, 0:HEADS], wg[:, 2 * HEADS:3 * HEADS], wg[:, HEADS:2 * HEADS], wg[:, 3 * HEADS:4 * HEADS]]
    w = jnp.pad(jnp.concatenate(order, axis=1), ((0, 0), (0, LANES - 2 * UNITS)))
    b_order = [bg[0:HEADS], bg[2 * HEADS:3 * HEADS], bg[HEADS:2 * HEADS], bg[3 * HEADS:4 * HEADS]]
    b = jnp.pad(jnp.concatenate(b_order), (0, LANES - 2 * UNITS))[None, :]
    w_hi = w.astype(BF16)
    w_lo = (w - w_hi.astype(F32)).astype(BF16)
    return jnp.concatenate([w_hi, w_lo], axis=1), b


def kernel(x, c, ctx, c_ctx, w_mod, b_mod, norm1_g, w_in, b_in, mlstm_g, sgu_ln_g, sgu_ln_b, sgu_w, sgu_b,
           conv_w, conv_b, conv_ln_g, conv_ln_b, w_out, norm2_g, w_gu, w_down, final_g):
    bsz = x.shape[0]
    lc = ctx.shape[1]
    assert bsz + 1 <= MOD_ROWS
    c_all = jnp.zeros((MOD_ROWS, D_MODEL), F32).at[:bsz].set(c).at[bsz].set(c_ctx)
    mod_all = _modulation(c_all, w_mod, b_mod).reshape(DEPTH, MOD_ROWS, 6, D_MODEL)
    row_x = lambda b, t: (b, 0, 0)
    row_c = lambda b, t: (bsz, 0, 0)
    tm_x, tm_c = 512, lc

    xc = ctx
    for l in range(DEPTH):
        last = l == DEPTH - 1
        mod = mod_all[l]
        w_main = jnp.concatenate([w_in[l][:, :GATE_OFF], w_in[l][:, SGU_OFF:]], axis=1).astype(BF16)
        b_main = jnp.concatenate([b_in[l][:GATE_OFF], b_in[l][SGU_OFF:]])[None, :]
        w_gate, b_gate = _gate_weights(w_in[l], b_in[l])
        ng1, ng2 = norm1_g[l][None, :], norm2_g[l][None, :]
        mix_w = (mlstm_g[l][None, :], sgu_ln_g[l][None, :], sgu_ln_b[l][None, :], sgu_w[l].astype(BF16),
                 jnp.repeat(sgu_b[l].T, SGU_WIDTH // SGU_GROUPS, axis=1),
                 conv_w[l], conv_b[l][None, :], conv_ln_g[l][None, :], conv_ln_b[l][None, :],
                 w_out[l].astype(BF16))
        wgu, wd = w_gu[l].astype(BF16), w_down[l].astype(BF16)
        fg = final_g[None, :]

        k_x, qt_x, kt_x, vt_x, rest_x, rows_x, ecol_x = _inproj(
            x, mod, row_x, ng1, w_main, b_main, w_gate, b_gate, tm_x)
        k_c, qt_c, kt_c, vt_c, rest_c, rows_c, ecol_c = _inproj(
            xc, mod, row_c, ng1, w_main, b_main, w_gate, b_gate, tm_c)
        hs = _mlstm((k_c, qt_c, kt_c, vt_c, rows_c, ecol_c), (k_x, qt_x, kt_x, vt_x, rows_x, ecol_x),
                    ctx_out=not last)
        x = _mix(x, mod, row_x, rest_x, hs[-1], *mix_w, tm=tm_x, grid_mode=True)
        x = _ffn(x, mod, row_x, ng2, wgu, wd, fg, tm_x, final=last)
        if not last:
            xc = _mix(xc, mod, row_c, rest_c, hs[0], *mix_w, tm=tm_c, grid_mode=False)
            xc = _ffn(xc, mod, row_c, ng2, wgu, wd, fg, tm_c, final=False)
    return x
```

```python
import functools

import jax
import jax.numpy as jnp
from jax import lax
from jax.experimental import pallas as pl
from jax.experimental.pallas import tpu as pltpu

D_MODEL = 1024
DEPTH = 2
GRID_W = 64
EPS = 1e-6

HEADS = 4
HEAD_DIM = 128
MLSTM_WIDTH = HEADS * HEAD_DIM
CHUNK = 128
UNITS = 2 * HEADS
SGU_GROUPS = 4
SGU_WIDTH = 256
CONV_WIDTH = 256
CONV_K = 31
CONV_R = CONV_K // 2
FFN_HIDDEN = 2816
FFN_CHUNK = 256

GATE_OFF = 4 * MLSTM_WIDTH
N_GATES = 4 * HEADS
SGU_OFF = GATE_OFF + N_GATES
QKV_COLS = 3 * MLSTM_WIDTH
QK_COLS = 2 * MLSTM_WIDTH
REST_COLS = MLSTM_WIDTH + 2 * SGU_WIDTH + 2 * CONV_WIDTH
MAIN_COLS = QKV_COLS + REST_COLS
LANES = 128
SUBLANES = 8
BF16_ROWS = 16
MOD_ROWS = 24
N_ROWQ = 5
AUG = HEAD_DIM + BF16_ROWS

VMEM_LIMIT = 56 * 1024 * 1024

F32 = jnp.float32
BF16 = jnp.bfloat16
HIGHEST = lax.Precision.HIGHEST
NT = (((1,), (1,)), ((), ()))


def _sigmoid(t):
    return 0.5 * jnp.tanh(0.5 * t) + 0.5


def _silu(t):
    return t * _sigmoid(t)


def _log_sigmoid(t):
    return jnp.minimum(t, 0.0) - jnp.log1p(jnp.exp(-jnp.abs(t)))


def _layernorm(t, g, b):
    mu = jnp.mean(t, axis=-1, keepdims=True)
    d = t - mu
    var = jnp.mean(d * d, axis=-1, keepdims=True)
    return d * lax.rsqrt(var + EPS) * g + b


def _rmsnorm(t, g):
    return t * lax.rsqrt(jnp.mean(t * t, axis=-1, keepdims=True) + EPS) * g


def _params(*sem):
    return pltpu.CompilerParams(dimension_semantics=sem, vmem_limit_bytes=VMEM_LIMIT)


def _mod_kernel(c_ref, w_ref, b_ref, o_ref):
    a = _silu(c_ref[...])
    o_ref[0] = jnp.dot(a, w_ref[0], precision=HIGHEST, preferred_element_type=F32) + b_ref[0]


def _modulation(c_all, w_mod, b_mod):
    tn = 1536
    n = w_mod.shape[-1]
    return pl.pallas_call(
        _mod_kernel,
        grid=(DEPTH, n // tn),
        in_specs=[
            pl.BlockSpec((MOD_ROWS, D_MODEL), lambda l, j: (0, 0)),
            pl.BlockSpec((1, D_MODEL, tn), lambda l, j: (l, 0, j)),
            pl.BlockSpec((1, 1, tn), lambda l, j: (l, 0, j)),
        ],
        out_specs=pl.BlockSpec((1, MOD_ROWS, tn), lambda l, j: (l, 0, j)),
        out_shape=jax.ShapeDtypeStruct((DEPTH, MOD_ROWS, n), F32),
        compiler_params=_params("parallel", "parallel"),
        name="modulation",
    )(c_all, w_mod, b_mod.reshape(DEPTH, 1, n))


def _lane_scan(x, op, lane, forward):
    n = x.shape[1]
    k = 1
    while k < n:
        if forward:
            x = jnp.where(lane >= k, op(x, pltpu.roll(x, k, axis=1)), x)
        else:
            x = jnp.where(lane < n - k, op(x, pltpu.roll(x, n - k, axis=1)), x)
        k *= 2
    return x


def _inproj_kernel(x_ref, mod_ref, ng_ref, w_ref, b_ref, wg_ref, bg_ref,
                   k_ref, qt_ref, kt_ref, vt_ref, rest_ref, rows_ref, ecol_ref, *, tm):
    x = x_ref[0]
    xm = _rmsnorm(x, ng_ref[...]) * (1.0 + mod_ref[0, 1:2, :]) + mod_ref[0, 0:1, :]
    xh = xm.astype(BF16)
    xl = (xm - xh.astype(F32)).astype(BF16)

    g_hi = jnp.dot(xh, wg_ref[...], preferred_element_type=F32)
    g_lo = jnp.dot(xl, wg_ref[...], preferred_element_type=F32)
    g = g_hi[:, 0:LANES] + g_hi[:, LANES:2 * LANES] + g_lo[:, 0:LANES] + bg_ref[...]
    lane = lax.broadcasted_iota(jnp.int32, (UNITS, CHUNK), 1)
    fwd_row = lax.broadcasted_iota(jnp.int32, (UNITS, CHUNK), 0) < HEADS
    pad_rows = jnp.zeros((CHUNK - UNITS, CHUNK), F32)
    for ch in range(tm // CHUNK):
        sl = slice(ch * CHUNK, (ch + 1) * CHUNK)
        gt = g[sl].T
        it = gt[0:UNITS]
        lsf = _log_sigmoid(gt[UNITS:2 * UNITS])
        bt = jnp.where(fwd_row, _lane_scan(lsf, jnp.add, lane, True), _lane_scan(lsf, jnp.add, lane, False))
        et = it - bt
        cme = jnp.where(fwd_row, _lane_scan(et, jnp.maximum, lane, True),
                        _lane_scan(et, jnp.maximum, lane, False))
        b_last = jnp.where(fwd_row, jnp.broadcast_to(bt[:, CHUNK - 1:CHUNK], bt.shape),
                           jnp.broadcast_to(bt[:, 0:1], bt.shape))
        cme_last = jnp.broadcast_to(jnp.max(et, axis=1, keepdims=True), et.shape)
        rows_ref[0, ch] = jnp.concatenate([et, cme, bt, b_last, cme_last], axis=0)
        ecol_ref[0, sl, :] = jnp.concatenate([et, pad_rows], axis=0).T

    p = jnp.dot(xh, w_ref[...], preferred_element_type=F32) + b_ref[...]
    pq = p[:, 0:MLSTM_WIDTH]
    pk = p[:, MLSTM_WIDTH:QK_COLS] * (HEAD_DIM ** -0.5)
    pv = p[:, QK_COLS:QKV_COLS]
    k_ref[0] = pk.astype(BF16)
    rest_ref[0] = p[:, QKV_COLS:MAIN_COLS]
    for ch in range(tm // CHUNK):
        sl = slice(ch * CHUNK, (ch + 1) * CHUNK)
        for head in range(HEADS):
            hs = slice(head * HEAD_DIM, (head + 1) * HEAD_DIM)
            qt_ref[0, ch, hs, :] = pq[sl, hs].T.astype(BF16)
            kt_ref[0, ch, hs, :] = pk[sl, hs].T.astype(BF16)
            vt_ref[0, ch, hs, :] = pv[sl, hs].T.astype(BF16)


def _inproj(x, mod, mod_row, ng, w_main, b_main, w_gate, b_gate, tm):
    bsz, seq, _ = x.shape
    nch = seq // CHUNK
    tch = tm // CHUNK
    const = lambda b, t: (0, 0)
    return pl.pallas_call(
        functools.partial(_inproj_kernel, tm=tm),
        grid=(bsz, seq // tm),
        in_specs=[
            pl.BlockSpec((1, tm, D_MODEL), lambda b, t: (b, t, 0)),
            pl.BlockSpec((1, 6, D_MODEL), mod_row),
            pl.BlockSpec((1, D_MODEL), const),
            pl.BlockSpec((D_MODEL, MAIN_COLS), const),
            pl.BlockSpec((1, MAIN_COLS), const),
            pl.BlockSpec((D_MODEL, 2 * LANES), const),
            pl.BlockSpec((1, LANES), const),
        ],
        out_specs=[
            pl.BlockSpec((1, tm, MLSTM_WIDTH), lambda b, t: (b, t, 0)),
            pl.BlockSpec((1, tch, MLSTM_WIDTH, CHUNK), lambda b, t: (b, t, 0, 0)),
            pl.BlockSpec((1, tch, MLSTM_WIDTH, CHUNK), lambda b, t: (b, t, 0, 0)),
            pl.BlockSpec((1, tch, MLSTM_WIDTH, CHUNK), lambda b, t: (b, t, 0, 0)),
            pl.BlockSpec((1, tm, REST_COLS), lambda b, t: (b, t, 0)),
            pl.BlockSpec((1, tch, N_ROWQ * UNITS, CHUNK), lambda b, t: (b, t, 0, 0)),
            pl.BlockSpec((1, tm, LANES), lambda b, t: (b, t, 0)),
        ],
        out_shape=[
            jax.ShapeDtypeStruct((bsz, seq, MLSTM_WIDTH), BF16),
            jax.ShapeDtypeStruct((bsz, nch, MLSTM_WIDTH, CHUNK), BF16),
            jax.ShapeDtypeStruct((bsz, nch, MLSTM_WIDTH, CHUNK), BF16),
            jax.ShapeDtypeStruct((bsz, nch, MLSTM_WIDTH, CHUNK), BF16),
            jax.ShapeDtypeStruct((bsz, seq, REST_COLS), F32),
            jax.ShapeDtypeStruct((bsz, nch, N_ROWQ * UNITS, CHUNK), F32),
            jax.ShapeDtypeStruct((bsz, seq, LANES), F32),
        ],
        compiler_params=_params("parallel", "parallel"),
        name="inproj",
    )(x, mod, ng, w_main, b_main, w_gate, b_gate)


def _mlstm_kernel(kc_ref, qtc_ref, ktc_ref, vtc_ref, rowc_ref, ecc_ref,
                  kx_ref, qtx_ref, ktx_ref, vtx_ref, rowx_ref, ecx_ref, *rest, ctx_out, n_ctx, n_x):
    if ctx_out:
        hc_ref, hx_ref, ct_ref, m_ref, part_ref = rest
    else:
        hx_ref, ct_ref, m_ref, part_ref = rest
        hc_ref = None
    ct_ref[...] = jnp.zeros(ct_ref.shape, F32)
    m_ref[...] = jnp.zeros(m_ref.shape, F32)
    r = lax.broadcasted_iota(jnp.int32, (CHUNK, CHUNK), 0)
    c = lax.broadcasted_iota(jnp.int32, (CHUNK, CHUNK), 1)
    masks = (r <= c, r >= c)
    ones = jnp.ones((BF16_ROWS, CHUNK), BF16)
    zeros = jnp.zeros((CHUNK, CHUNK), BF16)

    def block_diag(a, b):
        return jnp.concatenate([jnp.concatenate([a, zeros], axis=1),
                                jnp.concatenate([zeros, b], axis=1)], axis=0)

    def pair(refs, j, hp, direction, out_ref, mode, slot):
        k_ref, qt_ref, kt_ref, vt_ref, row_ref, ec_ref = refs
        heads = (2 * hp, 2 * hp + 1)
        units = tuple(direction * HEADS + h for h in heads)
        sp = direction * (HEADS // 2) + hp
        r0 = j * CHUNK if isinstance(j, int) else pl.multiple_of(j * CHUNK, CHUNK)
        toks = pl.ds(r0, CHUNK)
        hsl = [slice(h * HEAD_DIM, (h + 1) * HEAD_DIM) for h in heads]

        def row(q, u):
            return row_ref[0, j, q * UNITS + u:q * UNITS + u + 1, :]

        e, cme, b, b_last, cme_last = ([row(q, u) for u in units] for q in range(N_ROWQ))
        m_old = [m_ref[u, 0:1, :] for u in units]
        ct_old = ct_ref[sp]
        vt_aug = [jnp.concatenate([vt_ref[0, j, s, :], ones], axis=0) for s in hsl]

        if mode is not None:
            qt = [qt_ref[0, j, s, :] for s in hsl]
            k2 = k_ref[0, toks, 2 * hp * HEAD_DIM:(2 * hp + 2) * HEAD_DIM]
            qkt = jnp.dot(k2, block_diag(qt[0], qt[1]), preferred_element_type=F32)

        g_last = [jnp.maximum(m_old[i], cme_last[i]) for i in range(2)]
        kst = [(kt_ref[0, j, hsl[i], :].astype(F32) * jnp.exp(e[i] - cme_last[i])).astype(BF16) for i in range(2)]
        upd = lax.dot_general(jnp.concatenate(vt_aug, axis=1), block_diag(kst[0], kst[1]), NT,
                              preferred_element_type=F32)
        keep = jnp.concatenate([jnp.exp(m_old[i] - g_last[i]) for i in range(2)], axis=1)
        gain = jnp.concatenate([jnp.exp(cme_last[i] - g_last[i]) for i in range(2)], axis=1)
        ct_ref[sp] = keep * ct_old + gain * upd
        for i in range(2):
            m_ref[units[i]] = jnp.broadcast_to(b_last[i] + g_last[i], m_ref.shape[1:])

        yield
        if mode is None:
            return
        ct_b = ct_old.astype(BF16)
        for i in range(2):
            u, cols = units[i], hsl[i]
            g = jnp.maximum(m_old[i], cme[i])
            dl = jnp.where(masks[direction], ec_ref[0, toks, u:u + 1] - g, -jnp.inf)
            st = (qkt[:, i * CHUNK:(i + 1) * CHUNK] * jnp.exp(dl)).astype(BF16)
            qs = qt[i] * jnp.exp(m_old[i] - g).astype(BF16)
            res = jnp.dot(jnp.concatenate([vt_aug[i], ct_b[:, i * HEAD_DIM:(i + 1) * HEAD_DIM]], axis=1),
                          jnp.concatenate([st, qs], axis=0), preferred_element_type=F32)
            den = jnp.maximum(jnp.abs(res[HEAD_DIM:HEAD_DIM + 1, :]), jnp.exp(-(b[i] + g)))
            ht = res[0:HEAD_DIM, :] * (1.0 / den)
            if mode == "first":
                part_ref[direction, slot, cols, :] = ht
            else:
                tot = part_ref[1 - direction, slot, cols, :] + ht
                hn = tot * lax.rsqrt(jnp.mean(tot * tot, axis=0, keepdims=True) + EPS)
                out_ref[0, toks, cols] = hn.T

    def step(refs, j, n, out_ref, mode):
        slot = None if mode is None else (j if mode == "first" else n - 1 - j)
        pairs = []
        for hp in range(HEADS // 2):
            pairs.append(pair(refs, j, hp, 0, out_ref, mode, slot))
            pairs.append(pair(refs, n - 1 - j, hp, 1, out_ref, mode, slot))
        for phase in range(2):
            for p in pairs:
                next(p, None)

    crefs = (kc_ref, qtc_ref, ktc_ref, vtc_ref, rowc_ref, ecc_ref)
    xrefs = (kx_ref, qtx_ref, ktx_ref, vtx_ref, rowx_ref, ecx_ref)
    for j in range(n_ctx):
        mode = None if not ctx_out else ("first" if j < n_ctx // 2 else "second")
        step(crefs, j, n_ctx, hc_ref, mode)

    def body(j, carry, mode):
        step(xrefs, j, n_x, hx_ref, mode)
        return carry

    lax.fori_loop(0, n_x // 2, functools.partial(body, mode="first"), 0, unroll=2)
    lax.fori_loop(n_x // 2, n_x, functools.partial(body, mode="second"), 0, unroll=2)


def _mlstm(ctx_in, x_in, ctx_out):
    bsz, lc, _ = ctx_in[0].shape
    lx = x_in[0].shape[1]
    n_ctx, n_x = lc // CHUNK, lx // CHUNK
    assert n_ctx % 2 == 0 and n_x % 4 == 0
    i3 = lambda b: (b, 0, 0)
    i4 = lambda b: (b, 0, 0, 0)

    def specs(seq, nch):
        return [
            pl.BlockSpec((1, seq, MLSTM_WIDTH), i3),
            pl.BlockSpec((1, nch, MLSTM_WIDTH, CHUNK), i4),
            pl.BlockSpec((1, nch, MLSTM_WIDTH, CHUNK), i4),
            pl.BlockSpec((1, nch, MLSTM_WIDTH, CHUNK), i4),
            pl.BlockSpec((1, nch, N_ROWQ * UNITS, CHUNK), i4),
            pl.BlockSpec((1, seq, LANES), i3),
        ]

    out_specs = [pl.BlockSpec((1, lx, MLSTM_WIDTH), i3)]
    out_shape = [jax.ShapeDtypeStruct((bsz, lx, MLSTM_WIDTH), F32)]
    if ctx_out:
        out_specs.insert(0, pl.BlockSpec((1, lc, MLSTM_WIDTH), i3))
        out_shape.insert(0, jax.ShapeDtypeStruct((bsz, lc, MLSTM_WIDTH), F32))
    return pl.pallas_call(
        functools.partial(_mlstm_kernel, ctx_out=ctx_out, n_ctx=n_ctx, n_x=n_x),
        grid=(bsz,),
        in_specs=specs(lc, n_ctx) + specs(lx, n_x),
        out_specs=out_specs,
        out_shape=out_shape,
        scratch_shapes=[
            pltpu.VMEM((UNITS // 2, AUG, 2 * HEAD_DIM), F32),
            pltpu.VMEM((UNITS, SUBLANES, LANES), F32),
            pltpu.VMEM((2, max(n_ctx, n_x) // 2, MLSTM_WIDTH, CHUNK), F32),
        ],
        compiler_params=_params("parallel"),
        name="mlstm",
    )(*ctx_in, *x_in)


def _shift_conv(y, w_ref, lo, width, pos):
    n = y.shape[0]
    acc = jnp.zeros(y.shape, F32)
    for j in range(CONV_K):
        d = j - CONV_R
        w = w_ref[j:j + 1, lo:lo + y.shape[1]]
        if d == 0:
            acc = acc + w * y
        else:
            sh = pltpu.roll(y, (-d) % n, axis=0)
            valid = (pos >= -d) if d < 0 else (pos < width - d)
            acc = acc + w * jnp.where(valid, sh, 0.0)
    return acc


def _mix_kernel(x_ref, mod_ref, os_ref, h_ref, z_ref, mg_ref, slg_ref, slb_ref, sw_ref, sb_ref,
                cw_ref, cb_ref, clg_ref, clb_ref, wo_ref, out_ref, *scratch, seq, tm, grid_mode):
    t = pl.program_id(1)
    r0 = pl.multiple_of(t * tm, tm)
    rows = pl.ds(r0, tm)
    half = CONV_WIDTH // 2
    pad = CONV_R * GRID_W

    if grid_mode:
        (ycol_ref,) = scratch

        @pl.when(t == 0)
        def _():
            ycol_ref[0:pad, :] = jnp.zeros((pad, half), F32)
            ycol_ref[pad + seq:pad + seq + pad, :] = jnp.zeros((pad, half), F32)
            ycol_ref[pad:pad + seq, :] = z_ref[0, :, half:CONV_WIDTH] * _sigmoid(
                z_ref[0, :, CONV_WIDTH + half:2 * CONV_WIDTH])

        pos = lax.broadcasted_iota(jnp.int32, (tm, half), 0) & (GRID_W - 1)
        y_row = z_ref[0, rows, 0:half] * _sigmoid(z_ref[0, rows, CONV_WIDTH:CONV_WIDTH + half])
        conv_r = _shift_conv(y_row, cw_ref, 0, GRID_W, pos)
        conv_c = jnp.zeros((tm, half), F32)
        for j in range(CONV_K):
            start = pl.multiple_of(r0 + GRID_W * j, GRID_W)
            conv_c = conv_c + cw_ref[j:j + 1, half:CONV_WIDTH] * ycol_ref[pl.ds(start, tm), :]
        conv = jnp.concatenate([conv_r, conv_c], axis=1)
    else:
        pos = lax.broadcasted_iota(jnp.int32, (tm, CONV_WIDTH), 0)
        y = z_ref[0, :, 0:CONV_WIDTH] * _sigmoid(z_ref[0, :, CONV_WIDTH:2 * CONV_WIDTH])
        conv = _shift_conv(y, cw_ref, 0, tm, pos)
    y_c = _silu(_layernorm(conv + cb_ref[...], clg_ref[...], clb_ref[...]))

    zs = jax.nn.gelu(os_ref[0, :, MLSTM_WIDTH:MLSTM_WIDTH + 2 * SGU_WIDTH])
    u = zs[:, 0:SGU_WIDTH]
    vn = _layernorm(zs[:, SGU_WIDTH:2 * SGU_WIDTH], slg_ref[...], slb_ref[...]).astype(BF16)
    lane_group = lax.broadcasted_iota(jnp.int32, (CHUNK, SGU_WIDTH), 1) // (SGU_WIDTH // SGU_GROUPS)
    mixed = []
    for ch in range(tm // CHUNK):
        vc = vn[ch * CHUNK:(ch + 1) * CHUNK]
        m = sb_ref[...]
        for g in range(SGU_GROUPS):
            m = m + jnp.where(lane_group == g, jnp.dot(sw_ref[g], vc, preferred_element_type=F32), 0.0)
        mixed.append(m)
    y_b = u * jnp.concatenate(mixed, axis=0)

    y_a = h_ref[0] * mg_ref[...] * _sigmoid(os_ref[0, :, 0:MLSTM_WIDTH])

    o_a, o_b = MLSTM_WIDTH, MLSTM_WIDTH + SGU_WIDTH
    proj = jnp.dot(y_a.astype(BF16), wo_ref[0:o_a, :], preferred_element_type=F32)
    proj = proj + jnp.dot(y_b.astype(BF16), wo_ref[o_a:o_b, :], preferred_element_type=F32)
    proj = proj + jnp.dot(y_c.astype(BF16), wo_ref[o_b:, :], preferred_element_type=F32)
    out_ref[0] = x_ref[0] + mod_ref[0, 2:3, :] * proj


def _mix(x, mod, mod_row, rest, h, mg, slg, slb, sw, sbias, cw, cb, clg, clb, wo, tm, grid_mode):
    bsz, seq, _ = x.shape
    const2 = lambda b, t: (0, 0)
    const3 = lambda b, t: (0, 0, 0)
    scratch = []
    if grid_mode:
        scratch.append(pltpu.VMEM((seq + 2 * CONV_R * GRID_W, CONV_WIDTH // 2), F32))
    return pl.pallas_call(
        functools.partial(_mix_kernel, seq=seq, tm=tm, grid_mode=grid_mode),
        grid=(bsz, seq // tm),
        in_specs=[
            pl.BlockSpec((1, tm, D_MODEL), lambda b, t: (b, t, 0)),
            pl.BlockSpec((1, 6, D_MODEL), mod_row),
            pl.BlockSpec((1, tm, MLSTM_WIDTH + 2 * SGU_WIDTH), lambda b, t: (b, t, 0)),
            pl.BlockSpec((1, tm, MLSTM_WIDTH), lambda b, t: (b, t, 0)),
            pl.BlockSpec((1, seq, 2 * CONV_WIDTH), lambda b, t: (b, 0, 2)),
            pl.BlockSpec((1, MLSTM_WIDTH), const2),
            pl.BlockSpec((1, SGU_WIDTH), const2),
            pl.BlockSpec((1, SGU_WIDTH), const2),
            pl.BlockSpec((SGU_GROUPS, CHUNK, CHUNK), const3),
            pl.BlockSpec((CHUNK, SGU_WIDTH), const2),
            pl.BlockSpec((CONV_K, CONV_WIDTH), const2),
            pl.BlockSpec((1, CONV_WIDTH), const2),
            pl.BlockSpec((1, CONV_WIDTH), const2),
            pl.BlockSpec((1, CONV_WIDTH), const2),
            pl.BlockSpec((D_MODEL, D_MODEL), const2),
        ],
        out_specs=pl.BlockSpec((1, tm, D_MODEL), lambda b, t: (b, t, 0)),
        out_shape=jax.ShapeDtypeStruct((bsz, seq, D_MODEL), F32),
        scratch_shapes=scratch,
        compiler_params=_params("parallel", "arbitrary"),
        name="mix",
    )(x, mod, rest, h, rest, mg, slg, slb, sw, sbias, cw, cb, clg, clb, wo)


def _ffn_kernel(x_ref, mod_ref, ng_ref, wgu_ref, wd_ref, fg_ref, out_ref, *, final):
    x = x_ref[0]
    xm = (_rmsnorm(x, ng_ref[...]) * (1.0 + mod_ref[0, 4:5, :]) + mod_ref[0, 3:4, :]).astype(BF16)
    acc = jnp.zeros(x.shape, F32)
    for ch in range(FFN_HIDDEN // FFN_CHUNK):
        lo = ch * FFN_CHUNK
        gate = jnp.dot(xm, wgu_ref[:, lo:lo + FFN_CHUNK], preferred_element_type=F32)
        up = jnp.dot(xm, wgu_ref[:, FFN_HIDDEN + lo:FFN_HIDDEN + lo + FFN_CHUNK], preferred_element_type=F32)
        act = (_silu(gate) * up).astype(BF16)
        acc = acc + jnp.dot(act, wd_ref[lo:lo + FFN_CHUNK, :], preferred_element_type=F32)
    y = x + mod_ref[0, 5:6, :] * acc
    if final:
        y = _rmsnorm(y, fg_ref[...])
    out_ref[0] = y


def _ffn(x, mod, mod_row, ng, wgu, wd, fg, tm, final):
    bsz, seq, _ = x.shape
    const = lambda b, t: (0, 0)
    return pl.pallas_call(
        functools.partial(_ffn_kernel, final=final),
        grid=(bsz, seq // tm),
        in_specs=[
            pl.BlockSpec((1, tm, D_MODEL), lambda b, t: (b, t, 0)),
            pl.BlockSpec((1, 6, D_MODEL), mod_row),
            pl.BlockSpec((1, D_MODEL), const),
            pl.BlockSpec((D_MODEL, 2 * FFN_HIDDEN), const),
            pl.BlockSpec((FFN_HIDDEN, D_MODEL), const),
            pl.BlockSpec((1, D_MODEL), const),
        ],
        out_specs=pl.BlockSpec((1, tm, D_MODEL), lambda b, t: (b, t, 0)),
        out_shape=jax.ShapeDtypeStruct((bsz, seq, D_MODEL), F32),
        compiler_params=_params("parallel", "parallel"),
        name="ffn",
    )(x, mod, ng, wgu, wd, fg)


def _gate_weights(w_in_l, b_in_l):
    wg = w_in_l[:, GATE_OFF:SGU_OFF]
    bg = b_in_l[GATE_OFF:SGU_OFF]
    order = [wg[:, 0:HEADS], wg[:, 2 * HEADS:3 * HEADS], wg[:, HEADS:2 * HEADS], wg[:, 3 * HEADS:4 * HEADS]]
    w = jnp.pad(jnp.concatenate(order, axis=1), ((0, 0), (0, LANES - 2 * UNITS)))
    b_order = [bg[0:HEADS], bg[2 * HEADS:3 * HEADS], bg[HEADS:2 * HEADS], bg[3 * HEADS:4 * HEADS]]
    b = jnp.pad(jnp.concatenate(b_order), (0, LANES - 2 * UNITS))[None, :]
    w_hi = w.astype(BF16)
    w_lo = (w - w_hi.astype(F32)).astype(BF16)
    return jnp.concatenate([w_hi, w_lo], axis=1), b


def kernel(x, c, ctx, c_ctx, w_mod, b_mod, norm1_g, w_in, b_in, mlstm_g, sgu_ln_g, sgu_ln_b, sgu_w, sgu_b,
           conv_w, conv_b, conv_ln_g, conv_ln_b, w_out, norm2_g, w_gu, w_down, final_g):
    bsz = x.shape[0]
    lc = ctx.shape[1]
    assert bsz + 1 <= MOD_ROWS
    c_all = jnp.zeros((MOD_ROWS, D_MODEL), F32).at[:bsz].set(c).at[bsz].set(c_ctx)
    mod_all = _modulation(c_all, w_mod, b_mod).reshape(DEPTH, MOD_ROWS, 6, D_MODEL)
    row_x = lambda b, t: (b, 0, 0)
    row_c = lambda b, t: (bsz, 0, 0)
    tm_x, tm_c = 512, lc

    xc = ctx
    for l in range(DEPTH):
        last = l == DEPTH - 1
        mod = mod_all[l]
        w_main = jnp.concatenate([w_in[l][:, :GATE_OFF], w_in[l][:, SGU_OFF:]], axis=1).astype(BF16)
        b_main = jnp.concatenate([b_in[l][:GATE_OFF], b_in[l][SGU_OFF:]])[None, :]
        w_gate, b_gate = _gate_weights(w_in[l], b_in[l])
        ng1, ng2 = norm1_g[l][None, :], norm2_g[l][None, :]
        mix_w = (mlstm_g[l][None, :], sgu_ln_g[l][None, :], sgu_ln_b[l][None, :], sgu_w[l].astype(BF16),
                 jnp.repeat(sgu_b[l].T, SGU_WIDTH // SGU_GROUPS, axis=1),
                 conv_w[l], conv_b[l][None, :], conv_ln_g[l][None, :], conv_ln_b[l][None, :],
                 w_out[l].astype(BF16))
        wgu, wd = w_gu[l].astype(BF16), w_down[l].astype(BF16)
        fg = final_g[None, :]

        k_x, qt_x, kt_x, vt_x, rest_x, rows_x, ecol_x = _inproj(
            x, mod, row_x, ng1, w_main, b_main, w_gate, b_gate, tm_x)
        k_c, qt_c, kt_c, vt_c, rest_c, rows_c, ecol_c = _inproj(
            xc, mod, row_c, ng1, w_main, b_main, w_gate, b_gate, tm_c)
        hs = _mlstm((k_c, qt_c, kt_c, vt_c, rows_c, ecol_c), (k_x, qt_x, kt_x, vt_x, rows_x, ecol_x),
                    ctx_out=not last)
        x = _mix(x, mod, row_x, rest_x, hs[-1], *mix_w, tm=tm_x, grid_mode=True)
        x = _ffn(x, mod, row_x, ng2, wgu, wd, fg, tm_x, final=last)
        if not last:
            xc = _mix(xc, mod, row_c, rest_c, hs[0], *mix_w, tm=tm_c, grid_mode=False)
            xc = _ffn(xc, mod, row_c, ng2, wgu, wd, fg, tm_c, final=False)
    return x
```

```python
import functools

import jax
import jax.numpy as jnp
from jax import lax
from jax.experimental import pallas as pl
from jax.experimental.pallas import tpu as pltpu

D_MODEL = 1024
DEPTH = 2
GRID_W = 64
EPS = 1e-6

HEADS = 4
HEAD_DIM = 128
MLSTM_WIDTH = HEADS * HEAD_DIM
CHUNK = 128
UNITS = 2 * HEADS
SGU_GROUPS = 4
SGU_WIDTH = 256
CONV_WIDTH = 256
CONV_K = 31
CONV_R = CONV_K // 2
FFN_HIDDEN = 2816
FFN_CHUNK = 256

GATE_OFF = 4 * MLSTM_WIDTH
N_GATES = 4 * HEADS
SGU_OFF = GATE_OFF + N_GATES
QKV_COLS = 3 * MLSTM_WIDTH
QK_COLS = 2 * MLSTM_WIDTH
REST_COLS = MLSTM_WIDTH + 2 * SGU_WIDTH + 2 * CONV_WIDTH
MAIN_COLS = QKV_COLS + REST_COLS
LANES = 128
SUBLANES = 8
BF16_ROWS = 16
MOD_ROWS = 24
N_ROWQ = 5
AUG = HEAD_DIM + BF16_ROWS

VMEM_LIMIT = 56 * 1024 * 1024

F32 = jnp.float32
BF16 = jnp.bfloat16
HIGHEST = lax.Precision.HIGHEST
NT = (((1,), (1,)), ((), ()))

GELU_C = 0.7978845608028654
GELU_A = 0.044715


def _sigmoid(t):
    return 0.5 * jnp.tanh(0.5 * t) + 0.5


def _silu(t):
    return t * _sigmoid(t)


def _gelu_tanh(t):
    inner = t * (GELU_C * GELU_A * (t * t) + GELU_C)
    return t * (0.5 * jnp.tanh(inner) + 0.5)


def _log_sigmoid(t):
    return jnp.minimum(t, 0.0) - jnp.log1p(jnp.exp(-jnp.abs(t)))


def _layernorm(t, g, b):
    mu = jnp.mean(t, axis=-1, keepdims=True)
    d = t - mu
    var = jnp.mean(d * d, axis=-1, keepdims=True)
    return d * lax.rsqrt(var + EPS) * g + b


def _rmsnorm(t, g):
    return t * lax.rsqrt(jnp.mean(t * t, axis=-1, keepdims=True) + EPS) * g


def _params(*sem):
    return pltpu.CompilerParams(dimension_semantics=sem, vmem_limit_bytes=VMEM_LIMIT)


def _mod_kernel(c_ref, w_ref, b_ref, o_ref):
    a = _silu(c_ref[...])
    o_ref[0] = jnp.dot(a, w_ref[0], precision=HIGHEST, preferred_element_type=F32) + b_ref[0]


def _modulation(c_all, w_mod, b_mod):
    tn = 1536
    n = w_mod.shape[-1]
    return pl.pallas_call(
        _mod_kernel,
        grid=(DEPTH, n // tn),
        in_specs=[
            pl.BlockSpec((MOD_ROWS, D_MODEL), lambda l, j: (0, 0)),
            pl.BlockSpec((1, D_MODEL, tn), lambda l, j: (l, 0, j)),
            pl.BlockSpec((1, 1, tn), lambda l, j: (l, 0, j)),
        ],
        out_specs=pl.BlockSpec((1, MOD_ROWS, tn), lambda l, j: (l, 0, j)),
        out_shape=jax.ShapeDtypeStruct((DEPTH, MOD_ROWS, n), F32),
        compiler_params=_params("parallel", "parallel"),
        name="modulation",
    )(c_all, w_mod, b_mod.reshape(DEPTH, 1, n))


def _lane_scan(x, op, lane, forward):
    n = x.shape[1]
    k = 1
    while k < n:
        if forward:
            x = jnp.where(lane >= k, op(x, pltpu.roll(x, k, axis=1)), x)
        else:
            x = jnp.where(lane < n - k, op(x, pltpu.roll(x, n - k, axis=1)), x)
        k *= 2
    return x


def _inproj_kernel(x_ref, mod_ref, ng_ref, w_ref, b_ref, wg_ref, bg_ref,
                   k_ref, qt_ref, kt_ref, vt_ref, rest_ref, rows_ref, ecol_ref, *, tm):
    x = x_ref[0]
    xm = _rmsnorm(x, ng_ref[...]) * (1.0 + mod_ref[0, 1:2, :]) + mod_ref[0, 0:1, :]
    xh = xm.astype(BF16)
    xl = (xm - xh.astype(F32)).astype(BF16)

    g_hi = jnp.dot(xh, wg_ref[...], preferred_element_type=F32)
    g_lo = jnp.dot(xl, wg_ref[...], preferred_element_type=F32)
    g = g_hi[:, 0:LANES] + g_hi[:, LANES:2 * LANES] + g_lo[:, 0:LANES] + bg_ref[...]
    lane = lax.broadcasted_iota(jnp.int32, (UNITS, CHUNK), 1)
    fwd_row = lax.broadcasted_iota(jnp.int32, (UNITS, CHUNK), 0) < HEADS
    pad_rows = jnp.zeros((CHUNK - UNITS, CHUNK), F32)
    for ch in range(tm // CHUNK):
        sl = slice(ch * CHUNK, (ch + 1) * CHUNK)
        gt = g[sl].T
        it = gt[0:UNITS]
        lsf = _log_sigmoid(gt[UNITS:2 * UNITS])
        bt = jnp.where(fwd_row, _lane_scan(lsf, jnp.add, lane, True), _lane_scan(lsf, jnp.add, lane, False))
        et = it - bt
        cme = jnp.where(fwd_row, _lane_scan(et, jnp.maximum, lane, True),
                        _lane_scan(et, jnp.maximum, lane, False))
        b_last = jnp.where(fwd_row, jnp.broadcast_to(bt[:, CHUNK - 1:CHUNK], bt.shape),
                           jnp.broadcast_to(bt[:, 0:1], bt.shape))
        cme_last = jnp.broadcast_to(jnp.max(et, axis=1, keepdims=True), et.shape)
        rows_ref[0, ch] = jnp.concatenate([et, cme, bt, b_last, cme_last], axis=0)
        ecol_ref[0, sl, :] = jnp.concatenate([et, pad_rows], axis=0).T

    p = jnp.dot(xh, w_ref[...], preferred_element_type=F32) + b_ref[...]
    pq = p[:, 0:MLSTM_WIDTH]
    pk = p[:, MLSTM_WIDTH:QK_COLS] * (HEAD_DIM ** -0.5)
    pv = p[:, QK_COLS:QKV_COLS]
    k_ref[0] = pk.astype(BF16)
    rest_ref[0] = p[:, QKV_COLS:MAIN_COLS]
    for ch in range(tm // CHUNK):
        sl = slice(ch * CHUNK, (ch + 1) * CHUNK)
        for head in range(HEADS):
            hs = slice(head * HEAD_DIM, (head + 1) * HEAD_DIM)
            qt_ref[0, ch, hs, :] = pq[sl, hs].T.astype(BF16)
            kt_ref[0, ch, hs, :] = pk[sl, hs].T.astype(BF16)
            vt_ref[0, ch, hs, :] = pv[sl, hs].T.astype(BF16)


def _inproj(x, mod, mod_row, ng, w_main, b_main, w_gate, b_gate, tm):
    bsz, seq, _ = x.shape
    nch = seq // CHUNK
    tch = tm // CHUNK
    const = lambda b, t: (0, 0)
    return pl.pallas_call(
        functools.partial(_inproj_kernel, tm=tm),
        grid=(bsz, seq // tm),
        in_specs=[
            pl.BlockSpec((1, tm, D_MODEL), lambda b, t: (b, t, 0)),
            pl.BlockSpec((1, 6, D_MODEL), mod_row),
            pl.BlockSpec((1, D_MODEL), const),
            pl.BlockSpec((D_MODEL, MAIN_COLS), const),
            pl.BlockSpec((1, MAIN_COLS), const),
            pl.BlockSpec((D_MODEL, 2 * LANES), const),
            pl.BlockSpec((1, LANES), const),
        ],
        out_specs=[
            pl.BlockSpec((1, tm, MLSTM_WIDTH), lambda b, t: (b, t, 0)),
            pl.BlockSpec((1, tch, MLSTM_WIDTH, CHUNK), lambda b, t: (b, t, 0, 0)),
            pl.BlockSpec((1, tch, MLSTM_WIDTH, CHUNK), lambda b, t: (b, t, 0, 0)),
            pl.BlockSpec((1, tch, MLSTM_WIDTH, CHUNK), lambda b, t: (b, t, 0, 0)),
            pl.BlockSpec((1, tm, REST_COLS), lambda b, t: (b, t, 0)),
            pl.BlockSpec((1, tch, N_ROWQ * UNITS, CHUNK), lambda b, t: (b, t, 0, 0)),
            pl.BlockSpec((1, tm, LANES), lambda b, t: (b, t, 0)),
        ],
        out_shape=[
            jax.ShapeDtypeStruct((bsz, seq, MLSTM_WIDTH), BF16),
            jax.ShapeDtypeStruct((bsz, nch, MLSTM_WIDTH, CHUNK), BF16),
            jax.ShapeDtypeStruct((bsz, nch, MLSTM_WIDTH, CHUNK), BF16),
            jax.ShapeDtypeStruct((bsz, nch, MLSTM_WIDTH, CHUNK), BF16),
            jax.ShapeDtypeStruct((bsz, seq, REST_COLS), F32),
            jax.ShapeDtypeStruct((bsz, nch, N_ROWQ * UNITS, CHUNK), F32),
            jax.ShapeDtypeStruct((bsz, seq, LANES), F32),
        ],
        compiler_params=_params("parallel", "parallel"),
        name="inproj",
    )(x, mod, ng, w_main, b_main, w_gate, b_gate)


def _mlstm_kernel(kc_ref, qtc_ref, ktc_ref, vtc_ref, rowc_ref, ecc_ref,
                  kx_ref, qtx_ref, ktx_ref, vtx_ref, rowx_ref, ecx_ref, *rest, ctx_out, n_ctx, n_x):
    if ctx_out:
        hc_ref, hx_ref, ct_ref, m_ref, part_ref = rest
    else:
        hx_ref, ct_ref, m_ref, part_ref = rest
        hc_ref = None
    ct_ref[...] = jnp.zeros(ct_ref.shape, F32)
    m_ref[...] = jnp.zeros(m_ref.shape, F32)
    r = lax.broadcasted_iota(jnp.int32, (CHUNK, CHUNK), 0)
    c = lax.broadcasted_iota(jnp.int32, (CHUNK, CHUNK), 1)
    masks = (r <= c, r >= c)
    ones = jnp.ones((BF16_ROWS, CHUNK), BF16)
    zeros = jnp.zeros((CHUNK, CHUNK), BF16)

    def block_diag(a, b):
        return jnp.concatenate([jnp.concatenate([a, zeros], axis=1),
                                jnp.concatenate([zeros, b], axis=1)], axis=0)

    def bf16_rows(row):
        tile = jnp.broadcast_to(row, (BF16_ROWS, CHUNK)).astype(BF16)
        return jnp.concatenate([tile] * (CHUNK // BF16_ROWS), axis=0)

    def pair(refs, j, hp, direction, out_ref, mode, slot):
        k_ref, qt_ref, kt_ref, vt_ref, row_ref, ec_ref = refs
        heads = (2 * hp, 2 * hp + 1)
        units = tuple(direction * HEADS + h for h in heads)
        sp = direction * (HEADS // 2) + hp
        r0 = j * CHUNK if isinstance(j, int) else pl.multiple_of(j * CHUNK, CHUNK)
        toks = pl.ds(r0, CHUNK)
        hsl = [slice(h * HEAD_DIM, (h + 1) * HEAD_DIM) for h in heads]

        def row(q, u):
            return row_ref[0, j, q * UNITS + u:q * UNITS + u + 1, :]

        e, cme, b, b_last, cme_last = ([row(q, u) for u in units] for q in range(N_ROWQ))
        m_old = [m_ref[u, 0:1, :] for u in units]
        ct_old = ct_ref[sp]
        vt_aug = [jnp.concatenate([vt_ref[0, j, s, :], ones], axis=0) for s in hsl]

        if mode is not None:
            qt = [qt_ref[0, j, s, :] for s in hsl]
            k2 = k_ref[0, toks, 2 * hp * HEAD_DIM:(2 * hp + 2) * HEAD_DIM]
            qkt = jnp.dot(k2, block_diag(qt[0], qt[1]), preferred_element_type=F32)

        g_last = [jnp.maximum(m_old[i], cme_last[i]) for i in range(2)]
        kst = [kt_ref[0, j, hsl[i], :] * bf16_rows(jnp.exp(e[i] - g_last[i])) for i in range(2)]
        upd = lax.dot_general(jnp.concatenate(vt_aug, axis=1), block_diag(kst[0], kst[1]), NT,
                              preferred_element_type=F32)
        keep = jnp.concatenate([jnp.exp(m_old[i] - g_last[i]) for i in range(2)], axis=1)
        ct_ref[sp] = keep * ct_old + upd
        for i in range(2):
            m_ref[units[i]] = jnp.broadcast_to(b_last[i] + g_last[i], m_ref.shape[1:])

        yield
        if mode is None:
            return
        ct_b = ct_old.astype(BF16)
        for i in range(2):
            u, cols = units[i], hsl[i]
            g = jnp.maximum(m_old[i], cme[i])
            dl = jnp.where(masks[direction], ec_ref[0, toks, u:u + 1] - g, -jnp.inf)
            st = (qkt[:, i * CHUNK:(i + 1) * CHUNK] * jnp.exp(dl)).astype(BF16)
            qs = qt[i] * bf16_rows(jnp.exp(m_old[i] - g))
            res = jnp.dot(jnp.concatenate([vt_aug[i], ct_b[:, i * HEAD_DIM:(i + 1) * HEAD_DIM]], axis=1),
                          jnp.concatenate([st, qs], axis=0), preferred_element_type=F32)
            den = jnp.maximum(jnp.abs(res[HEAD_DIM:HEAD_DIM + 1, :]), jnp.exp(-(b[i] + g)))
            ht = res[0:HEAD_DIM, :] * (1.0 / den)
            if mode == "first":
                part_ref[direction, slot, cols, :] = ht
            else:
                tot = part_ref[1 - direction, slot, cols, :] + ht
                hn = tot * lax.rsqrt(jnp.mean(tot * tot, axis=0, keepdims=True) + EPS)
                out_ref[0, toks, cols] = hn.T

    def step(refs, j, n, out_ref, mode):
        slot = None if mode is None else (j if mode == "first" else n - 1 - j)
        pairs = []
        for hp in range(HEADS // 2):
            pairs.append(pair(refs, j, hp, 0, out_ref, mode, slot))
            pairs.append(pair(refs, n - 1 - j, hp, 1, out_ref, mode, slot))
        for phase in range(2):
            for p in pairs:
                next(p, None)

    crefs = (kc_ref, qtc_ref, ktc_ref, vtc_ref, rowc_ref, ecc_ref)
    xrefs = (kx_ref, qtx_ref, ktx_ref, vtx_ref, rowx_ref, ecx_ref)
    for j in range(n_ctx):
        mode = None if not ctx_out else ("first" if j < n_ctx // 2 else "second")
        step(crefs, j, n_ctx, hc_ref, mode)

    def body(j, carry, mode):
        step(xrefs, j, n_x, hx_ref, mode)
        return carry

    lax.fori_loop(0, n_x // 2, functools.partial(body, mode="first"), 0, unroll=2)
    lax.fori_loop(n_x // 2, n_x, functools.partial(body, mode="second"), 0, unroll=2)


def _mlstm(ctx_in, x_in, ctx_out):
    bsz, lc, _ = ctx_in[0].shape
    lx = x_in[0].shape[1]
    n_ctx, n_x = lc // CHUNK, lx // CHUNK
    assert n_ctx % 2 == 0 and n_x % 4 == 0
    i3 = lambda b: (b, 0, 0)
    i4 = lambda b: (b, 0, 0, 0)

    def specs(seq, nch):
        return [
            pl.BlockSpec((1, seq, MLSTM_WIDTH), i3),
            pl.BlockSpec((1, nch, MLSTM_WIDTH, CHUNK), i4),
            pl.BlockSpec((1, nch, MLSTM_WIDTH, CHUNK), i4),
            pl.BlockSpec((1, nch, MLSTM_WIDTH, CHUNK), i4),
            pl.BlockSpec((1, nch, N_ROWQ * UNITS, CHUNK), i4),
            pl.BlockSpec((1, seq, LANES), i3),
        ]

    out_specs = [pl.BlockSpec((1, lx, MLSTM_WIDTH), i3)]
    out_shape = [jax.ShapeDtypeStruct((bsz, lx, MLSTM_WIDTH), F32)]
    if ctx_out:
        out_specs.insert(0, pl.BlockSpec((1, lc, MLSTM_WIDTH), i3))
        out_shape.insert(0, jax.ShapeDtypeStruct((bsz, lc, MLSTM_WIDTH), F32))
    return pl.pallas_call(
        functools.partial(_mlstm_kernel, ctx_out=ctx_out, n_ctx=n_ctx, n_x=n_x),
        grid=(bsz,),
        in_specs=specs(lc, n_ctx) + specs(lx, n_x),
        out_specs=out_specs,
        out_shape=out_shape,
        scratch_shapes=[
            pltpu.VMEM((UNITS // 2, AUG, 2 * HEAD_DIM), F32),
            pltpu.VMEM((UNITS, SUBLANES, LANES), F32),
            pltpu.VMEM((2, max(n_ctx, n_x) // 2, MLSTM_WIDTH, CHUNK), F32),
        ],
        compiler_params=_params("parallel"),
        name="mlstm",
    )(*ctx_in, *x_in)


def _shift_conv(y, w_ref, lo, width, pos):
    n = y.shape[0]
    acc = jnp.zeros(y.shape, F32)
    for j in range(CONV_K):
        d = j - CONV_R
        w = w_ref[j:j + 1, lo:lo + y.shape[1]]
        if d == 0:
            acc = acc + w * y
        else:
            sh = pltpu.roll(y, (-d) % n, axis=0)
            valid = (pos >= -d) if d < 0 else (pos < width - d)
            acc = acc + w * jnp.where(valid, sh, 0.0)
    return acc


def _mix_kernel(x_ref, mod_ref, os_ref, h_ref, z_ref, mg_ref, slg_ref, slb_ref, sw_ref, sb_ref,
                cw_ref, cb_ref, clg_ref, clb_ref, wo_ref, out_ref, *scratch, seq, tm, grid_mode):
    t = pl.program_id(1)
    r0 = pl.multiple_of(t * tm, tm)
    rows = pl.ds(r0, tm)
    half = CONV_WIDTH // 2
    pad = CONV_R * GRID_W

    if grid_mode:
        (ycol_ref,) = scratch

        @pl.when(t == 0)
        def _():
            ycol_ref[0:pad, :] = jnp.zeros((pad, half), F32)
            ycol_ref[pad + seq:pad + seq + pad, :] = jnp.zeros((pad, half), F32)
            ycol_ref[pad:pad + seq, :] = z_ref[0, :, half:CONV_WIDTH] * _sigmoid(
                z_ref[0, :, CONV_WIDTH + half:2 * CONV_WIDTH])

        pos = lax.broadcasted_iota(jnp.int32, (tm, half), 0) & (GRID_W - 1)
        y_row = z_ref[0, rows, 0:half] * _sigmoid(z_ref[0, rows, CONV_WIDTH:CONV_WIDTH + half])
        conv_r = _shift_conv(y_row, cw_ref, 0, GRID_W, pos)
        conv_c = jnp.zeros((tm, half), F32)
        for j in range(CONV_K):
            start = pl.multiple_of(r0 + GRID_W * j, GRID_W)
            conv_c = conv_c + cw_ref[j:j + 1, half:CONV_WIDTH] * ycol_ref[pl.ds(start, tm), :]
        conv = jnp.concatenate([conv_r, conv_c], axis=1)
    else:
        pos = lax.broadcasted_iota(jnp.int32, (tm, CONV_WIDTH), 0)
        y = z_ref[0, :, 0:CONV_WIDTH] * _sigmoid(z_ref[0, :, CONV_WIDTH:2 * CONV_WIDTH])
        conv = _shift_conv(y, cw_ref, 0, tm, pos)
    y_c = _silu(_layernorm(conv + cb_ref[...], clg_ref[...], clb_ref[...]))

    zs = _gelu_tanh(os_ref[0, :, MLSTM_WIDTH:MLSTM_WIDTH + 2 * SGU_WIDTH])
    u = zs[:, 0:SGU_WIDTH]
    vn = _layernorm(zs[:, SGU_WIDTH:2 * SGU_WIDTH], slg_ref[...], slb_ref[...]).astype(BF16)
    lane_group = lax.broadcasted_iota(jnp.int32, (CHUNK, SGU_WIDTH), 1) // (SGU_WIDTH // SGU_GROUPS)
    mixed = []
    for ch in range(tm // CHUNK):
        vc = vn[ch * CHUNK:(ch + 1) * CHUNK]
        m = sb_ref[...]
        for g in range(SGU_GROUPS):
            m = m + jnp.where(lane_group == g, jnp.dot(sw_ref[g], vc, preferred_element_type=F32), 0.0)
        mixed.append(m)
    y_b = u * jnp.concatenate(mixed, axis=0)

    y_a = h_ref[0] * mg_ref[...] * _sigmoid(os_ref[0, :, 0:MLSTM_WIDTH])

    o_a, o_b = MLSTM_WIDTH, MLSTM_WIDTH + SGU_WIDTH
    proj = jnp.dot(y_a.astype(BF16), wo_ref[0:o_a, :], preferred_element_type=F32)
    proj = proj + jnp.dot(y_b.astype(BF16), wo_ref[o_a:o_b, :], preferred_element_type=F32)
    proj = proj + jnp.dot(y_c.astype(BF16), wo_ref[o_b:, :], preferred_element_type=F32)
    out_ref[0] = x_ref[0] + mod_ref[0, 2:3, :] * proj


def _mix(x, mod, mod_row, rest, h, mg, slg, slb, sw, sbias, cw, cb, clg, clb, wo, tm, grid_mode):
    bsz, seq, _ = x.shape
    const2 = lambda b, t: (0, 0)
    const3 = lambda b, t: (0, 0, 0)
    scratch = []
    if grid_mode:
        scratch.append(pltpu.VMEM((seq + 2 * CONV_R * GRID_W, CONV_WIDTH // 2), F32))
    return pl.pallas_call(
        functools.partial(_mix_kernel, seq=seq, tm=tm, grid_mode=grid_mode),
        grid=(bsz, seq // tm),
        in_specs=[
            pl.BlockSpec((1, tm, D_MODEL), lambda b, t: (b, t, 0)),
            pl.BlockSpec((1, 6, D_MODEL), mod_row),
            pl.BlockSpec((1, tm, MLSTM_WIDTH + 2 * SGU_WIDTH), lambda b, t: (b, t, 0)),
            pl.BlockSpec((1, tm, MLSTM_WIDTH), lambda b, t: (b, t, 0)),
            pl.BlockSpec((1, seq, 2 * CONV_WIDTH), lambda b, t: (b, 0, 2)),
            pl.BlockSpec((1, MLSTM_WIDTH), const2),
            pl.BlockSpec((1, SGU_WIDTH), const2),
            pl.BlockSpec((1, SGU_WIDTH), const2),
            pl.BlockSpec((SGU_GROUPS, CHUNK, CHUNK), const3),
            pl.BlockSpec((CHUNK, SGU_WIDTH), const2),
            pl.BlockSpec((CONV_K, CONV_WIDTH), const2),
            pl.BlockSpec((1, CONV_WIDTH), const2),
            pl.BlockSpec((1, CONV_WIDTH), const2),
            pl.BlockSpec((1, CONV_WIDTH), const2),
            pl.BlockSpec((D_MODEL, D_MODEL), const2),
        ],
        out_specs=pl.BlockSpec((1, tm, D_MODEL), lambda b, t: (b, t, 0)),
        out_shape=jax.ShapeDtypeStruct((bsz, seq, D_MODEL), F32),
        scratch_shapes=scratch,
        compiler_params=_params("parallel", "arbitrary"),
        name="mix",
    )(x, mod, rest, h, rest, mg, slg, slb, sw, sbias, cw, cb, clg, clb, wo)


def _ffn_kernel(x_ref, mod_ref, ng_ref, wgu_ref, wd_ref, fg_ref, out_ref, *, final):
    x = x_ref[0]
    xm = (_rmsnorm(x, ng_ref[...]) * (1.0 + mod_ref[0, 4:5, :]) + mod_ref[0, 3:4, :]).astype(BF16)
    acc = jnp.zeros(x.shape, F32)
    for ch in range(FFN_HIDDEN // FFN_CHUNK):
        lo = ch * FFN_CHUNK
        gate = jnp.dot(xm, wgu_ref[:, lo:lo + FFN_CHUNK], preferred_element_type=F32)
        up = jnp.dot(xm, wgu_ref[:, FFN_HIDDEN + lo:FFN_HIDDEN + lo + FFN_CHUNK], preferred_element_type=F32)
        act = (_silu(gate) * up).astype(BF16)
        acc = acc + jnp.dot(act, wd_ref[lo:lo + FFN_CHUNK, :], preferred_element_type=F32)
    y = x + mod_ref[0, 5:6, :] * acc
    if final:
        y = _rmsnorm(y, fg_ref[...])
    out_ref[0] = y


def _ffn(x, mod, mod_row, ng, wgu, wd, fg, tm, final):
    bsz, seq, _ = x.shape
    const = lambda b, t: (0, 0)
    return pl.pallas_call(
        functools.partial(_ffn_kernel, final=final),
        grid=(bsz, seq // tm),
        in_specs=[
            pl.BlockSpec((1, tm, D_MODEL), lambda b, t: (b, t, 0)),
            pl.BlockSpec((1, 6, D_MODEL), mod_row),
            pl.BlockSpec((1, D_MODEL), const),
            pl.BlockSpec((D_MODEL, 2 * FFN_HIDDEN), const),
            pl.BlockSpec((FFN_HIDDEN, D_MODEL), const),
            pl.BlockSpec((1, D_MODEL), const),
        ],
        out_specs=pl.BlockSpec((1, tm, D_MODEL), lambda b, t: (b, t, 0)),
        out_shape=jax.ShapeDtypeStruct((bsz, seq, D_MODEL), F32),
        compiler_params=_params("parallel", "parallel"),
        name="ffn",
    )(x, mod, ng, wgu, wd, fg)


def _gate_weights(w_in_l, b_in_l):
    wg = w_in_l[:, GATE_OFF:SGU_OFF]
    bg = b_in_l[GATE_OFF:SGU_OFF]
    order = [wg[:, 0:HEADS], wg[:, 2 * HEADS:3 * HEADS], wg[:, HEADS:2 * HEADS], wg[:, 3 * HEADS:4 * HEADS]]
    w = jnp.pad(jnp.concatenate(order, axis=1), ((0, 0), (0, LANES - 2 * UNITS)))
    b_order = [bg[0:HEADS], bg[2 * HEADS:3 * HEADS], bg[HEADS:2 * HEADS], bg[3 * HEADS:4 * HEADS]]
    b = jnp.pad(jnp.concatenate(b_order), (0, LANES - 2 * UNITS))[None, :]
    w_hi = w.astype(BF16)
    w_lo = (w - w_hi.astype(F32)).astype(BF16)
    return jnp.concatenate([w_hi, w_lo], axis=1), b


def kernel(x, c, ctx, c_ctx, w_mod, b_mod, norm1_g, w_in, b_in, mlstm_g, sgu_ln_g, sgu_ln_b, sgu_w, sgu_b,
           conv_w, conv_b, conv_ln_g, conv_ln_b, w_out, norm2_g, w_gu, w_down, final_g):
    bsz = x.shape[0]
    lc = ctx.shape[1]
    assert bsz + 1 <= MOD_ROWS
    c_all = jnp.zeros((MOD_ROWS, D_MODEL), F32).at[:bsz].set(c).at[bsz].set(c_ctx)
    mod_all = _modulation(c_all, w_mod, b_mod).reshape(DEPTH, MOD_ROWS, 6, D_MODEL)
    row_x = lambda b, t: (b, 0, 0)
    row_c = lambda b, t: (bsz, 0, 0)
    tm_x, tm_c = 512, lc

    xc = ctx
    for l in range(DEPTH):
        last = l == DEPTH - 1
        mod = mod_all[l]
        w_main = jnp.concatenate([w_in[l][:, :GATE_OFF], w_in[l][:, SGU_OFF:]], axis=1).astype(BF16)
        b_main = jnp.concatenate([b_in[l][:GATE_OFF], b_in[l][SGU_OFF:]])[None, :]
        w_gate, b_gate = _gate_weights(w_in[l], b_in[l])
        ng1, ng2 = norm1_g[l][None, :], norm2_g[l][None, :]
        mix_w = (mlstm_g[l][None, :], sgu_ln_g[l][None, :], sgu_ln_b[l][None, :], sgu_w[l].astype(BF16),
                 jnp.repeat(sgu_b[l].T, SGU_WIDTH // SGU_GROUPS, axis=1),
                 conv_w[l], conv_b[l][None, :], conv_ln_g[l][None, :], conv_ln_b[l][None, :],
                 w_out[l].astype(BF16))
        wgu, wd = w_gu[l].astype(BF16), w_down[l].astype(BF16)
        fg = final_g[None, :]

        k_x, qt_x, kt_x, vt_x, rest_x, rows_x, ecol_x = _inproj(
            x, mod, row_x, ng1, w_main, b_main, w_gate, b_gate, tm_x)
        k_c, qt_c, kt_c, vt_c, rest_c, rows_c, ecol_c = _inproj(
            xc, mod, row_c, ng1, w_main, b_main, w_gate, b_gate, tm_c)
        hs = _mlstm((k_c, qt_c, kt_c, vt_c, rows_c, ecol_c), (k_x, qt_x, kt_x, vt_x, rows_x, ecol_x),
                    ctx_out=not last)
        x = _mix(x, mod, row_x, rest_x, hs[-1], *mix_w, tm=tm_x, grid_mode=True)
        x = _ffn(x, mod, row_x, ng2, wgu, wd, fg, tm_x, final=last)
        if not last:
            xc = _mix(xc, mod, row_c, rest_c, hs[0], *mix_w, tm=tm_c, grid_mode=False)
            xc = _ffn(xc, mod, row_c, ng2, wgu, wd, fg, tm_c, final=False)
    return x
```

```python
import functools

import jax
import jax.numpy as jnp
from jax import lax
from jax.experimental import pallas as pl
from jax.experimental.pallas import tpu as pltpu

D_MODEL = 1024
DEPTH = 2
GRID_W = 64
EPS = 1e-6

HEADS = 4
HEAD_DIM = 128
MLSTM_WIDTH = HEADS * HEAD_DIM
CHUNK = 128
UNITS = 2 * HEADS
SGU_GROUPS = 4
SGU_WIDTH = 256
CONV_WIDTH = 256
CONV_K = 31
CONV_R = CONV_K // 2
FFN_HIDDEN = 2816
FFN_CHUNK = 256

GATE_OFF = 4 * MLSTM_WIDTH
N_GATES = 4 * HEADS
SGU_OFF = GATE_OFF + N_GATES
QKV_COLS = 3 * MLSTM_WIDTH
QK_COLS = 2 * MLSTM_WIDTH
REST_COLS = MLSTM_WIDTH + 2 * SGU_WIDTH + 2 * CONV_WIDTH
MAIN_COLS = QKV_COLS + REST_COLS
LANES = 128
SUBLANES = 8
BF16_ROWS = 16
MOD_ROWS = 24
N_ROWQ = 5
AUG = HEAD_DIM + BF16_ROWS

VMEM_LIMIT = 56 * 1024 * 1024

F32 = jnp.float32
BF16 = jnp.bfloat16
HIGHEST = lax.Precision.HIGHEST
NT = (((1,), (1,)), ((), ()))

GELU_C = 0.7978845608028654
GELU_A = 0.044715


def _sigmoid(t):
    return 0.5 * jnp.tanh(0.5 * t) + 0.5


def _silu(t):
    return t * _sigmoid(t)


def _gelu_tanh(t):
    inner = t * (GELU_C * GELU_A * (t * t) + GELU_C)
    return t * (0.5 * jnp.tanh(inner) + 0.5)


def _log_sigmoid(t):
    return jnp.minimum(t, 0.0) - jnp.log1p(jnp.exp(-jnp.abs(t)))


def _layernorm(t, g, b):
    mu = jnp.mean(t, axis=-1, keepdims=True)
    d = t - mu
    var = jnp.mean(d * d, axis=-1, keepdims=True)
    return d * lax.rsqrt(var + EPS) * g + b


def _rmsnorm(t, g):
    return t * lax.rsqrt(jnp.mean(t * t, axis=-1, keepdims=True) + EPS) * g


def _modulated_rmsnorm(t, g, scale, shift):
    return t * lax.rsqrt(jnp.mean(t * t, axis=-1, keepdims=True) + EPS) * (g * (1.0 + scale)) + shift


def _params(*sem):
    return pltpu.CompilerParams(dimension_semantics=sem, vmem_limit_bytes=VMEM_LIMIT)


def _mod_kernel(c_ref, w_ref, b_ref, o_ref):
    a = _silu(c_ref[...])
    o_ref[0] = jnp.dot(a, w_ref[0], precision=HIGHEST, preferred_element_type=F32) + b_ref[0]


def _modulation(c_all, w_mod, b_mod):
    tn = 1536
    n = w_mod.shape[-1]
    return pl.pallas_call(
        _mod_kernel,
        grid=(DEPTH, n // tn),
        in_specs=[
            pl.BlockSpec((MOD_ROWS, D_MODEL), lambda l, j: (0, 0)),
            pl.BlockSpec((1, D_MODEL, tn), lambda l, j: (l, 0, j)),
            pl.BlockSpec((1, 1, tn), lambda l, j: (l, 0, j)),
        ],
        out_specs=pl.BlockSpec((1, MOD_ROWS, tn), lambda l, j: (l, 0, j)),
        out_shape=jax.ShapeDtypeStruct((DEPTH, MOD_ROWS, n), F32),
        compiler_params=_params("parallel", "parallel"),
        name="modulation",
    )(c_all, w_mod, b_mod.reshape(DEPTH, 1, n))


def _lane_scan(x, op, lane, forward):
    n = x.shape[1]
    k = 1
    while k < n:
        if forward:
            x = jnp.where(lane >= k, op(x, pltpu.roll(x, k, axis=1)), x)
        else:
            x = jnp.where(lane < n - k, op(x, pltpu.roll(x, n - k, axis=1)), x)
        k *= 2
    return x


def _inproj_kernel(x_ref, mod_ref, ng_ref, w_ref, b_ref, wg_ref, bg_ref,
                   k_ref, qt_ref, kt_ref, vt_ref, rest_ref, rows_ref, ecol_ref, *, tm):
    x = x_ref[0]
    xm = _modulated_rmsnorm(x, ng_ref[...], mod_ref[0, 1:2, :], mod_ref[0, 0:1, :])
    xh = xm.astype(BF16)
    xl = (xm - xh.astype(F32)).astype(BF16)

    g_hi = jnp.dot(xh, wg_ref[...], preferred_element_type=F32)
    g_lo = jnp.dot(xl, wg_ref[...], preferred_element_type=F32)
    g = g_hi[:, 0:LANES] + g_hi[:, LANES:2 * LANES] + g_lo[:, 0:LANES] + bg_ref[...]
    lane = lax.broadcasted_iota(jnp.int32, (UNITS, CHUNK), 1)
    fwd_row = lax.broadcasted_iota(jnp.int32, (UNITS, CHUNK), 0) < HEADS
    pad_rows = jnp.zeros((CHUNK - UNITS, CHUNK), F32)
    for ch in range(tm // CHUNK):
        sl = slice(ch * CHUNK, (ch + 1) * CHUNK)
        gt = g[sl].T
        it = gt[0:UNITS]
        lsf = _log_sigmoid(gt[UNITS:2 * UNITS])
        bt = jnp.where(fwd_row, _lane_scan(lsf, jnp.add, lane, True), _lane_scan(lsf, jnp.add, lane, False))
        et = it - bt
        cme = jnp.where(fwd_row, _lane_scan(et, jnp.maximum, lane, True),
                        _lane_scan(et, jnp.maximum, lane, False))
        b_last = jnp.where(fwd_row, jnp.broadcast_to(bt[:, CHUNK - 1:CHUNK], bt.shape),
                           jnp.broadcast_to(bt[:, 0:1], bt.shape))
        cme_last = jnp.broadcast_to(jnp.max(et, axis=1, keepdims=True), et.shape)
        rows_ref[0, ch] = jnp.concatenate([et, cme, bt, b_last, cme_last], axis=0)
        ecol_ref[0, sl, :] = jnp.concatenate([et, pad_rows], axis=0).T

    p = jnp.dot(xh, w_ref[...], preferred_element_type=F32) + b_ref[...]
    pq = p[:, 0:MLSTM_WIDTH]
    pk = p[:, MLSTM_WIDTH:QK_COLS] * (HEAD_DIM ** -0.5)
    pv = p[:, QK_COLS:QKV_COLS]
    k_ref[0] = pk.astype(BF16)
    rest_ref[0] = p[:, QKV_COLS:MAIN_COLS]
    for ch in range(tm // CHUNK):
        sl = slice(ch * CHUNK, (ch + 1) * CHUNK)
        for head in range(HEADS):
            hs = slice(head * HEAD_DIM, (head + 1) * HEAD_DIM)
            qt_ref[0, ch, hs, :] = pq[sl, hs].T.astype(BF16)
            kt_ref[0, ch, hs, :] = pk[sl, hs].T.astype(BF16)
            vt_ref[0, ch, hs, :] = pv[sl, hs].T.astype(BF16)


def _inproj(x, mod, mod_row, ng, w_main, b_main, w_gate, b_gate, tm):
    bsz, seq, _ = x.shape
    nch = seq // CHUNK
    tch = tm // CHUNK
    const = lambda b, t: (0, 0)
    return pl.pallas_call(
        functools.partial(_inproj_kernel, tm=tm),
        grid=(bsz, seq // tm),
        in_specs=[
            pl.BlockSpec((1, tm, D_MODEL), lambda b, t: (b, t, 0)),
            pl.BlockSpec((1, 6, D_MODEL), mod_row),
            pl.BlockSpec((1, D_MODEL), const),
            pl.BlockSpec((D_MODEL, MAIN_COLS), const),
            pl.BlockSpec((1, MAIN_COLS), const),
            pl.BlockSpec((D_MODEL, 2 * LANES), const),
            pl.BlockSpec((1, LANES), const),
        ],
        out_specs=[
            pl.BlockSpec((1, tm, MLSTM_WIDTH), lambda b, t: (b, t, 0)),
            pl.BlockSpec((1, tch, MLSTM_WIDTH, CHUNK), lambda b, t: (b, t, 0, 0)),
            pl.BlockSpec((1, tch, MLSTM_WIDTH, CHUNK), lambda b, t: (b, t, 0, 0)),
            pl.BlockSpec((1, tch, MLSTM_WIDTH, CHUNK), lambda b, t: (b, t, 0, 0)),
            pl.BlockSpec((1, tm, REST_COLS), lambda b, t: (b, t, 0)),
            pl.BlockSpec((1, tch, N_ROWQ * UNITS, CHUNK), lambda b, t: (b, t, 0, 0)),
            pl.BlockSpec((1, tm, LANES), lambda b, t: (b, t, 0)),
        ],
        out_shape=[
            jax.ShapeDtypeStruct((bsz, seq, MLSTM_WIDTH), BF16),
            jax.ShapeDtypeStruct((bsz, nch, MLSTM_WIDTH, CHUNK), BF16),
            jax.ShapeDtypeStruct((bsz, nch, MLSTM_WIDTH, CHUNK), BF16),
            jax.ShapeDtypeStruct((bsz, nch, MLSTM_WIDTH, CHUNK), BF16),
            jax.ShapeDtypeStruct((bsz, seq, REST_COLS), F32),
            jax.ShapeDtypeStruct((bsz, nch, N_ROWQ * UNITS, CHUNK), F32),
            jax.ShapeDtypeStruct((bsz, seq, LANES), F32),
        ],
        compiler_params=_params("parallel", "parallel"),
        name="inproj",
    )(x, mod, ng, w_main, b_main, w_gate, b_gate)


def _mlstm_kernel(kc_ref, qtc_ref, ktc_ref, vtc_ref, rowc_ref, ecc_ref,
                  kx_ref, qtx_ref, ktx_ref, vtx_ref, rowx_ref, ecx_ref, *rest, ctx_out, n_ctx, n_x):
    if ctx_out:
        hc_ref, hx_ref, ct_ref, m_ref, part_ref = rest
    else:
        hx_ref, ct_ref, m_ref, part_ref = rest
        hc_ref = None
    ct_ref[...] = jnp.zeros(ct_ref.shape, F32)
    m_ref[...] = jnp.zeros(m_ref.shape, F32)
    r = lax.broadcasted_iota(jnp.int32, (CHUNK, CHUNK), 0)
    c = lax.broadcasted_iota(jnp.int32, (CHUNK, CHUNK), 1)
    masks = (r <= c, r >= c)
    ones = jnp.ones((BF16_ROWS, CHUNK), BF16)
    zeros = jnp.zeros((CHUNK, CHUNK), BF16)

    def block_diag(a, b):
        return jnp.concatenate([jnp.concatenate([a, zeros], axis=1),
                                jnp.concatenate([zeros, b], axis=1)], axis=0)

    def bf16_rows(row):
        tile = jnp.broadcast_to(row, (BF16_ROWS, CHUNK)).astype(BF16)
        return jnp.concatenate([tile] * (CHUNK // BF16_ROWS), axis=0)

    def pair(refs, j, hp, direction, out_ref, mode, slot):
        k_ref, qt_ref, kt_ref, vt_ref, row_ref, ec_ref = refs
        heads = (2 * hp, 2 * hp + 1)
        units = tuple(direction * HEADS + h for h in heads)
        sp = direction * (HEADS // 2) + hp
        r0 = j * CHUNK if isinstance(j, int) else pl.multiple_of(j * CHUNK, CHUNK)
        toks = pl.ds(r0, CHUNK)
        hsl = [slice(h * HEAD_DIM, (h + 1) * HEAD_DIM) for h in heads]

        def row(q, u):
            return row_ref[0, j, q * UNITS + u:q * UNITS + u + 1, :]

        e, cme, b, b_last, cme_last = ([row(q, u) for u in units] for q in range(N_ROWQ))
        m_old = [m_ref[u, 0:1, :] for u in units]
        ct_old = ct_ref[sp]
        vt_aug = [jnp.concatenate([vt_ref[0, j, s, :], ones], axis=0) for s in hsl]

        if mode is not None:
            qt = [qt_ref[0, j, s, :] for s in hsl]
            k2 = k_ref[0, toks, 2 * hp * HEAD_DIM:(2 * hp + 2) * HEAD_DIM]
            qkt = jnp.dot(k2, block_diag(qt[0], qt[1]), preferred_element_type=F32)

        g_last = [jnp.maximum(m_old[i], cme_last[i]) for i in range(2)]
        kst = [kt_ref[0, j, hsl[i], :] * bf16_rows(jnp.exp(e[i] - g_last[i])) for i in range(2)]
        upd = lax.dot_general(jnp.concatenate(vt_aug, axis=1), block_diag(kst[0], kst[1]), NT,
                              preferred_element_type=F32)
        keep = jnp.concatenate([jnp.exp(m_old[i] - g_last[i]) for i in range(2)], axis=1)
        ct_ref[sp] = keep * ct_old + upd
        for i in range(2):
            m_ref[units[i]] = jnp.broadcast_to(b_last[i] + g_last[i], m_ref.shape[1:])

        yield
        if mode is None:
            return
        ct_b = ct_old.astype(BF16)
        for i in range(2):
            u, cols = units[i], hsl[i]
            g = jnp.maximum(m_old[i], cme[i])
            dl = jnp.where(masks[direction], ec_ref[0, toks, u:u + 1] - g, -jnp.inf)
            st = (qkt[:, i * CHUNK:(i + 1) * CHUNK] * jnp.exp(dl)).astype(BF16)
            qs = qt[i] * bf16_rows(jnp.exp(m_old[i] - g))
            res = jnp.dot(jnp.concatenate([vt_aug[i], ct_b[:, i * HEAD_DIM:(i + 1) * HEAD_DIM]], axis=1),
                          jnp.concatenate([st, qs], axis=0), preferred_element_type=F32)
            den = jnp.maximum(jnp.abs(res[HEAD_DIM:HEAD_DIM + 1, :]), jnp.exp(-(b[i] + g)))
            ht = res[0:HEAD_DIM, :] * (1.0 / den)
            if mode == "first":
                part_ref[direction, slot, cols, :] = ht
            else:
                tot = part_ref[1 - direction, slot, cols, :] + ht
                hn = tot * lax.rsqrt(jnp.mean(tot * tot, axis=0, keepdims=True) + EPS)
                out_ref[0, toks, cols] = hn.T

    def step(refs, j, n, out_ref, mode):
        slot = None if mode is None else (j if mode == "first" else n - 1 - j)
        pairs = []
        for hp in range(HEADS // 2):
            pairs.append(pair(refs, j, hp, 0, out_ref, mode, slot))
            pairs.append(pair(refs, n - 1 - j, hp, 1, out_ref, mode, slot))
        for phase in range(2):
            for p in pairs:
                next(p, None)

    crefs = (kc_ref, qtc_ref, ktc_ref, vtc_ref, rowc_ref, ecc_ref)
    xrefs = (kx_ref, qtx_ref, ktx_ref, vtx_ref, rowx_ref, ecx_ref)
    for j in range(n_ctx):
        mode = None if not ctx_out else ("first" if j < n_ctx // 2 else "second")
        step(crefs, j, n_ctx, hc_ref, mode)

    def body(j, carry, mode):
        step(xrefs, j, n_x, hx_ref, mode)
        return carry

    lax.fori_loop(0, n_x // 2, functools.partial(body, mode="first"), 0, unroll=2)
    lax.fori_loop(n_x // 2, n_x, functools.partial(body, mode="second"), 0, unroll=2)


def _mlstm(ctx_in, x_in, ctx_out):
    bsz, lc, _ = ctx_in[0].shape
    lx = x_in[0].shape[1]
    n_ctx, n_x = lc // CHUNK, lx // CHUNK
    assert n_ctx % 2 == 0 and n_x % 4 == 0
    i3 = lambda b: (b, 0, 0)
    i4 = lambda b: (b, 0, 0, 0)

    def specs(seq, nch):
        return [
            pl.BlockSpec((1, seq, MLSTM_WIDTH), i3),
            pl.BlockSpec((1, nch, MLSTM_WIDTH, CHUNK), i4),
            pl.BlockSpec((1, nch, MLSTM_WIDTH, CHUNK), i4),
            pl.BlockSpec((1, nch, MLSTM_WIDTH, CHUNK), i4),
            pl.BlockSpec((1, nch, N_ROWQ * UNITS, CHUNK), i4),
            pl.BlockSpec((1, seq, LANES), i3),
        ]

    out_specs = [pl.BlockSpec((1, lx, MLSTM_WIDTH), i3)]
    out_shape = [jax.ShapeDtypeStruct((bsz, lx, MLSTM_WIDTH), F32)]
    if ctx_out:
        out_specs.insert(0, pl.BlockSpec((1, lc, MLSTM_WIDTH), i3))
        out_shape.insert(0, jax.ShapeDtypeStruct((bsz, lc, MLSTM_WIDTH), F32))
    return pl.pallas_call(
        functools.partial(_mlstm_kernel, ctx_out=ctx_out, n_ctx=n_ctx, n_x=n_x),
        grid=(bsz,),
        in_specs=specs(lc, n_ctx) + specs(lx, n_x),
        out_specs=out_specs,
        out_shape=out_shape,
        scratch_shapes=[
            pltpu.VMEM((UNITS // 2, AUG, 2 * HEAD_DIM), F32),
            pltpu.VMEM((UNITS, SUBLANES, LANES), F32),
            pltpu.VMEM((2, max(n_ctx, n_x) // 2, MLSTM_WIDTH, CHUNK), F32),
        ],
        compiler_params=_params("parallel"),
        name="mlstm",
    )(*ctx_in, *x_in)


def _shift_conv(y, w_ref, lo, width, pos):
    n = y.shape[0]
    acc = jnp.zeros(y.shape, F32)
    for j in range(CONV_K):
        d = j - CONV_R
        w = w_ref[j:j + 1, lo:lo + y.shape[1]]
        if d == 0:
            acc = acc + w * y
        else:
            sh = pltpu.roll(y, (-d) % n, axis=0)
            valid = (pos >= -d) if d < 0 else (pos < width - d)
            acc = acc + w * jnp.where(valid, sh, 0.0)
    return acc


def _mix_kernel(x_ref, mod_ref, os_ref, h_ref, z_ref, mg_ref, slg_ref, slb_ref, sw_ref, sb_ref,
                cw_ref, cb_ref, clg_ref, clb_ref, wo_ref, out_ref, *scratch, seq, tm, grid_mode):
    t = pl.program_id(1)
    r0 = pl.multiple_of(t * tm, tm)
    rows = pl.ds(r0, tm)
    half = CONV_WIDTH // 2
    pad = CONV_R * GRID_W

    if grid_mode:
        (ycol_ref,) = scratch

        @pl.when(t == 0)
        def _():
            ycol_ref[0:pad, :] = jnp.zeros((pad, half), F32)
            ycol_ref[pad + seq:pad + seq + pad, :] = jnp.zeros((pad, half), F32)
            ycol_ref[pad:pad + seq, :] = z_ref[0, :, half:CONV_WIDTH] * _sigmoid(
                z_ref[0, :, CONV_WIDTH + half:2 * CONV_WIDTH])

        pos = lax.broadcasted_iota(jnp.int32, (tm, half), 0) & (GRID_W - 1)
        y_row = z_ref[0, rows, 0:half] * _sigmoid(z_ref[0, rows, CONV_WIDTH:CONV_WIDTH + half])
        conv_r = _shift_conv(y_row, cw_ref, 0, GRID_W, pos)
        conv_c = jnp.zeros((tm, half), F32)
        for j in range(CONV_K):
            start = pl.multiple_of(r0 + GRID_W * j, GRID_W)
            conv_c = conv_c + cw_ref[j:j + 1, half:CONV_WIDTH] * ycol_ref[pl.ds(start, tm), :]
        conv = jnp.concatenate([conv_r, conv_c], axis=1)
    else:
        pos = lax.broadcasted_iota(jnp.int32, (tm, CONV_WIDTH), 0)
        y = z_ref[0, :, 0:CONV_WIDTH] * _sigmoid(z_ref[0, :, CONV_WIDTH:2 * CONV_WIDTH])
        conv = _shift_conv(y, cw_ref, 0, tm, pos)
    y_c = _silu(_layernorm(conv + cb_ref[...], clg_ref[...], clb_ref[...]))

    zs = _gelu_tanh(os_ref[0, :, MLSTM_WIDTH:MLSTM_WIDTH + 2 * SGU_WIDTH])
    u = zs[:, 0:SGU_WIDTH]
    vn = _layernorm(zs[:, SGU_WIDTH:2 * SGU_WIDTH], slg_ref[...], slb_ref[...]).astype(BF16)
    lane_group = lax.broadcasted_iota(jnp.int32, (CHUNK, SGU_WIDTH), 1) // (SGU_WIDTH // SGU_GROUPS)
    mixed = []
    for ch in range(tm // CHUNK):
        vc = vn[ch * CHUNK:(ch + 1) * CHUNK]
        m = sb_ref[...]
        for g in range(SGU_GROUPS):
            m = m + jnp.where(lane_group == g, jnp.dot(sw_ref[g], vc, preferred_element_type=F32), 0.0)
        mixed.append(m)
    y_b = u * jnp.concatenate(mixed, axis=0)

    y_a = h_ref[0] * mg_ref[...] * _sigmoid(os_ref[0, :, 0:MLSTM_WIDTH])

    o_a, o_b = MLSTM_WIDTH, MLSTM_WIDTH + SGU_WIDTH
    proj = jnp.dot(y_a.astype(BF16), wo_ref[0:o_a, :], preferred_element_type=F32)
    proj = proj + jnp.dot(y_b.astype(BF16), wo_ref[o_a:o_b, :], preferred_element_type=F32)
    proj = proj + jnp.dot(y_c.astype(BF16), wo_ref[o_b:, :], preferred_element_type=F32)
    out_ref[0] = x_ref[0] + mod_ref[0, 2:3, :] * proj


def _mix(x, mod, mod_row, rest, h, mg, slg, slb, sw, sbias, cw, cb, clg, clb, wo, tm, grid_mode):
    bsz, seq, _ = x.shape
    const2 = lambda b, t: (0, 0)
    const3 = lambda b, t: (0, 0, 0)
    scratch = []
    if grid_mode:
        scratch.append(pltpu.VMEM((seq + 2 * CONV_R * GRID_W, CONV_WIDTH // 2), F32))
    return pl.pallas_call(
        functools.partial(_mix_kernel, seq=seq, tm=tm, grid_mode=grid_mode),
        grid=(bsz, seq // tm),
        in_specs=[
            pl.BlockSpec((1, tm, D_MODEL), lambda b, t: (b, t, 0)),
            pl.BlockSpec((1, 6, D_MODEL), mod_row),
            pl.BlockSpec((1, tm, MLSTM_WIDTH + 2 * SGU_WIDTH), lambda b, t: (b, t, 0)),
            pl.BlockSpec((1, tm, MLSTM_WIDTH), lambda b, t: (b, t, 0)),
            pl.BlockSpec((1, seq, 2 * CONV_WIDTH), lambda b, t: (b, 0, 2)),
            pl.BlockSpec((1, MLSTM_WIDTH), const2),
            pl.BlockSpec((1, SGU_WIDTH), const2),
            pl.BlockSpec((1, SGU_WIDTH), const2),
            pl.BlockSpec((SGU_GROUPS, CHUNK, CHUNK), const3),
            pl.BlockSpec((CHUNK, SGU_WIDTH), const2),
            pl.BlockSpec((CONV_K, CONV_WIDTH), const2),
            pl.BlockSpec((1, CONV_WIDTH), const2),
            pl.BlockSpec((1, CONV_WIDTH), const2),
            pl.BlockSpec((1, CONV_WIDTH), const2),
            pl.BlockSpec((D_MODEL, D_MODEL), const2),
        ],
        out_specs=pl.BlockSpec((1, tm, D_MODEL), lambda b, t: (b, t, 0)),
        out_shape=jax.ShapeDtypeStruct((bsz, seq, D_MODEL), F32),
        scratch_shapes=scratch,
        compiler_params=_params("parallel", "arbitrary"),
        name="mix",
    )(x, mod, rest, h, rest, mg, slg, slb, sw, sbias, cw, cb, clg, clb, wo)


def _ffn_kernel(x_ref, mod_ref, ng_ref, wgu_ref, wd_ref, fg_ref, out_ref, *, final):
    x = x_ref[0]
    xm = _modulated_rmsnorm(x, ng_ref[...], mod_ref[0, 4:5, :], mod_ref[0, 3:4, :]).astype(BF16)
    acc = jnp.zeros(x.shape, F32)
    for ch in range(FFN_HIDDEN // FFN_CHUNK):
        lo = ch * FFN_CHUNK
        gate = jnp.dot(xm, wgu_ref[:, lo:lo + FFN_CHUNK], preferred_element_type=F32)
        up = jnp.dot(xm, wgu_ref[:, FFN_HIDDEN + lo:FFN_HIDDEN + lo + FFN_CHUNK], preferred_element_type=F32)
        act = (_silu(gate) * up).astype(BF16)
        acc = acc + jnp.dot(act, wd_ref[lo:lo + FFN_CHUNK, :], preferred_element_type=F32)
    y = x + mod_ref[0, 5:6, :] * acc
    if final:
        y = _rmsnorm(y, fg_ref[...])
    out_ref[0] = y


def _ffn(x, mod, mod_row, ng, wgu, wd, fg, tm, final):
    bsz, seq, _ = x.shape
    const = lambda b, t: (0, 0)
    return pl.pallas_call(
        functools.partial(_ffn_kernel, final=final),
        grid=(bsz, seq // tm),
        in_specs=[
            pl.BlockSpec((1, tm, D_MODEL), lambda b, t: (b, t, 0)),
            pl.BlockSpec((1, 6, D_MODEL), mod_row),
            pl.BlockSpec((1, D_MODEL), const),
            pl.BlockSpec((D_MODEL, 2 * FFN_HIDDEN), const),
            pl.BlockSpec((FFN_HIDDEN, D_MODEL), const),
            pl.BlockSpec((1, D_MODEL), const),
        ],
        out_specs=pl.BlockSpec((1, tm, D_MODEL), lambda b, t: (b, t, 0)),
        out_shape=jax.ShapeDtypeStruct((bsz, seq, D_MODEL), F32),
        compiler_params=_params("parallel", "parallel"),
        name="ffn",
    )(x, mod, ng, wgu, wd, fg)


def _gate_weights(w_in_l, b_in_l):
    wg = w_in_l[:, GATE_OFF:SGU_OFF]
    bg = b_in_l[GATE_OFF:SGU_OFF]
    order = [wg[:, 0:HEADS], wg[:, 2 * HEADS:3 * HEADS], wg[:, HEADS:2 * HEADS], wg[:, 3 * HEADS:4 * HEADS]]
    w = jnp.pad(jnp.concatenate(order, axis=1), ((0, 0), (0, LANES - 2 * UNITS)))
    b_order = [bg[0:HEADS], bg[2 * HEADS:3 * HEADS], bg[HEADS:2 * HEADS], bg[3 * HEADS:4 * HEADS]]
    b = jnp.pad(jnp.concatenate(b_order), (0, LANES - 2 * UNITS))[None, :]
    w_hi = w.astype(BF16)
    w_lo = (w - w_hi.astype(F32)).astype(BF16)
    return jnp.concatenate([w_hi, w_lo], axis=1), b


def kernel(x, c, ctx, c_ctx, w_mod, b_mod, norm1_g, w_in, b_in, mlstm_g, sgu_ln_g, sgu_ln_b, sgu_w, sgu_b,
           conv_w, conv_b, conv_ln_g, conv_ln_b, w_out, norm2_g, w_gu, w_down, final_g):
    bsz = x.shape[0]
    lc = ctx.shape[1]
    assert bsz + 1 <= MOD_ROWS
    c_all = jnp.zeros((MOD_ROWS, D_MODEL), F32).at[:bsz].set(c).at[bsz].set(c_ctx)
    mod_all = _modulation(c_all, w_mod, b_mod).reshape(DEPTH, MOD_ROWS, 6, D_MODEL)
    row_x = lambda b, t: (b, 0, 0)
    row_c = lambda b, t: (bsz, 0, 0)
    tm_x, tm_c = 512, lc

    xc = ctx
    for l in range(DEPTH):
        last = l == DEPTH - 1
        mod = mod_all[l]
        w_main = jnp.concatenate([w_in[l][:, :GATE_OFF], w_in[l][:, SGU_OFF:]], axis=1).astype(BF16)
        b_main = jnp.concatenate([b_in[l][:GATE_OFF], b_in[l][SGU_OFF:]])[None, :]
        w_gate, b_gate = _gate_weights(w_in[l], b_in[l])
        ng1, ng2 = norm1_g[l][None, :], norm2_g[l][None, :]
        mix_w = (mlstm_g[l][None, :], sgu_ln_g[l][None, :], sgu_ln_b[l][None, :], sgu_w[l].astype(BF16),
                 jnp.repeat(sgu_b[l].T, SGU_WIDTH // SGU_GROUPS, axis=1),
                 conv_w[l], conv_b[l][None, :], conv_ln_g[l][None, :], conv_ln_b[l][None, :],
                 w_out[l].astype(BF16))
        wgu, wd = w_gu[l].astype(BF16), w_down[l].astype(BF16)
        fg = final_g[None, :]

        k_x, qt_x, kt_x, vt_x, rest_x, rows_x, ecol_x = _inproj(
            x, mod, row_x, ng1, w_main, b_main, w_gate, b_gate, tm_x)
        ctx_proj = _inproj(xc.reshape(1, bsz * lc, D_MODEL), mod, row_c, ng1, w_main, b_main, w_gate, b_gate,
                           tm_x)
        k_c, qt_c, kt_c, vt_c, rest_c, rows_c, ecol_c = (
            a.reshape((bsz, a.shape[1] // bsz) + a.shape[2:]) for a in ctx_proj)
        hs = _mlstm((k_c, qt_c, kt_c, vt_c, rows_c, ecol_c), (k_x, qt_x, kt_x, vt_x, rows_x, ecol_x),
                    ctx_out=not last)
        x = _mix(x, mod, row_x, rest_x, hs[-1], *mix_w, tm=tm_x, grid_mode=True)
        x = _ffn(x, mod, row_x, ng2, wgu, wd, fg, tm_x, final=last)
        if not last:
            xc = _mix(xc, mod, row_c, rest_c, hs[0], *mix_w, tm=tm_c, grid_mode=False)
            xc = _ffn(xc.reshape(1, bsz * lc, D_MODEL), mod, row_c, ng2, wgu, wd, fg, tm_x,
                      final=False).reshape(bsz, lc, D_MODEL)
    return x
```

```python
import functools

import jax
import jax.numpy as jnp
from jax import lax
from jax.experimental import pallas as pl
from jax.experimental.pallas import tpu as pltpu

D_MODEL = 1024
DEPTH = 2
GRID_W = 64
EPS = 1e-6

HEADS = 4
HEAD_DIM = 128
MLSTM_WIDTH = HEADS * HEAD_DIM
CHUNK = 128
UNITS = 2 * HEADS
SGU_GROUPS = 4
SGU_WIDTH = 256
CONV_WIDTH = 256
CONV_K = 31
CONV_R = CONV_K // 2
FFN_HIDDEN = 2816
FFN_CHUNK = 256

GATE_OFF = 4 * MLSTM_WIDTH
N_GATES = 4 * HEADS
SGU_OFF = GATE_OFF + N_GATES
QKV_COLS = 3 * MLSTM_WIDTH
QK_COLS = 2 * MLSTM_WIDTH
REST_COLS = MLSTM_WIDTH + 2 * SGU_WIDTH + 2 * CONV_WIDTH
MAIN_COLS = QKV_COLS + REST_COLS
LANES = 128
SUBLANES = 8
BF16_ROWS = 16
MOD_ROWS = 24
N_ROWQ = 5
AUG = HEAD_DIM + BF16_ROWS

VMEM_LIMIT = 56 * 1024 * 1024

F32 = jnp.float32
BF16 = jnp.bfloat16
HIGHEST = lax.Precision.HIGHEST
NT = (((1,), (1,)), ((), ()))

GELU_C = 0.7978845608028654
GELU_A = 0.044715


def _sigmoid(t):
    return 0.5 * jnp.tanh(0.5 * t) + 0.5


def _silu(t):
    return t * _sigmoid(t)


def _gelu_tanh(t):
    inner = t * (GELU_C * GELU_A * (t * t) + GELU_C)
    return t * (0.5 * jnp.tanh(inner) + 0.5)


def _log_sigmoid(t):
    return jnp.minimum(t, 0.0) - jnp.log1p(jnp.exp(-jnp.abs(t)))


def _layernorm(t, g, b):
    mu = jnp.mean(t, axis=-1, keepdims=True)
    d = t - mu
    var = jnp.mean(d * d, axis=-1, keepdims=True)
    return d * lax.rsqrt(var + EPS) * g + b


def _rmsnorm(t, g):
    return t * lax.rsqrt(jnp.mean(t * t, axis=-1, keepdims=True) + EPS) * g


def _modulated_rmsnorm(t, g, scale, shift):
    return t * lax.rsqrt(jnp.mean(t * t, axis=-1, keepdims=True) + EPS) * (g * (1.0 + scale)) + shift


def _params(*sem):
    return pltpu.CompilerParams(dimension_semantics=sem, vmem_limit_bytes=VMEM_LIMIT)


def _mod_kernel(c_ref, w_ref, b_ref, o_ref):
    a = _silu(c_ref[...])
    o_ref[0] = jnp.dot(a, w_ref[0], precision=HIGHEST, preferred_element_type=F32) + b_ref[0]


def _modulation(c_all, w_mod, b_mod):
    tn = 1536
    n = w_mod.shape[-1]
    return pl.pallas_call(
        _mod_kernel,
        grid=(DEPTH, n // tn),
        in_specs=[
            pl.BlockSpec((MOD_ROWS, D_MODEL), lambda l, j: (0, 0)),
            pl.BlockSpec((1, D_MODEL, tn), lambda l, j: (l, 0, j)),
            pl.BlockSpec((1, 1, tn), lambda l, j: (l, 0, j)),
        ],
        out_specs=pl.BlockSpec((1, MOD_ROWS, tn), lambda l, j: (l, 0, j)),
        out_shape=jax.ShapeDtypeStruct((DEPTH, MOD_ROWS, n), F32),
        compiler_params=_params("parallel", "parallel"),
        name="modulation",
    )(c_all, w_mod, b_mod.reshape(DEPTH, 1, n))


def _lane_scan(x, op, lane, forward):
    n = x.shape[1]
    k = 1
    while k < n:
        if forward:
            x = jnp.where(lane >= k, op(x, pltpu.roll(x, k, axis=1)), x)
        else:
            x = jnp.where(lane < n - k, op(x, pltpu.roll(x, n - k, axis=1)), x)
        k *= 2
    return x


def _inproj_kernel(x_ref, mod_ref, ng_ref, w_ref, b_ref, wg_ref, bg_ref,
                   k_ref, qt_ref, kt_ref, vt_ref, rest_ref, rows_ref, ecol_ref, *, tm):
    x = x_ref[0]
    xm = _modulated_rmsnorm(x, ng_ref[...], mod_ref[0, 1:2, :], mod_ref[0, 0:1, :])
    xh = xm.astype(BF16)

    g2 = jnp.dot(xh, wg_ref[...], preferred_element_type=F32)
    g = g2[:, 0:LANES] + g2[:, LANES:2 * LANES] + bg_ref[...]
    lane = lax.broadcasted_iota(jnp.int32, (UNITS, CHUNK), 1)
    fwd_row = lax.broadcasted_iota(jnp.int32, (UNITS, CHUNK), 0) < HEADS
    pad_rows = jnp.zeros((CHUNK - UNITS, CHUNK), F32)
    for ch in range(tm // CHUNK):
        sl = slice(ch * CHUNK, (ch + 1) * CHUNK)
        gt = g[sl].T
        it = gt[0:UNITS]
        lsf = _log_sigmoid(gt[UNITS:2 * UNITS])
        bt = jnp.where(fwd_row, _lane_scan(lsf, jnp.add, lane, True), _lane_scan(lsf, jnp.add, lane, False))
        et = it - bt
        cme = jnp.where(fwd_row, _lane_scan(et, jnp.maximum, lane, True),
                        _lane_scan(et, jnp.maximum, lane, False))
        b_last = jnp.where(fwd_row, jnp.broadcast_to(bt[:, CHUNK - 1:CHUNK], bt.shape),
                           jnp.broadcast_to(bt[:, 0:1], bt.shape))
        cme_last = jnp.broadcast_to(jnp.max(et, axis=1, keepdims=True), et.shape)
        rows_ref[0, ch] = jnp.concatenate([et, cme, bt, b_last, cme_last], axis=0)
        ecol_ref[0, sl, :] = jnp.concatenate([et, pad_rows], axis=0).T

    p = jnp.dot(xh, w_ref[...], preferred_element_type=F32) + b_ref[...]
    pq = p[:, 0:MLSTM_WIDTH]
    pk = p[:, MLSTM_WIDTH:QK_COLS] * (HEAD_DIM ** -0.5)
    pv = p[:, QK_COLS:QKV_COLS]
    k_ref[0] = pk.astype(BF16)
    rest_ref[0] = p[:, QKV_COLS:MAIN_COLS]
    for ch in range(tm // CHUNK):
        sl = slice(ch * CHUNK, (ch + 1) * CHUNK)
        for head in range(HEADS):
            hs = slice(head * HEAD_DIM, (head + 1) * HEAD_DIM)
            qt_ref[0, ch, hs, :] = pq[sl, hs].T.astype(BF16)
            kt_ref[0, ch, hs, :] = pk[sl, hs].T.astype(BF16)
            vt_ref[0, ch, hs, :] = pv[sl, hs].T.astype(BF16)


def _inproj(x, mod, mod_row, ng, w_main, b_main, w_gate, b_gate, tm):
    bsz, seq, _ = x.shape
    nch = seq // CHUNK
    tch = tm // CHUNK
    const = lambda b, t: (0, 0)
    return pl.pallas_call(
        functools.partial(_inproj_kernel, tm=tm),
        grid=(bsz, seq // tm),
        in_specs=[
            pl.BlockSpec((1, tm, D_MODEL), lambda b, t: (b, t, 0)),
            pl.BlockSpec((1, 6, D_MODEL), mod_row),
            pl.BlockSpec((1, D_MODEL), const),
            pl.BlockSpec((D_MODEL, MAIN_COLS), const),
            pl.BlockSpec((1, MAIN_COLS), const),
            pl.BlockSpec((D_MODEL, 2 * LANES), const),
            pl.BlockSpec((1, LANES), const),
        ],
        out_specs=[
            pl.BlockSpec((1, tm, MLSTM_WIDTH), lambda b, t: (b, t, 0)),
            pl.BlockSpec((1, tch, MLSTM_WIDTH, CHUNK), lambda b, t: (b, t, 0, 0)),
            pl.BlockSpec((1, tch, MLSTM_WIDTH, CHUNK), lambda b, t: (b, t, 0, 0)),
            pl.BlockSpec((1, tch, MLSTM_WIDTH, CHUNK), lambda b, t: (b, t, 0, 0)),
            pl.BlockSpec((1, tm, REST_COLS), lambda b, t: (b, t, 0)),
            pl.BlockSpec((1, tch, N_ROWQ * UNITS, CHUNK), lambda b, t: (b, t, 0, 0)),
            pl.BlockSpec((1, tm, LANES), lambda b, t: (b, t, 0)),
        ],
        out_shape=[
            jax.ShapeDtypeStruct((bsz, seq, MLSTM_WIDTH), BF16),
            jax.ShapeDtypeStruct((bsz, nch, MLSTM_WIDTH, CHUNK), BF16),
            jax.ShapeDtypeStruct((bsz, nch, MLSTM_WIDTH, CHUNK), BF16),
            jax.ShapeDtypeStruct((bsz, nch, MLSTM_WIDTH, CHUNK), BF16),
            jax.ShapeDtypeStruct((bsz, seq, REST_COLS), F32),
            jax.ShapeDtypeStruct((bsz, nch, N_ROWQ * UNITS, CHUNK), F32),
            jax.ShapeDtypeStruct((bsz, seq, LANES), F32),
        ],
        compiler_params=_params("parallel", "parallel"),
        name="inproj",
    )(x, mod, ng, w_main, b_main, w_gate, b_gate)


def _mlstm_kernel(kc_ref, qtc_ref, ktc_ref, vtc_ref, rowc_ref, ecc_ref,
                  kx_ref, qtx_ref, ktx_ref, vtx_ref, rowx_ref, ecx_ref, *rest, ctx_out, n_ctx, n_x):
    if ctx_out:
        hc_ref, hx_ref, ct_ref, m_ref, part_ref = rest
    else:
        hx_ref, ct_ref, m_ref, part_ref = rest
        hc_ref = None
    ct_ref[...] = jnp.zeros(ct_ref.shape, F32)
    m_ref[...] = jnp.zeros(m_ref.shape, F32)
    r = lax.broadcasted_iota(jnp.int32, (CHUNK, CHUNK), 0)
    c = lax.broadcasted_iota(jnp.int32, (CHUNK, CHUNK), 1)
    masks = (r <= c, r >= c)
    ones = jnp.ones((BF16_ROWS, CHUNK), BF16)
    zeros = jnp.zeros((CHUNK, CHUNK), BF16)

    def block_diag(a, b):
        return jnp.concatenate([jnp.concatenate([a, zeros], axis=1),
                                jnp.concatenate([zeros, b], axis=1)], axis=0)

    def bf16_rows(row):
        tile = jnp.broadcast_to(row, (BF16_ROWS, CHUNK)).astype(BF16)
        return jnp.concatenate([tile] * (CHUNK // BF16_ROWS), axis=0)

    def pair(refs, j, hp, direction, out_ref, mode, slot):
        k_ref, qt_ref, kt_ref, vt_ref, row_ref, ec_ref = refs
        heads = (2 * hp, 2 * hp + 1)
        units = tuple(direction * HEADS + h for h in heads)
        sp = direction * (HEADS // 2) + hp
        r0 = j * CHUNK if isinstance(j, int) else pl.multiple_of(j * CHUNK, CHUNK)
        toks = pl.ds(r0, CHUNK)
        hsl = [slice(h * HEAD_DIM, (h + 1) * HEAD_DIM) for h in heads]

        def row(q, u):
            return row_ref[0, j, q * UNITS + u:q * UNITS + u + 1, :]

        e, cme, b, b_last, cme_last = ([row(q, u) for u in units] for q in range(N_ROWQ))
        m_old = [m_ref[u, 0:1, :] for u in units]
        ct_old = ct_ref[sp]
        vt_aug = [jnp.concatenate([vt_ref[0, j, s, :], ones], axis=0) for s in hsl]

        if mode is not None:
            qt = [qt_ref[0, j, s, :] for s in hsl]
            k2 = k_ref[0, toks, 2 * hp * HEAD_DIM:(2 * hp + 2) * HEAD_DIM]
            qkt = jnp.dot(k2, block_diag(qt[0], qt[1]), preferred_element_type=F32)

        g_last = [jnp.maximum(m_old[i], cme_last[i]) for i in range(2)]
        kst = [kt_ref[0, j, hsl[i], :] * bf16_rows(jnp.exp(e[i] - g_last[i])) for i in range(2)]
        upd = lax.dot_general(jnp.concatenate(vt_aug, axis=1), block_diag(kst[0], kst[1]), NT,
                              preferred_element_type=F32)
        keep = jnp.concatenate([jnp.exp(m_old[i] - g_last[i]) for i in range(2)], axis=1)
        ct_ref[sp] = keep * ct_old + upd
        for i in range(2):
            m_ref[units[i]] = jnp.broadcast_to(b_last[i] + g_last[i], m_ref.shape[1:])

        yield
        if mode is None:
            return
        ct_b = ct_old.astype(BF16)
        for i in range(2):
            u, cols = units[i], hsl[i]
            g = jnp.maximum(m_old[i], cme[i])
            dl = jnp.where(masks[direction], ec_ref[0, toks, u:u + 1] - g, -jnp.inf)
            st = (qkt[:, i * CHUNK:(i + 1) * CHUNK] * jnp.exp(dl)).astype(BF16)
            qs = qt[i] * bf16_rows(jnp.exp(m_old[i] - g))
            res = jnp.dot(jnp.concatenate([vt_aug[i], ct_b[:, i * HEAD_DIM:(i + 1) * HEAD_DIM]], axis=1),
                          jnp.concatenate([st, qs], axis=0), preferred_element_type=F32)
            den = jnp.maximum(jnp.abs(res[HEAD_DIM:HEAD_DIM + 1, :]), jnp.exp(-(b[i] + g)))
            ht = res[0:HEAD_DIM, :] * (1.0 / den)
            if mode == "first":
                part_ref[direction, slot, cols, :] = ht
            else:
                tot = part_ref[1 - direction, slot, cols, :] + ht
                hn = tot * lax.rsqrt(jnp.mean(tot * tot, axis=0, keepdims=True) + EPS)
                out_ref[0, toks, cols] = hn.T

    def step(refs, j, n, out_ref, mode):
        slot = None if mode is None else (j if mode == "first" else n - 1 - j)
        pairs = []
        for hp in range(HEADS // 2):
            pairs.append(pair(refs, j, hp, 0, out_ref, mode, slot))
            pairs.append(pair(refs, n - 1 - j, hp, 1, out_ref, mode, slot))
        for phase in range(2):
            for p in pairs:
                next(p, None)

    crefs = (kc_ref, qtc_ref, ktc_ref, vtc_ref, rowc_ref, ecc_ref)
    xrefs = (kx_ref, qtx_ref, ktx_ref, vtx_ref, rowx_ref, ecx_ref)
    for j in range(n_ctx):
        mode = None if not ctx_out else ("first" if j < n_ctx // 2 else "second")
        step(crefs, j, n_ctx, hc_ref, mode)

    def body(j, carry, mode):
        step(xrefs, j, n_x, hx_ref, mode)
        return carry

    lax.fori_loop(0, n_x // 2, functools.partial(body, mode="first"), 0, unroll=2)
    lax.fori_loop(n_x // 2, n_x, functools.partial(body, mode="second"), 0, unroll=2)


def _mlstm(ctx_in, x_in, ctx_out):
    bsz, lc, _ = ctx_in[0].shape
    lx = x_in[0].shape[1]
    n_ctx, n_x = lc // CHUNK, lx // CHUNK
    assert n_ctx % 2 == 0 and n_x % 4 == 0
    i3 = lambda b: (b, 0, 0)
    i4 = lambda b: (b, 0, 0, 0)

    def specs(seq, nch):
        return [
            pl.BlockSpec((1, seq, MLSTM_WIDTH), i3),
            pl.BlockSpec((1, nch, MLSTM_WIDTH, CHUNK), i4),
            pl.BlockSpec((1, nch, MLSTM_WIDTH, CHUNK), i4),
            pl.BlockSpec((1, nch, MLSTM_WIDTH, CHUNK), i4),
            pl.BlockSpec((1, nch, N_ROWQ * UNITS, CHUNK), i4),
            pl.BlockSpec((1, seq, LANES), i3),
        ]

    out_specs = [pl.BlockSpec((1, lx, MLSTM_WIDTH), i3)]
    out_shape = [jax.ShapeDtypeStruct((bsz, lx, MLSTM_WIDTH), F32)]
    if ctx_out:
        out_specs.insert(0, pl.BlockSpec((1, lc, MLSTM_WIDTH), i3))
        out_shape.insert(0, jax.ShapeDtypeStruct((bsz, lc, MLSTM_WIDTH), F32))
    return pl.pallas_call(
        functools.partial(_mlstm_kernel, ctx_out=ctx_out, n_ctx=n_ctx, n_x=n_x),
        grid=(bsz,),
        in_specs=specs(lc, n_ctx) + specs(lx, n_x),
        out_specs=out_specs,
        out_shape=out_shape,
        scratch_shapes=[
            pltpu.VMEM((UNITS // 2, AUG, 2 * HEAD_DIM), F32),
            pltpu.VMEM((UNITS, SUBLANES, LANES), F32),
            pltpu.VMEM((2, max(n_ctx, n_x) // 2, MLSTM_WIDTH, CHUNK), F32),
        ],
        compiler_params=_params("parallel"),
        name="mlstm",
    )(*ctx_in, *x_in)


def _shift_conv(y, w_ref, lo, width, pos):
    n = y.shape[0]
    acc = jnp.zeros(y.shape, F32)
    for j in range(CONV_K):
        d = j - CONV_R
        w = w_ref[j:j + 1, lo:lo + y.shape[1]]
        if d == 0:
            acc = acc + w * y
        else:
            sh = pltpu.roll(y, (-d) % n, axis=0)
            valid = (pos >= -d) if d < 0 else (pos < width - d)
            acc = acc + w * jnp.where(valid, sh, 0.0)
    return acc


def _mix_kernel(x_ref, mod_ref, os_ref, h_ref, z_ref, mg_ref, slg_ref, slb_ref, sw_ref, sb_ref,
                cw_ref, cb_ref, clg_ref, clb_ref, wo_ref, out_ref, *scratch, seq, tm, grid_mode):
    t = pl.program_id(1)
    r0 = pl.multiple_of(t * tm, tm)
    rows = pl.ds(r0, tm)
    half = CONV_WIDTH // 2
    pad = CONV_R * GRID_W

    if grid_mode:
        (ycol_ref,) = scratch

        @pl.when(t == 0)
        def _():
            ycol_ref[0:pad, :] = jnp.zeros((pad, half), F32)
            ycol_ref[pad + seq:pad + seq + pad, :] = jnp.zeros((pad, half), F32)
            ycol_ref[pad:pad + seq, :] = z_ref[0, :, half:CONV_WIDTH] * _sigmoid(
                z_ref[0, :, CONV_WIDTH + half:2 * CONV_WIDTH])

        pos = lax.broadcasted_iota(jnp.int32, (tm, half), 0) & (GRID_W - 1)
        y_row = z_ref[0, rows, 0:half] * _sigmoid(z_ref[0, rows, CONV_WIDTH:CONV_WIDTH + half])
        conv_r = _shift_conv(y_row, cw_ref, 0, GRID_W, pos)
        conv_c = jnp.zeros((tm, half), F32)
        for j in range(CONV_K):
            start = pl.multiple_of(r0 + GRID_W * j, GRID_W)
            conv_c = conv_c + cw_ref[j:j + 1, half:CONV_WIDTH] * ycol_ref[pl.ds(start, tm), :]
        conv = jnp.concatenate([conv_r, conv_c], axis=1)
    else:
        pos = lax.broadcasted_iota(jnp.int32, (tm, CONV_WIDTH), 0)
        y = z_ref[0, :, 0:CONV_WIDTH] * _sigmoid(z_ref[0, :, CONV_WIDTH:2 * CONV_WIDTH])
        conv = _shift_conv(y, cw_ref, 0, tm, pos)
    y_c = _silu(_layernorm(conv + cb_ref[...], clg_ref[...], clb_ref[...]))

    zs = _gelu_tanh(os_ref[0, :, MLSTM_WIDTH:MLSTM_WIDTH + 2 * SGU_WIDTH])
    u = zs[:, 0:SGU_WIDTH]
    vn = _layernorm(zs[:, SGU_WIDTH:2 * SGU_WIDTH], slg_ref[...], slb_ref[...]).astype(BF16)
    lane_group = lax.broadcasted_iota(jnp.int32, (CHUNK, SGU_WIDTH), 1) // (SGU_WIDTH // SGU_GROUPS)
    mixed = []
    for ch in range(tm // CHUNK):
        vc = vn[ch * CHUNK:(ch + 1) * CHUNK]
        m = sb_ref[...]
        for g in range(SGU_GROUPS):
            m = m + jnp.where(lane_group == g, jnp.dot(sw_ref[g], vc, preferred_element_type=F32), 0.0)
        mixed.append(m)
    y_b = u * jnp.concatenate(mixed, axis=0)

    y_a = h_ref[0] * mg_ref[...] * _sigmoid(os_ref[0, :, 0:MLSTM_WIDTH])

    o_a, o_b = MLSTM_WIDTH, MLSTM_WIDTH + SGU_WIDTH
    proj = jnp.dot(y_a.astype(BF16), wo_ref[0:o_a, :], preferred_element_type=F32)
    proj = proj + jnp.dot(y_b.astype(BF16), wo_ref[o_a:o_b, :], preferred_element_type=F32)
    proj = proj + jnp.dot(y_c.astype(BF16), wo_ref[o_b:, :], preferred_element_type=F32)
    out_ref[0] = x_ref[0] + mod_ref[0, 2:3, :] * proj


def _mix(x, mod, mod_row, rest, h, mg, slg, slb, sw, sbias, cw, cb, clg, clb, wo, tm, grid_mode):
    bsz, seq, _ = x.shape
    const2 = lambda b, t: (0, 0)
    const3 = lambda b, t: (0, 0, 0)
    scratch = []
    if grid_mode:
        scratch.append(pltpu.VMEM((seq + 2 * CONV_R * GRID_W, CONV_WIDTH // 2), F32))
    return pl.pallas_call(
        functools.partial(_mix_kernel, seq=seq, tm=tm, grid_mode=grid_mode),
        grid=(bsz, seq // tm),
        in_specs=[
            pl.BlockSpec((1, tm, D_MODEL), lambda b, t: (b, t, 0)),
            pl.BlockSpec((1, 6, D_MODEL), mod_row),
            pl.BlockSpec((1, tm, MLSTM_WIDTH + 2 * SGU_WIDTH), lambda b, t: (b, t, 0)),
            pl.BlockSpec((1, tm, MLSTM_WIDTH), lambda b, t: (b, t, 0)),
            pl.BlockSpec((1, seq, 2 * CONV_WIDTH), lambda b, t: (b, 0, 2)),
            pl.BlockSpec((1, MLSTM_WIDTH), const2),
            pl.BlockSpec((1, SGU_WIDTH), const2),
            pl.BlockSpec((1, SGU_WIDTH), const2),
            pl.BlockSpec((SGU_GROUPS, CHUNK, CHUNK), const3),
            pl.BlockSpec((CHUNK, SGU_WIDTH), const2),
            pl.BlockSpec((CONV_K, CONV_WIDTH), const2),
            pl.BlockSpec((1, CONV_WIDTH), const2),
            pl.BlockSpec((1, CONV_WIDTH), const2),
            pl.BlockSpec((1, CONV_WIDTH), const2),
            pl.BlockSpec((D_MODEL, D_MODEL), const2),
        ],
        out_specs=pl.BlockSpec((1, tm, D_MODEL), lambda b, t: (b, t, 0)),
        out_shape=jax.ShapeDtypeStruct((bsz, seq, D_MODEL), F32),
        scratch_shapes=scratch,
        compiler_params=_params("parallel", "arbitrary"),
        name="mix",
    )(x, mod, rest, h, rest, mg, slg, slb, sw, sbias, cw, cb, clg, clb, wo)


def _ffn_kernel(x_ref, mod_ref, ng_ref, wgu_ref, wd_ref, fg_ref, out_ref, *, final):
    x = x_ref[0]
    xm = _modulated_rmsnorm(x, ng_ref[...], mod_ref[0, 4:5, :], mod_ref[0, 3:4, :]).astype(BF16)
    acc = jnp.zeros(x.shape, F32)
    for ch in range(FFN_HIDDEN // FFN_CHUNK):
        lo = ch * FFN_CHUNK
        gate = jnp.dot(xm, wgu_ref[:, lo:lo + FFN_CHUNK], preferred_element_type=F32)
        up = jnp.dot(xm, wgu_ref[:, FFN_HIDDEN + lo:FFN_HIDDEN + lo + FFN_CHUNK], preferred_element_type=F32)
        act = (_silu(gate) * up).astype(BF16)
        acc = acc + jnp.dot(act, wd_ref[lo:lo + FFN_CHUNK, :], preferred_element_type=F32)
    y = x + mod_ref[0, 5:6, :] * acc
    if final:
        y = _rmsnorm(y, fg_ref[...])
    out_ref[0] = y


def _ffn(x, mod, mod_row, ng, wgu, wd, fg, tm, final):
    bsz, seq, _ = x.shape
    const = lambda b, t: (0, 0)
    return pl.pallas_call(
        functools.partial(_ffn_kernel, final=final),
        grid=(bsz, seq // tm),
        in_specs=[
            pl.BlockSpec((1, tm, D_MODEL), lambda b, t: (b, t, 0)),
            pl.BlockSpec((1, 6, D_MODEL), mod_row),
            pl.BlockSpec((1, D_MODEL), const),
            pl.BlockSpec((D_MODEL, 2 * FFN_HIDDEN), const),
            pl.BlockSpec((FFN_HIDDEN, D_MODEL), const),
            pl.BlockSpec((1, D_MODEL), const),
        ],
        out_specs=pl.BlockSpec((1, tm, D_MODEL), lambda b, t: (b, t, 0)),
        out_shape=jax.ShapeDtypeStruct((bsz, seq, D_MODEL), F32),
        compiler_params=_params("parallel", "parallel"),
        name="ffn",
    )(x, mod, ng, wgu, wd, fg)


def _gate_weights(w_in_l, b_in_l):
    wg = w_in_l[:, GATE_OFF:SGU_OFF]
    bg = b_in_l[GATE_OFF:SGU_OFF]
    order = [wg[:, 0:HEADS], wg[:, 2 * HEADS:3 * HEADS], wg[:, HEADS:2 * HEADS], wg[:, 3 * HEADS:4 * HEADS]]
    w = jnp.pad(jnp.concatenate(order, axis=1), ((0, 0), (0, LANES - 2 * UNITS)))
    b_order = [bg[0:HEADS], bg[2 * HEADS:3 * HEADS], bg[HEADS:2 * HEADS], bg[3 * HEADS:4 * HEADS]]
    b = jnp.pad(jnp.concatenate(b_order), (0, LANES - 2 * UNITS))[None, :]
    w_hi = w.astype(BF16)
    w_lo = (w - w_hi.astype(F32)).astype(BF16)
    return jnp.concatenate([w_hi, w_lo], axis=1), b


def kernel(x, c, ctx, c_ctx, w_mod, b_mod, norm1_g, w_in, b_in, mlstm_g, sgu_ln_g, sgu_ln_b, sgu_w, sgu_b,
           conv_w, conv_b, conv_ln_g, conv_ln_b, w_out, norm2_g, w_gu, w_down, final_g):
    bsz = x.shape[0]
    lc = ctx.shape[1]
    assert bsz + 1 <= MOD_ROWS
    c_all = jnp.zeros((MOD_ROWS, D_MODEL), F32).at[:bsz].set(c).at[bsz].set(c_ctx)
    mod_all = _modulation(c_all, w_mod, b_mod).reshape(DEPTH, MOD_ROWS, 6, D_MODEL)
    row_x = lambda b, t: (b, 0, 0)
    row_c = lambda b, t: (bsz, 0, 0)
    tm_x, tm_c = 512, lc

    xc = ctx
    for l in range(DEPTH):
        last = l == DEPTH - 1
        mod = mod_all[l]
        w_main = jnp.concatenate([w_in[l][:, :GATE_OFF], w_in[l][:, SGU_OFF:]], axis=1).astype(BF16)
        b_main = jnp.concatenate([b_in[l][:GATE_OFF], b_in[l][SGU_OFF:]])[None, :]
        w_gate, b_gate = _gate_weights(w_in[l], b_in[l])
        ng1, ng2 = norm1_g[l][None, :], norm2_g[l][None, :]
        mix_w = (mlstm_g[l][None, :], sgu_ln_g[l][None, :], sgu_ln_b[l][None, :], sgu_w[l].astype(BF16),
                 jnp.repeat(sgu_b[l].T, SGU_WIDTH // SGU_GROUPS, axis=1),
                 conv_w[l], conv_b[l][None, :], conv_ln_g[l][None, :], conv_ln_b[l][None, :],
                 w_out[l].astype(BF16))
        wgu, wd = w_gu[l].astype(BF16), w_down[l].astype(BF16)
        fg = final_g[None, :]

        k_x, qt_x, kt_x, vt_x, rest_x, rows_x, ecol_x = _inproj(
            x, mod, row_x, ng1, w_main, b_main, w_gate, b_gate, tm_x)
        ctx_proj = _inproj(xc.reshape(1, bsz * lc, D_MODEL), mod, row_c, ng1, w_main, b_main, w_gate, b_gate,
                           tm_x)
        k_c, qt_c, kt_c, vt_c, rest_c, rows_c, ecol_c = (
            a.reshape((bsz, a.shape[1] // bsz) + a.shape[2:]) for a in ctx_proj)
        hs = _mlstm((k_c, qt_c, kt_c, vt_c, rows_c, ecol_c), (k_x, qt_x, kt_x, vt_x, rows_x, ecol_x),
                    ctx_out=not last)
        x = _mix(x, mod, row_x, rest_x, hs[-1], *mix_w, tm=tm_x, grid_mode=True)
        x = _ffn(x, mod, row_x, ng2, wgu, wd, fg, tm_x, final=last)
        if not last:
            xc = _mix(xc, mod, row_c, rest_c, hs[0], *mix_w, tm=tm_c, grid_mode=False)
            xc = _ffn(xc.reshape(1, bsz * lc, D_MODEL), mod, row_c, ng2, wgu, wd, fg, tm_x,
                      final=False).reshape(bsz, lc, D_MODEL)
    return x
```

```python
import functools

import jax
import jax.numpy as jnp
from jax import lax
from jax.experimental import pallas as pl
from jax.experimental.pallas import tpu as pltpu

D_MODEL = 1024
DEPTH = 2
GRID_W = 64
EPS = 1e-6

HEADS = 4
HEAD_DIM = 128
MLSTM_WIDTH = HEADS * HEAD_DIM
CHUNK = 128
UNITS = 2 * HEADS
SGU_GROUPS = 4
SGU_WIDTH = 256
CONV_WIDTH = 256
CONV_K = 31
CONV_R = CONV_K // 2
FFN_HIDDEN = 2816
FFN_CHUNK = 256

GATE_OFF = 4 * MLSTM_WIDTH
N_GATES = 4 * HEADS
SGU_OFF = GATE_OFF + N_GATES
QKV_COLS = 3 * MLSTM_WIDTH
QK_COLS = 2 * MLSTM_WIDTH
REST_COLS = MLSTM_WIDTH + 2 * SGU_WIDTH + 2 * CONV_WIDTH
MAIN_COLS = QKV_COLS + REST_COLS
LANES = 128
SUBLANES = 8
BF16_ROWS = 16
MOD_ROWS = 24
N_ROWQ = 5
AUG = HEAD_DIM + BF16_ROWS

VMEM_LIMIT = 56 * 1024 * 1024

F32 = jnp.float32
BF16 = jnp.bfloat16
HIGHEST = lax.Precision.HIGHEST
NT = (((1,), (1,)), ((), ()))

GELU_C = 0.7978845608028654
GELU_A = 0.044715


def _sigmoid(t):
    return 0.5 * jnp.tanh(0.5 * t) + 0.5


def _silu(t):
    return t * _sigmoid(t)


def _gelu_tanh(t):
    inner = t * (GELU_C * GELU_A * (t * t) + GELU_C)
    return t * (0.5 * jnp.tanh(inner) + 0.5)


def _log_sigmoid(t):
    return jnp.minimum(t, 0.0) - jnp.log1p(jnp.exp(-jnp.abs(t)))


def _layernorm(t, g, b):
    mu = jnp.mean(t, axis=-1, keepdims=True)
    d = t - mu
    var = jnp.mean(d * d, axis=-1, keepdims=True)
    return d * lax.rsqrt(var + EPS) * g + b


def _rmsnorm(t, g):
    return t * lax.rsqrt(jnp.mean(t * t, axis=-1, keepdims=True) + EPS) * g


def _bf16_rows(row, n):
    tile = jnp.broadcast_to(row, (BF16_ROWS, row.shape[1])).astype(BF16)
    return jnp.concatenate([tile] * (n // BF16_ROWS), axis=0)


def _modulated_rmsnorm(t, g, scale, shift):
    return t * lax.rsqrt(jnp.mean(t * t, axis=-1, keepdims=True) + EPS) * (g * (1.0 + scale)) + shift


TIE_GROUP = 8


def _zero_of(v):
    u = pltpu.bitcast(v, jnp.uint32)
    u = lax.shift_right_logical(lax.shift_right_logical(u, jnp.uint32(16)), jnp.uint32(16))
    return pltpu.bitcast(u, F32)


def _tie_groups(val):
    tiles = [val[r:r + SUBLANES, c:c + LANES]
             for r in range(0, val.shape[0], SUBLANES) for c in range(0, val.shape[1], LANES)]
    return [tiles[i:i + TIE_GROUP] for i in range(0, len(tiles), TIE_GROUP)]


def _tied_lhs(xb, groups):
    n_rg = xb.shape[0] // BF16_ROWS
    assert len(groups) <= n_rg
    rows = [xb[i * BF16_ROWS:(i + 1) * BF16_ROWS] for i in range(n_rg)]
    for n, group in enumerate(groups):
        total = group[0]
        for v in group[1:]:
            total = total + v
        zb = jnp.concatenate([_zero_of(total)] * (BF16_ROWS // SUBLANES), axis=0).astype(BF16)
        rg = n * (n_rg // len(groups))
        rows[rg] = jnp.concatenate([rows[rg][:, 0:LANES] + zb, rows[rg][:, LANES:]], axis=1)
    return jnp.concatenate(rows, axis=0)


def _params(*sem):
    return pltpu.CompilerParams(dimension_semantics=sem, vmem_limit_bytes=VMEM_LIMIT)


def _mod_kernel(c_ref, w_ref, b_ref, o_ref):
    a = _silu(c_ref[...])
    o_ref[0] = jnp.dot(a, w_ref[0], precision=HIGHEST, preferred_element_type=F32) + b_ref[0]


def _modulation(c_all, w_mod, b_mod):
    tn = 1536
    n = w_mod.shape[-1]
    return pl.pallas_call(
        _mod_kernel,
        grid=(DEPTH, n // tn),
        in_specs=[
            pl.BlockSpec((MOD_ROWS, D_MODEL), lambda l, j: (0, 0)),
            pl.BlockSpec((1, D_MODEL, tn), lambda l, j: (l, 0, j)),
            pl.BlockSpec((1, 1, tn), lambda l, j: (l, 0, j)),
        ],
        out_specs=pl.BlockSpec((1, MOD_ROWS, tn), lambda l, j: (l, 0, j)),
        out_shape=jax.ShapeDtypeStruct((DEPTH, MOD_ROWS, n), F32),
        compiler_params=_params("parallel", "parallel"),
        name="modulation",
    )(c_all, w_mod, b_mod.reshape(DEPTH, 1, n))


def _lane_scan(x, op, lane, forward):
    n = x.shape[1]
    k = 1
    while k < n:
        if forward:
            x = jnp.where(lane >= k, op(x, pltpu.roll(x, k, axis=1)), x)
        else:
            x = jnp.where(lane < n - k, op(x, pltpu.roll(x, n - k, axis=1)), x)
        k *= 2
    return x


def _inproj_kernel(xn_ref, x0_ref, modn_ref, mod0_ref, ng_ref, w_ref, b_ref, wg_ref, bg_ref,
                   k_ref, qt_ref, kt_ref, vt_ref, rest_ref, rows_ref, ecol_ref, xh_ref, *, tm):
    s = pl.program_id(0)

    @pl.when(s == 0)
    def _():
        xh_ref[0] = _modulated_rmsnorm(x0_ref[0], ng_ref[...], mod0_ref[0, 1:2, :],
                                       mod0_ref[0, 0:1, :]).astype(BF16)

    xh = xh_ref[s % 2]
    nxt = _modulated_rmsnorm(xn_ref[0], ng_ref[...], modn_ref[0, 1:2, :], modn_ref[0, 0:1, :])
    ties = _tie_groups(nxt)

    g2 = jnp.dot(xh, wg_ref[...], preferred_element_type=F32)
    g = g2[:, 0:LANES] + g2[:, LANES:2 * LANES] + bg_ref[...]
    lane = lax.broadcasted_iota(jnp.int32, (UNITS, CHUNK), 1)
    fwd_row = lax.broadcasted_iota(jnp.int32, (UNITS, CHUNK), 0) < HEADS
    pad_rows = jnp.zeros((CHUNK - UNITS, CHUNK), F32)
    for ch in range(tm // CHUNK):
        sl = slice(ch * CHUNK, (ch + 1) * CHUNK)
        gt = g[sl].T
        it = gt[0:UNITS]
        lsf = _log_sigmoid(gt[UNITS:2 * UNITS])
        bt = jnp.where(fwd_row, _lane_scan(lsf, jnp.add, lane, True), _lane_scan(lsf, jnp.add, lane, False))
        et = it - bt
        cme = jnp.where(fwd_row, _lane_scan(et, jnp.maximum, lane, True),
                        _lane_scan(et, jnp.maximum, lane, False))
        b_last = jnp.where(fwd_row, jnp.broadcast_to(bt[:, CHUNK - 1:CHUNK], bt.shape),
                           jnp.broadcast_to(bt[:, 0:1], bt.shape))
        cme_last = jnp.broadcast_to(jnp.max(et, axis=1, keepdims=True), et.shape)
        rows_ref[0, ch] = jnp.concatenate([et, cme, bt, b_last, cme_last], axis=0)
        ecol_ref[0, sl, :] = jnp.concatenate([et, pad_rows], axis=0).T

    blk = MAIN_COLS // 3
    share = -(-len(ties) // 2)
    parts = [jnp.dot(_tied_lhs(xh, ties[(i - 1) * share:i * share]) if i else xh, w_ref[:, i * blk:(i + 1) * blk],
                     preferred_element_type=F32) + b_ref[:, i * blk:(i + 1) * blk] for i in range(3)]
    pq = parts[0][:, 0:MLSTM_WIDTH]
    pk = parts[0][:, MLSTM_WIDTH:QK_COLS] * (HEAD_DIM ** -0.5)
    pv = parts[1][:, 0:MLSTM_WIDTH]
    k_ref[0] = pk.astype(BF16)
    rest_ref[0, :, 0:MLSTM_WIDTH] = parts[1][:, MLSTM_WIDTH:blk]
    rest_ref[0, :, MLSTM_WIDTH:REST_COLS] = parts[2]
    xh_ref[(s + 1) % 2] = nxt.astype(BF16)
    for ch in range(tm // CHUNK):
        sl = slice(ch * CHUNK, (ch + 1) * CHUNK)
        for head in range(HEADS):
            hs = slice(head * HEAD_DIM, (head + 1) * HEAD_DIM)
            qt_ref[0, ch, hs, :] = pq[sl, hs].T.astype(BF16)
            kt_ref[0, ch, hs, :] = pk[sl, hs].T.astype(BF16)
            vt_ref[0, ch, hs, :] = pv[sl, hs].T.astype(BF16)


def _inproj(x, mod, shared_mod_row, ng, w_main, b_main, w_gate, b_gate, tm):
    bsz, seq, _ = x.shape
    nch = seq // CHUNK
    tch = tm // CHUNK
    nt = seq // tm
    n_tiles = bsz * nt
    nxt_tile = lambda s: jnp.minimum(s + 1, n_tiles - 1)
    mod_of = (lambda i: i // nt) if shared_mod_row is None else (lambda i: shared_mod_row)
    const = lambda s: (0, 0)
    cur3 = lambda s: (s // nt, s % nt, 0)
    cur4 = lambda s: (s // nt, s % nt, 0, 0)
    return pl.pallas_call(
        functools.partial(_inproj_kernel, tm=tm),
        grid=(n_tiles,),
        in_specs=[
            pl.BlockSpec((1, tm, D_MODEL), lambda s: (nxt_tile(s) // nt, nxt_tile(s) % nt, 0)),
            pl.BlockSpec((1, tm, D_MODEL), lambda s: (0, 0, 0)),
            pl.BlockSpec((1, 6, D_MODEL), lambda s: (mod_of(nxt_tile(s)), 0, 0)),
            pl.BlockSpec((1, 6, D_MODEL), lambda s: (mod_of(0), 0, 0)),
            pl.BlockSpec((1, D_MODEL), const),
            pl.BlockSpec((D_MODEL, MAIN_COLS), const),
            pl.BlockSpec((1, MAIN_COLS), const),
            pl.BlockSpec((D_MODEL, 2 * LANES), const),
            pl.BlockSpec((1, LANES), const),
        ],
        out_specs=[
            pl.BlockSpec((1, tm, MLSTM_WIDTH), cur3),
            pl.BlockSpec((1, tch, MLSTM_WIDTH, CHUNK), cur4),
            pl.BlockSpec((1, tch, MLSTM_WIDTH, CHUNK), cur4),
            pl.BlockSpec((1, tch, MLSTM_WIDTH, CHUNK), cur4),
            pl.BlockSpec((1, tm, REST_COLS), cur3),
            pl.BlockSpec((1, tch, N_ROWQ * UNITS, CHUNK), cur4),
            pl.BlockSpec((1, tm, LANES), cur3),
        ],
        out_shape=[
            jax.ShapeDtypeStruct((bsz, seq, MLSTM_WIDTH), BF16),
            jax.ShapeDtypeStruct((bsz, nch, MLSTM_WIDTH, CHUNK), BF16),
            jax.ShapeDtypeStruct((bsz, nch, MLSTM_WIDTH, CHUNK), BF16),
            jax.ShapeDtypeStruct((bsz, nch, MLSTM_WIDTH, CHUNK), BF16),
            jax.ShapeDtypeStruct((bsz, seq, REST_COLS), F32),
            jax.ShapeDtypeStruct((bsz, nch, N_ROWQ * UNITS, CHUNK), F32),
            jax.ShapeDtypeStruct((bsz, seq, LANES), F32),
        ],
        scratch_shapes=[pltpu.VMEM((2, tm, D_MODEL), BF16)],
        compiler_params=_params("arbitrary"),
        name="inproj",
    )(x, x, mod, mod, ng, w_main, b_main, w_gate, b_gate)


def _mlstm_kernel(kc_ref, qtc_ref, ktc_ref, vtc_ref, rowc_ref, ecc_ref,
                  kx_ref, qtx_ref, ktx_ref, vtx_ref, rowx_ref, ecx_ref, *rest, ctx_out, n_ctx, n_x):
    if ctx_out:
        hc_ref, hx_ref, ct_ref, m_ref, part_ref = rest
    else:
        hx_ref, ct_ref, m_ref, part_ref = rest
        hc_ref = None
    ct_ref[...] = jnp.zeros(ct_ref.shape, F32)
    m_ref[...] = jnp.zeros(m_ref.shape, F32)
    r = lax.broadcasted_iota(jnp.int32, (CHUNK, CHUNK), 0)
    c = lax.broadcasted_iota(jnp.int32, (CHUNK, CHUNK), 1)
    masks = (r <= c, r >= c)
    ones = jnp.ones((BF16_ROWS, CHUNK), BF16)
    zeros = jnp.zeros((CHUNK, CHUNK), BF16)

    def block_diag(a, b):
        return jnp.concatenate([jnp.concatenate([a, zeros], axis=1),
                                jnp.concatenate([zeros, b], axis=1)], axis=0)

    def pair(refs, j, hp, direction, out_ref, mode, slot):
        k_ref, qt_ref, kt_ref, vt_ref, row_ref, ec_ref = refs
        heads = (2 * hp, 2 * hp + 1)
        units = tuple(direction * HEADS + h for h in heads)
        sp = direction * (HEADS // 2) + hp
        r0 = j * CHUNK if isinstance(j, int) else pl.multiple_of(j * CHUNK, CHUNK)
        toks = pl.ds(r0, CHUNK)
        hsl = [slice(h * HEAD_DIM, (h + 1) * HEAD_DIM) for h in heads]

        def row(q, u):
            return row_ref[0, j, q * UNITS + u:q * UNITS + u + 1, :]

        e, cme, b, b_last, cme_last = ([row(q, u) for u in units] for q in range(N_ROWQ))
        m_old = [m_ref[u, 0:1, :] for u in units]
        ct_old = ct_ref[sp]
        vt_aug = [jnp.concatenate([vt_ref[0, j, s, :], ones], axis=0) for s in hsl]

        if mode is not None:
            qt = [qt_ref[0, j, s, :] for s in hsl]
            k2 = k_ref[0, toks, 2 * hp * HEAD_DIM:(2 * hp + 2) * HEAD_DIM]
            qkt = jnp.dot(k2, block_diag(qt[0], qt[1]), preferred_element_type=F32)

        g_last = [jnp.maximum(m_old[i], cme_last[i]) for i in range(2)]
        kst = [kt_ref[0, j, hsl[i], :] * _bf16_rows(jnp.exp(e[i] - g_last[i]), CHUNK) for i in range(2)]
        upd = lax.dot_general(jnp.concatenate(vt_aug, axis=1), block_diag(kst[0], kst[1]), NT,
                              preferred_element_type=F32)
        keep = jnp.concatenate([jnp.exp(m_old[i] - g_last[i]) for i in range(2)], axis=1)
        ct_ref[sp] = keep * ct_old + upd
        for i in range(2):
            m_ref[units[i]] = jnp.broadcast_to(b_last[i] + g_last[i], m_ref.shape[1:])

        yield
        if mode is None:
            return
        ct_b = ct_old.astype(BF16)
        for i in range(2):
            u, cols = units[i], hsl[i]
            g = jnp.maximum(m_old[i], cme[i])
            dl = jnp.where(masks[direction], ec_ref[0, toks, u:u + 1] - g, -jnp.inf)
            st = (qkt[:, i * CHUNK:(i + 1) * CHUNK] * jnp.exp(dl)).astype(BF16)
            qs = qt[i] * _bf16_rows(jnp.exp(m_old[i] - g), CHUNK)
            res = jnp.dot(jnp.concatenate([vt_aug[i], ct_b[:, i * HEAD_DIM:(i + 1) * HEAD_DIM]], axis=1),
                          jnp.concatenate([st, qs], axis=0), preferred_element_type=F32)
            den = jnp.maximum(jnp.abs(res[HEAD_DIM:HEAD_DIM + 1, :]), jnp.exp(-(b[i] + g)))
            ht = res[0:HEAD_DIM, :] * (1.0 / den)
            if mode == "first":
                part_ref[direction, slot, cols, :] = ht
            else:
                tot = part_ref[1 - direction, slot, cols, :] + ht
                hn = tot * lax.rsqrt(jnp.mean(tot * tot, axis=0, keepdims=True) + EPS)
                out_ref[0, toks, cols] = hn.T

    def step(refs, j, n, out_ref, mode):
        slot = None if mode is None else (j if mode == "first" else n - 1 - j)
        pairs = []
        for hp in range(HEADS // 2):
            pairs.append(pair(refs, j, hp, 0, out_ref, mode, slot))
            pairs.append(pair(refs, n - 1 - j, hp, 1, out_ref, mode, slot))
        for phase in range(2):
            for p in pairs:
                next(p, None)

    crefs = (kc_ref, qtc_ref, ktc_ref, vtc_ref, rowc_ref, ecc_ref)
    xrefs = (kx_ref, qtx_ref, ktx_ref, vtx_ref, rowx_ref, ecx_ref)
    for j in range(n_ctx):
        mode = None if not ctx_out else ("first" if j < n_ctx // 2 else "second")
        step(crefs, j, n_ctx, hc_ref, mode)

    def body(j, carry, mode):
        step(xrefs, j, n_x, hx_ref, mode)
        return carry

    lax.fori_loop(0, n_x // 2, functools.partial(body, mode="first"), 0, unroll=2)
    lax.fori_loop(n_x // 2, n_x, functools.partial(body, mode="second"), 0, unroll=2)


def _mlstm(ctx_in, x_in, ctx_out):
    bsz, lc, _ = ctx_in[0].shape
    lx = x_in[0].shape[1]
    n_ctx, n_x = lc // CHUNK, lx // CHUNK
    assert n_ctx % 2 == 0 and n_x % 4 == 0
    i3 = lambda b: (b, 0, 0)
    i4 = lambda b: (b, 0, 0, 0)

    def specs(seq, nch):
        return [
            pl.BlockSpec((1, seq, MLSTM_WIDTH), i3),
            pl.BlockSpec((1, nch, MLSTM_WIDTH, CHUNK), i4),
            pl.BlockSpec((1, nch, MLSTM_WIDTH, CHUNK), i4),
            pl.BlockSpec((1, nch, MLSTM_WIDTH, CHUNK), i4),
            pl.BlockSpec((1, nch, N_ROWQ * UNITS, CHUNK), i4),
            pl.BlockSpec((1, seq, LANES), i3),
        ]

    out_specs = [pl.BlockSpec((1, lx, MLSTM_WIDTH), i3)]
    out_shape = [jax.ShapeDtypeStruct((bsz, lx, MLSTM_WIDTH), F32)]
    if ctx_out:
        out_specs.insert(0, pl.BlockSpec((1, lc, MLSTM_WIDTH), i3))
        out_shape.insert(0, jax.ShapeDtypeStruct((bsz, lc, MLSTM_WIDTH), F32))
    return pl.pallas_call(
        functools.partial(_mlstm_kernel, ctx_out=ctx_out, n_ctx=n_ctx, n_x=n_x),
        grid=(bsz,),
        in_specs=specs(lc, n_ctx) + specs(lx, n_x),
        out_specs=out_specs,
        out_shape=out_shape,
        scratch_shapes=[
            pltpu.VMEM((UNITS // 2, AUG, 2 * HEAD_DIM), F32),
            pltpu.VMEM((UNITS, SUBLANES, LANES), F32),
            pltpu.VMEM((2, max(n_ctx, n_x) // 2, MLSTM_WIDTH, CHUNK), F32),
        ],
        compiler_params=_params("parallel"),
        name="mlstm",
    )(*ctx_in, *x_in)


def _shift_conv(y, w_ref, lo, width, pos):
    n = y.shape[0]
    acc = jnp.zeros(y.shape, F32)
    for j in range(CONV_K):
        d = j - CONV_R
        w = w_ref[j:j + 1, lo:lo + y.shape[1]]
        if d == 0:
            acc = acc + w * y
        else:
            sh = pltpu.roll(y, (-d) % n, axis=0)
            valid = (pos >= -d) if d < 0 else (pos < width - d)
            acc = acc + w * jnp.where(valid, sh, 0.0)
    return acc


def _mix_kernel(x_ref, mod_ref, os_ref, h_ref, z_ref, mg_ref, slg_ref, slb_ref, sw_ref, sb_ref,
                cw_ref, cb_ref, clg_ref, clb_ref, wo_ref, out_ref, *scratch, seq, tm, grid_mode):
    t = pl.program_id(1)
    r0 = pl.multiple_of(t * tm, tm)
    rows = pl.ds(r0, tm)
    half = CONV_WIDTH // 2
    pad = CONV_R * GRID_W

    if grid_mode:
        (ycol_ref,) = scratch

        @pl.when(t == 0)
        def _():
            ycol_ref[0:pad, :] = jnp.zeros((pad, half), F32)
            ycol_ref[pad + seq:pad + seq + pad, :] = jnp.zeros((pad, half), F32)
            ycol_ref[pad:pad + seq, :] = z_ref[0, :, half:CONV_WIDTH] * _sigmoid(
                z_ref[0, :, CONV_WIDTH + half:2 * CONV_WIDTH])

        pos = lax.broadcasted_iota(jnp.int32, (tm, half), 0) & (GRID_W - 1)
        y_row = z_ref[0, rows, 0:half] * _sigmoid(z_ref[0, rows, CONV_WIDTH:CONV_WIDTH + half])
        conv_r = _shift_conv(y_row, cw_ref, 0, GRID_W, pos)
        conv_c = jnp.zeros((tm, half), F32)
        for j in range(CONV_K):
            start = pl.multiple_of(r0 + GRID_W * j, GRID_W)
            conv_c = conv_c + cw_ref[j:j + 1, half:CONV_WIDTH] * ycol_ref[pl.ds(start, tm), :]
        conv = jnp.concatenate([conv_r, conv_c], axis=1)
    else:
        pos = lax.broadcasted_iota(jnp.int32, (tm, CONV_WIDTH), 0)
        y = z_ref[0, :, 0:CONV_WIDTH] * _sigmoid(z_ref[0, :, CONV_WIDTH:2 * CONV_WIDTH])
        conv = _shift_conv(y, cw_ref, 0, tm, pos)
    y_c = _silu(_layernorm(conv + cb_ref[...], clg_ref[...], clb_ref[...]))

    zs = _gelu_tanh(os_ref[0, :, MLSTM_WIDTH:MLSTM_WIDTH + 2 * SGU_WIDTH])
    u = zs[:, 0:SGU_WIDTH]
    vn = _layernorm(zs[:, SGU_WIDTH:2 * SGU_WIDTH], slg_ref[...], slb_ref[...]).astype(BF16)
    lane_group = lax.broadcasted_iota(jnp.int32, (CHUNK, SGU_WIDTH), 1) // (SGU_WIDTH // SGU_GROUPS)
    mixed = []
    for ch in range(tm // CHUNK):
        vc = vn[ch * CHUNK:(ch + 1) * CHUNK]
        m = sb_ref[...]
        for g in range(SGU_GROUPS):
            m = m + jnp.where(lane_group == g, jnp.dot(sw_ref[g], vc, preferred_element_type=F32), 0.0)
        mixed.append(m)
    y_b = u * jnp.concatenate(mixed, axis=0)

    y_a = h_ref[0] * mg_ref[...] * _sigmoid(os_ref[0, :, 0:MLSTM_WIDTH])

    o_a, o_b = MLSTM_WIDTH, MLSTM_WIDTH + SGU_WIDTH
    proj = jnp.dot(y_a.astype(BF16), wo_ref[0:o_a, :], preferred_element_type=F32)
    proj = proj + jnp.dot(y_b.astype(BF16), wo_ref[o_a:o_b, :], preferred_element_type=F32)
    proj = proj + jnp.dot(y_c.astype(BF16), wo_ref[o_b:, :], preferred_element_type=F32)
    out_ref[0] = x_ref[0] + mod_ref[0, 2:3, :] * proj


def _mix(x, mod, mod_row, rest, h, mg, slg, slb, sw, sbias, cw, cb, clg, clb, wo, tm, grid_mode):
    bsz, seq, _ = x.shape
    const2 = lambda b, t: (0, 0)
    const3 = lambda b, t: (0, 0, 0)
    scratch = []
    if grid_mode:
        scratch.append(pltpu.VMEM((seq + 2 * CONV_R * GRID_W, CONV_WIDTH // 2), F32))
    return pl.pallas_call(
        functools.partial(_mix_kernel, seq=seq, tm=tm, grid_mode=grid_mode),
        grid=(bsz, seq // tm),
        in_specs=[
            pl.BlockSpec((1, tm, D_MODEL), lambda b, t: (b, t, 0)),
            pl.BlockSpec((1, 6, D_MODEL), mod_row),
            pl.BlockSpec((1, tm, MLSTM_WIDTH + 2 * SGU_WIDTH), lambda b, t: (b, t, 0)),
            pl.BlockSpec((1, tm, MLSTM_WIDTH), lambda b, t: (b, t, 0)),
            pl.BlockSpec((1, seq, 2 * CONV_WIDTH), lambda b, t: (b, 0, 2)),
            pl.BlockSpec((1, MLSTM_WIDTH), const2),
            pl.BlockSpec((1, SGU_WIDTH), const2),
            pl.BlockSpec((1, SGU_WIDTH), const2),
            pl.BlockSpec((SGU_GROUPS, CHUNK, CHUNK), const3),
            pl.BlockSpec((CHUNK, SGU_WIDTH), const2),
            pl.BlockSpec((CONV_K, CONV_WIDTH), const2),
            pl.BlockSpec((1, CONV_WIDTH), const2),
            pl.BlockSpec((1, CONV_WIDTH), const2),
            pl.BlockSpec((1, CONV_WIDTH), const2),
            pl.BlockSpec((D_MODEL, D_MODEL), const2),
        ],
        out_specs=pl.BlockSpec((1, tm, D_MODEL), lambda b, t: (b, t, 0)),
        out_shape=jax.ShapeDtypeStruct((bsz, seq, D_MODEL), F32),
        scratch_shapes=scratch,
        compiler_params=_params("parallel", "arbitrary"),
        name="mix",
    )(x, mod, rest, h, rest, mg, slg, slb, sw, sbias, cw, cb, clg, clb, wo)


def _ffn_kernel(xn_ref, xp_ref, modn_ref, modp_ref, ng_ref, wgu_ref, wd_ref, fg_ref, out_ref, xm_ref, *, final):
    s = pl.program_id(0)
    tm = xn_ref.shape[1]

    @pl.when(s == 0)
    def _():
        xm_ref[0] = _modulated_rmsnorm(xp_ref[0], ng_ref[...], modp_ref[0, 4:5, :],
                                       modp_ref[0, 3:4, :]).astype(BF16)

    xm = xm_ref[s % 2]
    nxt = _modulated_rmsnorm(xn_ref[0], ng_ref[...], modn_ref[0, 4:5, :], modn_ref[0, 3:4, :])
    ties = _tie_groups(nxt)
    n_chunks = FFN_HIDDEN // FFN_CHUNK
    per_chunk = -(-len(ties) // (n_chunks - 1))
    acc = jnp.zeros((tm, D_MODEL), F32)
    for ch in range(n_chunks):
        lo = ch * FFN_CHUNK
        xk = _tied_lhs(xm, ties[(ch - 1) * per_chunk:ch * per_chunk]) if ch else xm
        gate = jnp.dot(xk, wgu_ref[:, lo:lo + FFN_CHUNK], preferred_element_type=F32)
        up = jnp.dot(xk, wgu_ref[:, FFN_HIDDEN + lo:FFN_HIDDEN + lo + FFN_CHUNK], preferred_element_type=F32)
        act = (_silu(gate) * up).astype(BF16)
        acc = acc + jnp.dot(act, wd_ref[lo:lo + FFN_CHUNK, :], preferred_element_type=F32)
    y = xp_ref[0] + modp_ref[0, 5:6, :] * acc
    if final:
        y = _rmsnorm(y, fg_ref[...])
    out_ref[0] = y
    xm_ref[(s + 1) % 2] = nxt.astype(BF16)


def _ffn(x, mod, shared_mod_row, ng, wgu, wd, fg, tm, final):
    bsz, seq, _ = x.shape
    nt = seq // tm
    n_tiles = bsz * nt
    nxt_tile = lambda s: jnp.minimum(s + 1, n_tiles - 1)
    cur_tile = lambda s: s
    mod_of = (lambda i: i // nt) if shared_mod_row is None else (lambda i: shared_mod_row)
    tile3 = lambda f: (lambda s: (f(s) // nt, f(s) % nt, 0))
    const = lambda s: (0, 0)
    return pl.pallas_call(
        functools.partial(_ffn_kernel, final=final),
        grid=(n_tiles,),
        in_specs=[
            pl.BlockSpec((1, tm, D_MODEL), tile3(nxt_tile)),
            pl.BlockSpec((1, tm, D_MODEL), tile3(cur_tile)),
            pl.BlockSpec((1, 6, D_MODEL), lambda s: (mod_of(nxt_tile(s)), 0, 0)),
            pl.BlockSpec((1, 6, D_MODEL), lambda s: (mod_of(cur_tile(s)), 0, 0)),
            pl.BlockSpec((1, D_MODEL), const),
            pl.BlockSpec((D_MODEL, 2 * FFN_HIDDEN), const),
            pl.BlockSpec((FFN_HIDDEN, D_MODEL), const),
            pl.BlockSpec((1, D_MODEL), const),
        ],
        out_specs=pl.BlockSpec((1, tm, D_MODEL), tile3(cur_tile)),
        out_shape=jax.ShapeDtypeStruct((bsz, seq, D_MODEL), F32),
        scratch_shapes=[pltpu.VMEM((2, tm, D_MODEL), BF16)],
        compiler_params=_params("arbitrary"),
        name="ffn",
    )(x, x, mod, mod, ng, wgu, wd, fg)


def _gate_weights(w_in_l, b_in_l):
    wg = w_in_l[:, GATE_OFF:SGU_OFF]
    bg = b_in_l[GATE_OFF:SGU_OFF]
    order = [wg[:, 0:HEADS], wg[:, 2 * HEADS:3 * HEADS], wg[:, HEADS:2 * HEADS], wg[:, 3 * HEADS:4 * HEADS]]
    w = jnp.pad(jnp.concatenate(order, axis=1), ((0, 0), (0, LANES - 2 * UNITS)))
    b_order = [bg[0:HEADS], bg[2 * HEADS:3 * HEADS], bg[HEADS:2 * HEADS], bg[3 * HEADS:4 * HEADS]]
    b = jnp.pad(jnp.concatenate(b_order), (0, LANES - 2 * UNITS))[None, :]
    w_hi = w.astype(BF16)
    w_lo = (w - w_hi.astype(F32)).astype(BF16)
    return jnp.concatenate([w_hi, w_lo], axis=1), b


def kernel(x, c, ctx, c_ctx, w_mod, b_mod, norm1_g, w_in, b_in, mlstm_g, sgu_ln_g, sgu_ln_b, sgu_w, sgu_b,
           conv_w, conv_b, conv_ln_g, conv_ln_b, w_out, norm2_g, w_gu, w_down, final_g):
    bsz = x.shape[0]
    lc = ctx.shape[1]
    assert bsz + 1 <= MOD_ROWS
    c_all = jnp.zeros((MOD_ROWS, D_MODEL), F32).at[:bsz].set(c).at[bsz].set(c_ctx)
    mod_all = _modulation(c_all, w_mod, b_mod).reshape(DEPTH, MOD_ROWS, 6, D_MODEL)
    row_x = lambda b, t: (b, 0, 0)
    row_c = lambda b, t: (bsz, 0, 0)
    tm_x, tm_c = 512, lc

    xc = ctx
    for l in range(DEPTH):
        last = l == DEPTH - 1
        mod = mod_all[l]
        w_main = jnp.concatenate([w_in[l][:, :GATE_OFF], w_in[l][:, SGU_OFF:]], axis=1).astype(BF16)
        b_main = jnp.concatenate([b_in[l][:GATE_OFF], b_in[l][SGU_OFF:]])[None, :]
        w_gate, b_gate = _gate_weights(w_in[l], b_in[l])
        ng1, ng2 = norm1_g[l][None, :], norm2_g[l][None, :]
        mix_w = (mlstm_g[l][None, :], sgu_ln_g[l][None, :], sgu_ln_b[l][None, :], sgu_w[l].astype(BF16),
                 jnp.repeat(sgu_b[l].T, SGU_WIDTH // SGU_GROUPS, axis=1),
                 conv_w[l], conv_b[l][None, :], conv_ln_g[l][None, :], conv_ln_b[l][None, :],
                 w_out[l].astype(BF16))
        wgu, wd = w_gu[l].astype(BF16), w_down[l].astype(BF16)
        fg = final_g[None, :]

        k_x, qt_x, kt_x, vt_x, rest_x, rows_x, ecol_x = _inproj(
            x, mod, None, ng1, w_main, b_main, w_gate, b_gate, tm_x)
        ctx_proj = _inproj(xc.reshape(1, bsz * lc, D_MODEL), mod, bsz, ng1, w_main, b_main, w_gate, b_gate,
                           tm_x)
        k_c, qt_c, kt_c, vt_c, rest_c, rows_c, ecol_c = (
            a.reshape((bsz, a.shape[1] // bsz) + a.shape[2:]) for a in ctx_proj)
        hs = _mlstm((k_c, qt_c, kt_c, vt_c, rows_c, ecol_c), (k_x, qt_x, kt_x, vt_x, rows_x, ecol_x),
                    ctx_out=not last)
        x = _mix(x, mod, row_x, rest_x, hs[-1], *mix_w, tm=tm_x, grid_mode=True)
        x = _ffn(x, mod, None, ng2, wgu, wd, fg, tm_x, final=last)
        if not last:
            xc = _mix(xc, mod, row_c, rest_c, hs[0], *mix_w, tm=tm_c, grid_mode=False)
            xc = _ffn(xc.reshape(1, bsz * lc, D_MODEL), mod, bsz, ng2, wgu, wd, fg, tm_x,
                      final=False).reshape(bsz, lc, D_MODEL)
    return x
```

```python
import functools

import jax
import jax.numpy as jnp
from jax import lax
from jax.experimental import pallas as pl
from jax.experimental.pallas import tpu as pltpu

D_MODEL = 1024
DEPTH = 2
GRID_W = 64
EPS = 1e-6

HEADS = 4
HEAD_DIM = 128
MLSTM_WIDTH = HEADS * HEAD_DIM
CHUNK = 128
UNITS = 2 * HEADS
SGU_GROUPS = 4
SGU_WIDTH = 256
CONV_WIDTH = 256
CONV_K = 31
CONV_R = CONV_K // 2
FFN_HIDDEN = 2816
FFN_CHUNK = 256

GATE_OFF = 4 * MLSTM_WIDTH
N_GATES = 4 * HEADS
SGU_OFF = GATE_OFF + N_GATES
QKV_COLS = 3 * MLSTM_WIDTH
QK_COLS = 2 * MLSTM_WIDTH
REST_COLS = MLSTM_WIDTH + 2 * SGU_WIDTH + 2 * CONV_WIDTH
MAIN_COLS = QKV_COLS + REST_COLS
LANES = 128
SUBLANES = 8
BF16_ROWS = 16
MOD_ROWS = 24
N_ROWQ = 5
AUG = HEAD_DIM + BF16_ROWS

VMEM_LIMIT = 56 * 1024 * 1024

F32 = jnp.float32
BF16 = jnp.bfloat16
HIGHEST = lax.Precision.HIGHEST
NT = (((1,), (1,)), ((), ()))

GELU_C = 0.7978845608028654
GELU_A = 0.044715


def _sigmoid(t):
    return 0.5 * jnp.tanh(0.5 * t) + 0.5


def _silu(t):
    return t * _sigmoid(t)


def _gelu_tanh(t):
    inner = t * (GELU_C * GELU_A * (t * t) + GELU_C)
    return t * (0.5 * jnp.tanh(inner) + 0.5)


def _log_sigmoid(t):
    return jnp.minimum(t, 0.0) - jnp.log1p(jnp.exp(-jnp.abs(t)))


def _layernorm(t, g, b):
    mu = jnp.mean(t, axis=-1, keepdims=True)
    d = t - mu
    var = jnp.mean(d * d, axis=-1, keepdims=True)
    return d * lax.rsqrt(var + EPS) * g + b


def _rmsnorm(t, g):
    return t * lax.rsqrt(jnp.mean(t * t, axis=-1, keepdims=True) + EPS) * g


def _modulated_rmsnorm(t, g, scale, shift):
    return t * lax.rsqrt(jnp.mean(t * t, axis=-1, keepdims=True) + EPS) * (g * (1.0 + scale)) + shift


def _bf16_rows(row, n):
    tile = jnp.broadcast_to(row, (BF16_ROWS, row.shape[1])).astype(BF16)
    return jnp.concatenate([tile] * (n // BF16_ROWS), axis=0)


def _params(*sem):
    return pltpu.CompilerParams(dimension_semantics=sem, vmem_limit_bytes=VMEM_LIMIT)


def _mod_kernel(c_ref, w_ref, b_ref, o_ref):
    a = _silu(c_ref[...])
    o_ref[0] = jnp.dot(a, w_ref[0], precision=HIGHEST, preferred_element_type=F32) + b_ref[0]


def _modulation(c_all, w_mod, b_mod):
    tn = 1536
    n = w_mod.shape[-1]
    return pl.pallas_call(
        _mod_kernel,
        grid=(DEPTH, n // tn),
        in_specs=[
            pl.BlockSpec((MOD_ROWS, D_MODEL), lambda l, j: (0, 0)),
            pl.BlockSpec((1, D_MODEL, tn), lambda l, j: (l, 0, j)),
            pl.BlockSpec((1, 1, tn), lambda l, j: (l, 0, j)),
        ],
        out_specs=pl.BlockSpec((1, MOD_ROWS, tn), lambda l, j: (l, 0, j)),
        out_shape=jax.ShapeDtypeStruct((DEPTH, MOD_ROWS, n), F32),
        compiler_params=_params("parallel", "parallel"),
        name="modulation",
    )(c_all, w_mod, b_mod.reshape(DEPTH, 1, n))


def _lane_scan(x, op, lane, forward):
    n = x.shape[1]
    k = 1
    while k < n:
        if forward:
            x = jnp.where(lane >= k, op(x, pltpu.roll(x, k, axis=1)), x)
        else:
            x = jnp.where(lane < n - k, op(x, pltpu.roll(x, n - k, axis=1)), x)
        k *= 2
    return x


def _inproj_kernel(x_ref, mod_ref, ng_ref, w_ref, b_ref, wg_ref, bg_ref,
                   k_ref, qt_ref, kt_ref, vt_ref, rest_ref, rows_ref, ecol_ref, *, tm):
    x = x_ref[0]
    xh = _modulated_rmsnorm(x, ng_ref[...], mod_ref[0, 1:2, :], mod_ref[0, 0:1, :]).astype(BF16)

    g2 = jnp.dot(xh, wg_ref[...], preferred_element_type=F32)
    g = g2[:, 0:LANES] + g2[:, LANES:2 * LANES] + bg_ref[...]
    lane = lax.broadcasted_iota(jnp.int32, (UNITS, CHUNK), 1)
    fwd_row = lax.broadcasted_iota(jnp.int32, (UNITS, CHUNK), 0) < HEADS
    pad_rows = jnp.zeros((CHUNK - UNITS, CHUNK), F32)
    for ch in range(tm // CHUNK):
        sl = slice(ch * CHUNK, (ch + 1) * CHUNK)
        gt = g[sl].T
        it = gt[0:UNITS]
        lsf = _log_sigmoid(gt[UNITS:2 * UNITS])
        bt = jnp.where(fwd_row, _lane_scan(lsf, jnp.add, lane, True), _lane_scan(lsf, jnp.add, lane, False))
        et = it - bt
        cme = jnp.where(fwd_row, _lane_scan(et, jnp.maximum, lane, True),
                        _lane_scan(et, jnp.maximum, lane, False))
        b_last = jnp.where(fwd_row, jnp.broadcast_to(bt[:, CHUNK - 1:CHUNK], bt.shape),
                           jnp.broadcast_to(bt[:, 0:1], bt.shape))
        cme_last = jnp.broadcast_to(jnp.max(et, axis=1, keepdims=True), et.shape)
        rows_ref[0, ch] = jnp.concatenate([et, cme, bt, b_last, cme_last], axis=0)
        ecol_ref[0, sl, :] = jnp.concatenate([et, pad_rows], axis=0).T

    p = jnp.dot(xh, w_ref[...], preferred_element_type=F32) + b_ref[...]
    pq = p[:, 0:MLSTM_WIDTH]
    pk = p[:, MLSTM_WIDTH:QK_COLS] * (HEAD_DIM ** -0.5)
    pv = p[:, QK_COLS:QKV_COLS]
    k_ref[0] = pk.astype(BF16)
    rest_ref[0] = p[:, QKV_COLS:MAIN_COLS]
    for ch in range(tm // CHUNK):
        sl = slice(ch * CHUNK, (ch + 1) * CHUNK)
        for head in range(HEADS):
            hs = slice(head * HEAD_DIM, (head + 1) * HEAD_DIM)
            qt_ref[0, ch, hs, :] = pq[sl, hs].T.astype(BF16)
            kt_ref[0, ch, hs, :] = pk[sl, hs].T.astype(BF16)
            vt_ref[0, ch, hs, :] = pv[sl, hs].T.astype(BF16)


def _inproj(x, mod, mod_row, ng, w_main, b_main, w_gate, b_gate, tm):
    bsz, seq, _ = x.shape
    nch = seq // CHUNK
    tch = tm // CHUNK
    const = lambda b, t: (0, 0)
    return pl.pallas_call(
        functools.partial(_inproj_kernel, tm=tm),
        grid=(bsz, seq // tm),
        in_specs=[
            pl.BlockSpec((1, tm, D_MODEL), lambda b, t: (b, t, 0)),
            pl.BlockSpec((1, 6, D_MODEL), mod_row),
            pl.BlockSpec((1, D_MODEL), const),
            pl.BlockSpec((D_MODEL, MAIN_COLS), const),
            pl.BlockSpec((1, MAIN_COLS), const),
            pl.BlockSpec((D_MODEL, 2 * LANES), const),
            pl.BlockSpec((1, LANES), const),
        ],
        out_specs=[
            pl.BlockSpec((1, tm, MLSTM_WIDTH), lambda b, t: (b, t, 0)),
            pl.BlockSpec((1, tch, MLSTM_WIDTH, CHUNK), lambda b, t: (b, t, 0, 0)),
            pl.BlockSpec((1, tch, MLSTM_WIDTH, CHUNK), lambda b, t: (b, t, 0, 0)),
            pl.BlockSpec((1, tch, MLSTM_WIDTH, CHUNK), lambda b, t: (b, t, 0, 0)),
            pl.BlockSpec((1, tm, REST_COLS), lambda b, t: (b, t, 0)),
            pl.BlockSpec((1, tch, N_ROWQ * UNITS, CHUNK), lambda b, t: (b, t, 0, 0)),
            pl.BlockSpec((1, tm, LANES), lambda b, t: (b, t, 0)),
        ],
        out_shape=[
            jax.ShapeDtypeStruct((bsz, seq, MLSTM_WIDTH), BF16),
            jax.ShapeDtypeStruct((bsz, nch, MLSTM_WIDTH, CHUNK), BF16),
            jax.ShapeDtypeStruct((bsz, nch, MLSTM_WIDTH, CHUNK), BF16),
            jax.ShapeDtypeStruct((bsz, nch, MLSTM_WIDTH, CHUNK), BF16),
            jax.ShapeDtypeStruct((bsz, seq, REST_COLS), F32),
            jax.ShapeDtypeStruct((bsz, nch, N_ROWQ * UNITS, CHUNK), F32),
            jax.ShapeDtypeStruct((bsz, seq, LANES), F32),
        ],
        compiler_params=_params("parallel", "parallel"),
        name="inproj",
    )(x, mod, ng, w_main, b_main, w_gate, b_gate)


def _mlstm_kernel(kc_ref, qtc_ref, ktc_ref, vtc_ref, rowc_ref, ecc_ref,
                  kx_ref, qtx_ref, ktx_ref, vtx_ref, rowx_ref, ecx_ref, *rest, ctx_out, n_ctx, n_x):
    if ctx_out:
        hc_ref, hx_ref, ct_ref, m_ref, part_ref = rest
    else:
        hx_ref, ct_ref, m_ref, part_ref = rest
        hc_ref = None
    ct_ref[...] = jnp.zeros(ct_ref.shape, F32)
    m_ref[...] = jnp.zeros(m_ref.shape, F32)
    r = lax.broadcasted_iota(jnp.int32, (CHUNK, CHUNK), 0)
    c = lax.broadcasted_iota(jnp.int32, (CHUNK, CHUNK), 1)
    masks = (r <= c, r >= c)
    ones = jnp.ones((BF16_ROWS, CHUNK), BF16)
    zeros = jnp.zeros((CHUNK, CHUNK), BF16)

    def block_diag(a, b):
        return jnp.concatenate([jnp.concatenate([a, zeros], axis=1),
                                jnp.concatenate([zeros, b], axis=1)], axis=0)

    def pair(refs, j, hp, direction, out_ref, mode, slot):
        k_ref, qt_ref, kt_ref, vt_ref, row_ref, ec_ref = refs
        heads = (2 * hp, 2 * hp + 1)
        units = tuple(direction * HEADS + h for h in heads)
        sp = direction * (HEADS // 2) + hp
        toks = slice(j * CHUNK, (j + 1) * CHUNK)
        hsl = [slice(h * HEAD_DIM, (h + 1) * HEAD_DIM) for h in heads]

        def row(q, u):
            return row_ref[0, j, q * UNITS + u:q * UNITS + u + 1, :]

        e, cme, b, b_last, cme_last = ([row(q, u) for u in units] for q in range(N_ROWQ))
        m_old = [m_ref[u, 0:1, :] for u in units]
        ct_old = ct_ref[sp]
        vt_aug = [jnp.concatenate([vt_ref[0, j, s, :], ones], axis=0) for s in hsl]

        if mode is not None:
            qt = [qt_ref[0, j, s, :] for s in hsl]
            k2 = k_ref[0, toks, 2 * hp * HEAD_DIM:(2 * hp + 2) * HEAD_DIM]
            qkt = jnp.dot(k2, block_diag(qt[0], qt[1]), preferred_element_type=F32)

        g_last = [jnp.maximum(m_old[i], cme_last[i]) for i in range(2)]
        kst = [kt_ref[0, j, hsl[i], :] * _bf16_rows(jnp.exp(e[i] - g_last[i]), CHUNK) for i in range(2)]
        upd = lax.dot_general(jnp.concatenate(vt_aug, axis=1), block_diag(kst[0], kst[1]), NT,
                              preferred_element_type=F32)
        keep = jnp.concatenate([jnp.exp(m_old[i] - g_last[i]) for i in range(2)], axis=1)
        ct_ref[sp] = keep * ct_old + upd
        for i in range(2):
            m_ref[units[i]] = jnp.broadcast_to(b_last[i] + g_last[i], m_ref.shape[1:])

        yield
        if mode is None:
            return
        ct_b = ct_old.astype(BF16)
        for i in range(2):
            u, cols = units[i], hsl[i]
            g = jnp.maximum(m_old[i], cme[i])
            dl = jnp.where(masks[direction], ec_ref[0, toks, u:u + 1] - g, -jnp.inf)
            st = (qkt[:, i * CHUNK:(i + 1) * CHUNK] * jnp.exp(dl)).astype(BF16)
            qs = qt[i] * _bf16_rows(jnp.exp(m_old[i] - g), CHUNK)
            res = jnp.dot(jnp.concatenate([vt_aug[i], ct_b[:, i * HEAD_DIM:(i + 1) * HEAD_DIM]], axis=1),
                          jnp.concatenate([st, qs], axis=0), preferred_element_type=F32)
            den = jnp.maximum(jnp.abs(res[HEAD_DIM:HEAD_DIM + 1, :]), jnp.exp(-(b[i] + g)))
            ht = res[0:HEAD_DIM, :] * (1.0 / den)
            if mode == "first":
                part_ref[direction, slot, cols, :] = ht
            else:
                tot = part_ref[1 - direction, slot, cols, :] + ht
                hn = tot * lax.rsqrt(jnp.mean(tot * tot, axis=0, keepdims=True) + EPS)
                out_ref[0, toks, cols] = hn.T

    def step(refs, j, n, out_ref, mode):
        slot = None if mode is None else (j if mode == "first" else n - 1 - j)
        pairs = []
        for hp in range(HEADS // 2):
            pairs.append(pair(refs, j, hp, 0, out_ref, mode, slot))
            pairs.append(pair(refs, n - 1 - j, hp, 1, out_ref, mode, slot))
        for phase in range(2):
            for p in pairs:
                next(p, None)

    crefs = (kc_ref, qtc_ref, ktc_ref, vtc_ref, rowc_ref, ecc_ref)
    xrefs = (kx_ref, qtx_ref, ktx_ref, vtx_ref, rowx_ref, ecx_ref)
    for j in range(n_ctx):
        mode = None if not ctx_out else ("first" if j < n_ctx // 2 else "second")
        step(crefs, j, n_ctx, hc_ref, mode)
    for j in range(n_x):
        step(xrefs, j, n_x, hx_ref, "first" if j < n_x // 2 else "second")


def _mlstm(ctx_in, x_in, ctx_out):
    bsz, lc, _ = ctx_in[0].shape
    lx = x_in[0].shape[1]
    n_ctx, n_x = lc // CHUNK, lx // CHUNK
    assert n_ctx % 2 == 0 and n_x % 2 == 0
    i3 = lambda b: (b, 0, 0)
    i4 = lambda b: (b, 0, 0, 0)

    def specs(seq, nch):
        return [
            pl.BlockSpec((1, seq, MLSTM_WIDTH), i3),
            pl.BlockSpec((1, nch, MLSTM_WIDTH, CHUNK), i4),
            pl.BlockSpec((1, nch, MLSTM_WIDTH, CHUNK), i4),
            pl.BlockSpec((1, nch, MLSTM_WIDTH, CHUNK), i4),
            pl.BlockSpec((1, nch, N_ROWQ * UNITS, CHUNK), i4),
            pl.BlockSpec((1, seq, LANES), i3),
        ]

    out_specs = [pl.BlockSpec((1, lx, MLSTM_WIDTH), i3)]
    out_shape = [jax.ShapeDtypeStruct((bsz, lx, MLSTM_WIDTH), F32)]
    if ctx_out:
        out_specs.insert(0, pl.BlockSpec((1, lc, MLSTM_WIDTH), i3))
        out_shape.insert(0, jax.ShapeDtypeStruct((bsz, lc, MLSTM_WIDTH), F32))
    return pl.pallas_call(
        functools.partial(_mlstm_kernel, ctx_out=ctx_out, n_ctx=n_ctx, n_x=n_x),
        grid=(bsz,),
        in_specs=specs(lc, n_ctx) + specs(lx, n_x),
        out_specs=out_specs,
        out_shape=out_shape,
        scratch_shapes=[
            pltpu.VMEM((UNITS // 2, AUG, 2 * HEAD_DIM), F32),
            pltpu.VMEM((UNITS, SUBLANES, LANES), F32),
            pltpu.VMEM((2, max(n_ctx, n_x) // 2, MLSTM_WIDTH, CHUNK), F32),
        ],
        compiler_params=_params("parallel"),
        name="mlstm",
    )(*ctx_in, *x_in)


def _shift_conv(y, w_ref, lo, width, pos):
    n = y.shape[0]
    acc = jnp.zeros(y.shape, F32)
    for j in range(CONV_K):
        d = j - CONV_R
        w = w_ref[j:j + 1, lo:lo + y.shape[1]]
        if d == 0:
            acc = acc + w * y
        else:
            sh = pltpu.roll(y, (-d) % n, axis=0)
            valid = (pos >= -d) if d < 0 else (pos < width - d)
            acc = acc + w * jnp.where(valid, sh, 0.0)
    return acc


def _mix_kernel(x_ref, mod_ref, os_ref, h_ref, z_ref, mg_ref, slg_ref, slb_ref, sw_ref, sb_ref,
                cw_ref, cb_ref, clg_ref, clb_ref, wo_ref, out_ref, *scratch, seq, tm, grid_mode):
    t = pl.program_id(1)
    r0 = pl.multiple_of(t * tm, tm)
    rows = pl.ds(r0, tm)
    half = CONV_WIDTH // 2
    pad = CONV_R * GRID_W

    if grid_mode:
        (ycol_ref,) = scratch

        @pl.when(t == 0)
        def _():
            ycol_ref[0:pad, :] = jnp.zeros((pad, half), F32)
            ycol_ref[pad + seq:pad + seq + pad, :] = jnp.zeros((pad, half), F32)
            ycol_ref[pad:pad + seq, :] = z_ref[0, :, half:CONV_WIDTH] * _sigmoid(
                z_ref[0, :, CONV_WIDTH + half:2 * CONV_WIDTH])

        pos = lax.broadcasted_iota(jnp.int32, (tm, half), 0) & (GRID_W - 1)
        y_row = z_ref[0, rows, 0:half] * _sigmoid(z_ref[0, rows, CONV_WIDTH:CONV_WIDTH + half])
        conv_r = _shift_conv(y_row, cw_ref, 0, GRID_W, pos)
        conv_c = jnp.zeros((tm, half), F32)
        for j in range(CONV_K):
            start = pl.multiple_of(r0 + GRID_W * j, GRID_W)
            conv_c = conv_c + cw_ref[j:j + 1, half:CONV_WIDTH] * ycol_ref[pl.ds(start, tm), :]
        conv = jnp.concatenate([conv_r, conv_c], axis=1)
    else:
        pos = lax.broadcasted_iota(jnp.int32, (tm, CONV_WIDTH), 0)
        y = z_ref[0, :, 0:CONV_WIDTH] * _sigmoid(z_ref[0, :, CONV_WIDTH:2 * CONV_WIDTH])
        conv = _shift_conv(y, cw_ref, 0, tm, pos)
    y_c = _silu(_layernorm(conv + cb_ref[...], clg_ref[...], clb_ref[...]))

    zs = _gelu_tanh(os_ref[0, :, MLSTM_WIDTH:MLSTM_WIDTH + 2 * SGU_WIDTH])
    u = zs[:, 0:SGU_WIDTH]
    vn = _layernorm(zs[:, SGU_WIDTH:2 * SGU_WIDTH], slg_ref[...], slb_ref[...]).astype(BF16)
    lane_group = lax.broadcasted_iota(jnp.int32, (CHUNK, SGU_WIDTH), 1) // (SGU_WIDTH // SGU_GROUPS)
    mixed = []
    for ch in range(tm // CHUNK):
        vc = vn[ch * CHUNK:(ch + 1) * CHUNK]
        m = sb_ref[...]
        for g in range(SGU_GROUPS):
            m = m + jnp.where(lane_group == g, jnp.dot(sw_ref[g], vc, preferred_element_type=F32), 0.0)
        mixed.append(m)
    y_b = u * jnp.concatenate(mixed, axis=0)

    y_a = h_ref[0] * mg_ref[...] * _sigmoid(os_ref[0, :, 0:MLSTM_WIDTH])

    o_a, o_b = MLSTM_WIDTH, MLSTM_WIDTH + SGU_WIDTH
    proj = jnp.dot(y_a.astype(BF16), wo_ref[0:o_a, :], preferred_element_type=F32)
    proj = proj + jnp.dot(y_b.astype(BF16), wo_ref[o_a:o_b, :], preferred_element_type=F32)
    proj = proj + jnp.dot(y_c.astype(BF16), wo_ref[o_b:, :], preferred_element_type=F32)
    out_ref[0] = x_ref[0] + mod_ref[0, 2:3, :] * proj


def _mix(x, mod, mod_row, rest, h, mg, slg, slb, sw, sbias, cw, cb, clg, clb, wo, tm, grid_mode):
    bsz, seq, _ = x.shape
    const2 = lambda b, t: (0, 0)
    const3 = lambda b, t: (0, 0, 0)
    scratch = []
    if grid_mode:
        scratch.append(pltpu.VMEM((seq + 2 * CONV_R * GRID_W, CONV_WIDTH // 2), F32))
    return pl.pallas_call(
        functools.partial(_mix_kernel, seq=seq, tm=tm, grid_mode=grid_mode),
        grid=(bsz, seq // tm),
        in_specs=[
            pl.BlockSpec((1, tm, D_MODEL), lambda b, t: (b, t, 0)),
            pl.BlockSpec((1, 6, D_MODEL), mod_row),
            pl.BlockSpec((1, tm, MLSTM_WIDTH + 2 * SGU_WIDTH), lambda b, t: (b, t, 0)),
            pl.BlockSpec((1, tm, MLSTM_WIDTH), lambda b, t: (b, t, 0)),
            pl.BlockSpec((1, seq, 2 * CONV_WIDTH), lambda b, t: (b, 0, 2)),
            pl.BlockSpec((1, MLSTM_WIDTH), const2),
            pl.BlockSpec((1, SGU_WIDTH), const2),
            pl.BlockSpec((1, SGU_WIDTH), const2),
            pl.BlockSpec((SGU_GROUPS, CHUNK, CHUNK), const3),
            pl.BlockSpec((CHUNK, SGU_WIDTH), const2),
            pl.BlockSpec((CONV_K, CONV_WIDTH), const2),
            pl.BlockSpec((1, CONV_WIDTH), const2),
            pl.BlockSpec((1, CONV_WIDTH), const2),
            pl.BlockSpec((1, CONV_WIDTH), const2),
            pl.BlockSpec((D_MODEL, D_MODEL), const2),
        ],
        out_specs=pl.BlockSpec((1, tm, D_MODEL), lambda b, t: (b, t, 0)),
        out_shape=jax.ShapeDtypeStruct((bsz, seq, D_MODEL), F32),
        scratch_shapes=scratch,
        compiler_params=_params("parallel", "arbitrary"),
        name="mix",
    )(x, mod, rest, h, rest, mg, slg, slb, sw, sbias, cw, cb, clg, clb, wo)


def _ffn_kernel(x_ref, mod_ref, ng_ref, wgu_ref, wd_ref, fg_ref, out_ref, *, final):
    x = x_ref[0]
    xm = _modulated_rmsnorm(x, ng_ref[...], mod_ref[0, 4:5, :], mod_ref[0, 3:4, :]).astype(BF16)
    acc = jnp.zeros(x.shape, F32)
    for ch in range(FFN_HIDDEN // FFN_CHUNK):
        lo = ch * FFN_CHUNK
        gate = jnp.dot(xm, wgu_ref[:, lo:lo + FFN_CHUNK], preferred_element_type=F32)
        up = jnp.dot(xm, wgu_ref[:, FFN_HIDDEN + lo:FFN_HIDDEN + lo + FFN_CHUNK], preferred_element_type=F32)
        act = (_silu(gate) * up).astype(BF16)
        acc = acc + jnp.dot(act, wd_ref[lo:lo + FFN_CHUNK, :], preferred_element_type=F32)
    y = x + mod_ref[0, 5:6, :] * acc
    if final:
        y = _rmsnorm(y, fg_ref[...])
    out_ref[0] = y


def _ffn(x, mod, mod_row, ng, wgu, wd, fg, tm, final):
    bsz, seq, _ = x.shape
    const = lambda b, t: (0, 0)
    return pl.pallas_call(
        functools.partial(_ffn_kernel, final=final),
        grid=(bsz, seq // tm),
        in_specs=[
            pl.BlockSpec((1, tm, D_MODEL), lambda b, t: (b, t, 0)),
            pl.BlockSpec((1, 6, D_MODEL), mod_row),
            pl.BlockSpec((1, D_MODEL), const),
            pl.BlockSpec((D_MODEL, 2 * FFN_HIDDEN), const),
            pl.BlockSpec((FFN_HIDDEN, D_MODEL), const),
            pl.BlockSpec((1, D_MODEL), const),
        ],
        out_specs=pl.BlockSpec((1, tm, D_MODEL), lambda b, t: (b, t, 0)),
        out_shape=jax.ShapeDtypeStruct((bsz, seq, D_MODEL), F32),
        compiler_params=_params("parallel", "parallel"),
        name="ffn",
    )(x, mod, ng, wgu, wd, fg)


def _gate_weights(w_in_l, b_in_l):
    wg = w_in_l[:, GATE_OFF:SGU_OFF]
    bg = b_in_l[GATE_OFF:SGU_OFF]
    order = [wg[:, 0:HEADS], wg[:, 2 * HEADS:3 * HEADS], wg[:, HEADS:2 * HEADS], wg[:, 3 * HEADS:4 * HEADS]]
    w = jnp.pad(jnp.concatenate(order, axis=1), ((0, 0), (0, LANES - 2 * UNITS)))
    b_order = [bg[0:HEADS], bg[2 * HEADS:3 * HEADS], bg[HEADS:2 * HEADS], bg[3 * HEADS:4 * HEADS]]
    b = jnp.pad(jnp.concatenate(b_order), (0, LANES - 2 * UNITS))[None, :]
    w_hi = w.astype(BF16)
    w_lo = (w - w_hi.astype(F32)).astype(BF16)
    return jnp.concatenate([w_hi, w_lo], axis=1), b


def kernel(x, c, ctx, c_ctx, w_mod, b_mod, norm1_g, w_in, b_in, mlstm_g, sgu_ln_g, sgu_ln_b, sgu_w, sgu_b,
           conv_w, conv_b, conv_ln_g, conv_ln_b, w_out, norm2_g, w_gu, w_down, final_g):
    bsz = x.shape[0]
    lc = ctx.shape[1]
    assert bsz + 1 <= MOD_ROWS
    c_all = jnp.zeros((MOD_ROWS, D_MODEL), F32).at[:bsz].set(c).at[bsz].set(c_ctx)
    mod_all = _modulation(c_all, w_mod, b_mod).reshape(DEPTH, MOD_ROWS, 6, D_MODEL)
    row_x = lambda b, t: (b, 0, 0)
    row_c = lambda b, t: (bsz, 0, 0)
    tm_x, tm_c = 512, lc

    xc = ctx
    for l in range(DEPTH):
        last = l == DEPTH - 1
        mod = mod_all[l]
        w_main = jnp.concatenate([w_in[l][:, :GATE_OFF], w_in[l][:, SGU_OFF:]], axis=1).astype(BF16)
        b_main = jnp.concatenate([b_in[l][:GATE_OFF], b_in[l][SGU_OFF:]])[None, :]
        w_gate, b_gate = _gate_weights(w_in[l], b_in[l])
        ng1, ng2 = norm1_g[l][None, :], norm2_g[l][None, :]
        mix_w = (mlstm_g[l][None, :], sgu_ln_g[l][None, :], sgu_ln_b[l][None, :], sgu_w[l].astype(BF16),
                 jnp.repeat(sgu_b[l].T, SGU_WIDTH // SGU_GROUPS, axis=1),
                 conv_w[l], conv_b[l][None, :], conv_ln_g[l][None, :], conv_ln_b[l][None, :],
                 w_out[l].astype(BF16))
        wgu, wd = w_gu[l].astype(BF16), w_down[l].astype(BF16)
        fg = final_g[None, :]

        k_x, qt_x, kt_x, vt_x, rest_x, rows_x, ecol_x = _inproj(
            x, mod, row_x, ng1, w_main, b_main, w_gate, b_gate, tm_x)
        ctx_proj = _inproj(xc.reshape(1, bsz * lc, D_MODEL), mod, row_c, ng1, w_main, b_main, w_gate, b_gate,
                           tm_x)
        k_c, qt_c, kt_c, vt_c, rest_c, rows_c, ecol_c = (
            a.reshape((bsz, a.shape[1] // bsz) + a.shape[2:]) for a in ctx_proj)
        hs = _mlstm((k_c, qt_c, kt_c, vt_c, rows_c, ecol_c), (k_x, qt_x, kt_x, vt_x, rows_x, ecol_x),
                    ctx_out=not last)
        x = _mix(x, mod, row_x, rest_x, hs[-1], *mix_w, tm=tm_x, grid_mode=True)
        x = _ffn(x, mod, row_x, ng2, wgu, wd, fg, tm_x, final=last)
        if not last:
            xc = _mix(xc, mod, row_c, rest_c, hs[0], *mix_w, tm=tm_c, grid_mode=False)
            xc = _ffn(xc.reshape(1, bsz * lc, D_MODEL), mod, row_c, ng2, wgu, wd, fg, tm_x,
                      final=False).reshape(bsz, lc, D_MODEL)
    return x
```

```python
import functools

import jax
import jax.numpy as jnp
from jax import lax
from jax.experimental import pallas as pl
from jax.experimental.pallas import tpu as pltpu

D_MODEL = 1024
DEPTH = 2
GRID_W = 64
EPS = 1e-6

HEADS = 4
HEAD_DIM = 128
MLSTM_WIDTH = HEADS * HEAD_DIM
CHUNK = 128
UNITS = 2 * HEADS
SGU_GROUPS = 4
SGU_WIDTH = 256
CONV_WIDTH = 256
CONV_K = 31
CONV_R = CONV_K // 2
FFN_HIDDEN = 2816
FFN_CHUNK = 256

GATE_OFF = 4 * MLSTM_WIDTH
N_GATES = 4 * HEADS
SGU_OFF = GATE_OFF + N_GATES
QKV_COLS = 3 * MLSTM_WIDTH
QK_COLS = 2 * MLSTM_WIDTH
REST_COLS = MLSTM_WIDTH + 2 * SGU_WIDTH + 2 * CONV_WIDTH
MAIN_COLS = QKV_COLS + REST_COLS
LANES = 128
SUBLANES = 8
BF16_ROWS = 16
MOD_ROWS = 32
MOD_TILE = 1536
TOKEN_TILE = 512
N_ROWQ = 5
AUG = HEAD_DIM + BF16_ROWS

VMEM_LIMIT = 56 * 1024 * 1024

F32 = jnp.float32
BF16 = jnp.bfloat16
NT = (((1,), (1,)), ((), ()))

GELU_C = 0.7978845608028654
GELU_A = 0.044715


def _sigmoid(t):
    return 0.5 * jnp.tanh(0.5 * t) + 0.5


def _silu(t):
    return t * _sigmoid(t)


def _gelu_tanh(t):
    inner = t * (GELU_C * GELU_A * (t * t) + GELU_C)
    return t * (0.5 * jnp.tanh(inner) + 0.5)


def _log_sigmoid(t):
    return jnp.minimum(t, 0.0) - jnp.log1p(jnp.exp(-jnp.abs(t)))


def _layernorm(t, g, b):
    mu = jnp.mean(t, axis=-1, keepdims=True)
    d = t - mu
    var = jnp.mean(d * d, axis=-1, keepdims=True)
    return d * lax.rsqrt(var + EPS) * g + b


def _rmsnorm(t, g):
    return t * lax.rsqrt(jnp.mean(t * t, axis=-1, keepdims=True) + EPS) * g


def _modulated_rmsnorm(t, g, scale, shift):
    return t * lax.rsqrt(jnp.mean(t * t, axis=-1, keepdims=True) + EPS) * (g * (1.0 + scale)) + shift


def _bf16_rows(row, n):
    tile = jnp.broadcast_to(row, (BF16_ROWS, row.shape[1])).astype(BF16)
    return jnp.concatenate([tile] * (n // BF16_ROWS), axis=0)


def _params(*sem):
    return pltpu.CompilerParams(dimension_semantics=sem, vmem_limit_bytes=VMEM_LIMIT)


def _split_bf16(t, terms):
    parts = []
    for _ in range(terms):
        parts.append(t.astype(BF16))
        t = t - parts[-1].astype(F32)
    return parts


def _mod_kernel(c_ref, w_ref, b_ref, o_ref):
    a = _split_bf16(_silu(c_ref[...]), 3)
    w_hi, w_lo = _split_bf16(w_ref[0], 2)
    hi = jnp.dot(jnp.concatenate(a, axis=0), w_hi, preferred_element_type=F32)
    lo = jnp.dot(jnp.concatenate(a[:2], axis=0), w_lo, preferred_element_type=F32)
    n = MOD_ROWS
    o_ref[0] = hi[0:n] + hi[n:2 * n] + hi[2 * n:3 * n] + lo[0:n] + lo[n:2 * n] + b_ref[0]


def _modulation(c_all, w_mod, b_mod):
    tn = MOD_TILE
    n = w_mod.shape[-1]
    return pl.pallas_call(
        _mod_kernel,
        grid=(DEPTH, n // tn),
        in_specs=[
            pl.BlockSpec((MOD_ROWS, D_MODEL), lambda l, j: (0, 0)),
            pl.BlockSpec((1, D_MODEL, tn), lambda l, j: (l, 0, j)),
            pl.BlockSpec((1, 1, tn), lambda l, j: (l, 0, j)),
        ],
        out_specs=pl.BlockSpec((1, MOD_ROWS, tn), lambda l, j: (l, 0, j)),
        out_shape=jax.ShapeDtypeStruct((DEPTH, MOD_ROWS, n), F32),
        compiler_params=_params("parallel", "parallel"),
        name="modulation",
    )(c_all, w_mod, b_mod.reshape(DEPTH, 1, n))


def _lane_scan(x, op, lane, forward):
    n = x.shape[1]
    k = 1
    while k < n:
        if forward:
            x = jnp.where(lane >= k, op(x, pltpu.roll(x, k, axis=1)), x)
        else:
            x = jnp.where(lane < n - k, op(x, pltpu.roll(x, n - k, axis=1)), x)
        k *= 2
    return x


def _inproj_kernel(x_ref, mod_ref, ng_ref, w_ref, b_ref, wg_ref, bg_ref,
                   k_ref, qt_ref, kt_ref, vt_ref, rest_ref, rows_ref, ecol_ref, *, tm):
    x = x_ref[0]
    xh = _modulated_rmsnorm(x, ng_ref[...], mod_ref[0, 1:2, :], mod_ref[0, 0:1, :]).astype(BF16)

    g2 = jnp.dot(xh, wg_ref[...], preferred_element_type=F32)
    g = g2[:, 0:LANES] + g2[:, LANES:2 * LANES] + bg_ref[...]
    lane = lax.broadcasted_iota(jnp.int32, (UNITS, CHUNK), 1)
    fwd_row = lax.broadcasted_iota(jnp.int32, (UNITS, CHUNK), 0) < HEADS
    pad_rows = jnp.zeros((CHUNK - UNITS, CHUNK), F32)
    for ch in range(tm // CHUNK):
        sl = slice(ch * CHUNK, (ch + 1) * CHUNK)
        gt = g[sl].T
        it = gt[0:UNITS]
        lsf = _log_sigmoid(gt[UNITS:2 * UNITS])
        bt = jnp.where(fwd_row, _lane_scan(lsf, jnp.add, lane, True), _lane_scan(lsf, jnp.add, lane, False))
        et = it - bt
        cme = jnp.where(fwd_row, _lane_scan(et, jnp.maximum, lane, True),
                        _lane_scan(et, jnp.maximum, lane, False))
        b_last = jnp.where(fwd_row, jnp.broadcast_to(bt[:, CHUNK - 1:CHUNK], bt.shape),
                           jnp.broadcast_to(bt[:, 0:1], bt.shape))
        cme_last = jnp.broadcast_to(jnp.max(et, axis=1, keepdims=True), et.shape)
        rows_ref[0, ch] = jnp.concatenate([et, cme, bt, b_last, cme_last], axis=0)
        ecol_ref[0, sl, :] = jnp.concatenate([et, pad_rows], axis=0).T

    p = jnp.dot(xh, w_ref[...], preferred_element_type=F32) + b_ref[...]
    pq = p[:, 0:MLSTM_WIDTH]
    pk = p[:, MLSTM_WIDTH:QK_COLS] * (HEAD_DIM ** -0.5)
    pv = p[:, QK_COLS:QKV_COLS]
    k_ref[0] = pk.astype(BF16)
    rest_ref[0] = p[:, QKV_COLS:MAIN_COLS]
    for ch in range(tm // CHUNK):
        sl = slice(ch * CHUNK, (ch + 1) * CHUNK)
        for head in range(HEADS):
            hs = slice(head * HEAD_DIM, (head + 1) * HEAD_DIM)
            qt_ref[0, ch, hs, :] = pq[sl, hs].T.astype(BF16)
            kt_ref[0, ch, hs, :] = pk[sl, hs].T.astype(BF16)
            vt_ref[0, ch, hs, :] = pv[sl, hs].T.astype(BF16)


def _inproj(x, mod, mod_row, ng, w_main, b_main, w_gate, b_gate, tm):
    bsz, seq, _ = x.shape
    nch = seq // CHUNK
    tch = tm // CHUNK
    const = lambda b, t: (0, 0)
    return pl.pallas_call(
        functools.partial(_inproj_kernel, tm=tm),
        grid=(bsz, seq // tm),
        in_specs=[
            pl.BlockSpec((1, tm, D_MODEL), lambda b, t: (b, t, 0)),
            pl.BlockSpec((1, 6, D_MODEL), mod_row),
            pl.BlockSpec((1, D_MODEL), const),
            pl.BlockSpec((D_MODEL, MAIN_COLS), const),
            pl.BlockSpec((1, MAIN_COLS), const),
            pl.BlockSpec((D_MODEL, 2 * LANES), const),
            pl.BlockSpec((1, LANES), const),
        ],
        out_specs=[
            pl.BlockSpec((1, tm, MLSTM_WIDTH), lambda b, t: (b, t, 0)),
            pl.BlockSpec((1, tch, MLSTM_WIDTH, CHUNK), lambda b, t: (b, t, 0, 0)),
            pl.BlockSpec((1, tch, MLSTM_WIDTH, CHUNK), lambda b, t: (b, t, 0, 0)),
            pl.BlockSpec((1, tch, MLSTM_WIDTH, CHUNK), lambda b, t: (b, t, 0, 0)),
            pl.BlockSpec((1, tm, REST_COLS), lambda b, t: (b, t, 0)),
            pl.BlockSpec((1, tch, N_ROWQ * UNITS, CHUNK), lambda b, t: (b, t, 0, 0)),
            pl.BlockSpec((1, tm, LANES), lambda b, t: (b, t, 0)),
        ],
        out_shape=[
            jax.ShapeDtypeStruct((bsz, seq, MLSTM_WIDTH), BF16),
            jax.ShapeDtypeStruct((bsz, nch, MLSTM_WIDTH, CHUNK), BF16),
            jax.ShapeDtypeStruct((bsz, nch, MLSTM_WIDTH, CHUNK), BF16),
            jax.ShapeDtypeStruct((bsz, nch, MLSTM_WIDTH, CHUNK), BF16),
            jax.ShapeDtypeStruct((bsz, seq, REST_COLS), F32),
            jax.ShapeDtypeStruct((bsz, nch, N_ROWQ * UNITS, CHUNK), F32),
            jax.ShapeDtypeStruct((bsz, seq, LANES), F32),
        ],
        compiler_params=_params("parallel", "parallel"),
        name="inproj",
    )(x, mod, ng, w_main, b_main, w_gate, b_gate)


def _mlstm_kernel(kc_ref, qtc_ref, ktc_ref, vtc_ref, rowc_ref, ecc_ref,
                  kx_ref, qtx_ref, ktx_ref, vtx_ref, rowx_ref, ecx_ref, *rest, ctx_out, n_ctx, n_x):
    if ctx_out:
        hc_ref, hx_ref, ct_ref, m_ref, part_ref = rest
    else:
        hx_ref, ct_ref, m_ref, part_ref = rest
        hc_ref = None
    ct_ref[...] = jnp.zeros(ct_ref.shape, F32)
    m_ref[...] = jnp.zeros(m_ref.shape, F32)
    r = lax.broadcasted_iota(jnp.int32, (CHUNK, CHUNK), 0)
    c = lax.broadcasted_iota(jnp.int32, (CHUNK, CHUNK), 1)
    masks = (r <= c, r >= c)
    ones = jnp.ones((BF16_ROWS, CHUNK), BF16)
    zeros = jnp.zeros((CHUNK, CHUNK), BF16)

    def block_diag(a, b):
        return jnp.concatenate([jnp.concatenate([a, zeros], axis=1),
                                jnp.concatenate([zeros, b], axis=1)], axis=0)

    def pair(refs, j, hp, direction, out_ref, mode, slot):
        k_ref, qt_ref, kt_ref, vt_ref, row_ref, ec_ref = refs
        heads = (2 * hp, 2 * hp + 1)
        units = tuple(direction * HEADS + h for h in heads)
        sp = direction * (HEADS // 2) + hp
        toks = slice(j * CHUNK, (j + 1) * CHUNK)
        hsl = [slice(h * HEAD_DIM, (h + 1) * HEAD_DIM) for h in heads]

        def row(q, u):
            return row_ref[0, j, q * UNITS + u:q * UNITS + u + 1, :]

        e, cme, b, b_last, cme_last = ([row(q, u) for u in units] for q in range(N_ROWQ))
        m_old = [m_ref[u, 0:1, :] for u in units]
        ct_old = ct_ref[sp]
        vt_aug = [jnp.concatenate([vt_ref[0, j, s, :], ones], axis=0) for s in hsl]

        if mode is not None:
            qt = [qt_ref[0, j, s, :] for s in hsl]
            k2 = k_ref[0, toks, 2 * hp * HEAD_DIM:(2 * hp + 2) * HEAD_DIM]
            qkt = jnp.dot(k2, block_diag(qt[0], qt[1]), preferred_element_type=F32)

        g_last = [jnp.maximum(m_old[i], cme_last[i]) for i in range(2)]
        kst = [kt_ref[0, j, hsl[i], :] * _bf16_rows(jnp.exp(e[i] - g_last[i]), CHUNK) for i in range(2)]
        upd = lax.dot_general(jnp.concatenate(vt_aug, axis=1), block_diag(kst[0], kst[1]), NT,
                              preferred_element_type=F32)
        keep = jnp.concatenate([jnp.exp(m_old[i] - g_last[i]) for i in range(2)], axis=1)
        ct_ref[sp] = keep * ct_old + upd
        for i in range(2):
            m_ref[units[i]] = jnp.broadcast_to(b_last[i] + g_last[i], m_ref.shape[1:])

        yield
        if mode is None:
            return
        ct_b = ct_old.astype(BF16)
        for i in range(2):
            u, cols = units[i], hsl[i]
            g = jnp.maximum(m_old[i], cme[i])
            dl = jnp.where(masks[direction], ec_ref[0, toks, u:u + 1] - g, -jnp.inf)
            st = (qkt[:, i * CHUNK:(i + 1) * CHUNK] * jnp.exp(dl)).astype(BF16)
            qs = qt[i] * _bf16_rows(jnp.exp(m_old[i] - g), CHUNK)
            res = jnp.dot(jnp.concatenate([vt_aug[i], ct_b[:, i * HEAD_DIM:(i + 1) * HEAD_DIM]], axis=1),
                          jnp.concatenate([st, qs], axis=0), preferred_element_type=F32)
            den = jnp.maximum(jnp.abs(res[HEAD_DIM:HEAD_DIM + 1, :]), jnp.exp(-(b[i] + g)))
            ht = res[0:HEAD_DIM, :] * (1.0 / den)
            if mode == "first":
                part_ref[direction, slot, cols, :] = ht
            else:
                tot = part_ref[1 - direction, slot, cols, :] + ht
                hn = tot * lax.rsqrt(jnp.mean(tot * tot, axis=0, keepdims=True) + EPS)
                out_ref[0, toks, cols] = hn.T

    def step(refs, j, n, out_ref, mode):
        slot = None if mode is None else (j if mode == "first" else n - 1 - j)
        pairs = []
        for hp in range(HEADS // 2):
            pairs.append(pair(refs, j, hp, 0, out_ref, mode, slot))
            pairs.append(pair(refs, n - 1 - j, hp, 1, out_ref, mode, slot))
        for phase in range(2):
            for p in pairs:
                next(p, None)

    crefs = (kc_ref, qtc_ref, ktc_ref, vtc_ref, rowc_ref, ecc_ref)
    xrefs = (kx_ref, qtx_ref, ktx_ref, vtx_ref, rowx_ref, ecx_ref)
    for j in range(n_ctx):
        mode = None if not ctx_out else ("first" if j < n_ctx // 2 else "second")
        step(crefs, j, n_ctx, hc_ref, mode)
    for j in range(n_x):
        step(xrefs, j, n_x, hx_ref, "first" if j < n_x // 2 else "second")


def _mlstm(ctx_in, x_in, ctx_out):
    bsz, lc, _ = ctx_in[0].shape
    lx = x_in[0].shape[1]
    n_ctx, n_x = lc // CHUNK, lx // CHUNK
    assert n_ctx % 2 == 0 and n_x % 2 == 0
    i3 = lambda b: (b, 0, 0)
    i4 = lambda b: (b, 0, 0, 0)

    def specs(seq, nch):
        return [
            pl.BlockSpec((1, seq, MLSTM_WIDTH), i3),
            pl.BlockSpec((1, nch, MLSTM_WIDTH, CHUNK), i4),
            pl.BlockSpec((1, nch, MLSTM_WIDTH, CHUNK), i4),
            pl.BlockSpec((1, nch, MLSTM_WIDTH, CHUNK), i4),
            pl.BlockSpec((1, nch, N_ROWQ * UNITS, CHUNK), i4),
            pl.BlockSpec((1, seq, LANES), i3),
        ]

    out_specs = [pl.BlockSpec((1, lx, MLSTM_WIDTH), i3)]
    out_shape = [jax.ShapeDtypeStruct((bsz, lx, MLSTM_WIDTH), F32)]
    if ctx_out:
        out_specs.insert(0, pl.BlockSpec((1, lc, MLSTM_WIDTH), i3))
        out_shape.insert(0, jax.ShapeDtypeStruct((bsz, lc, MLSTM_WIDTH), F32))
    return pl.pallas_call(
        functools.partial(_mlstm_kernel, ctx_out=ctx_out, n_ctx=n_ctx, n_x=n_x),
        grid=(bsz,),
        in_specs=specs(lc, n_ctx) + specs(lx, n_x),
        out_specs=out_specs,
        out_shape=out_shape,
        scratch_shapes=[
            pltpu.VMEM((UNITS // 2, AUG, 2 * HEAD_DIM), F32),
            pltpu.VMEM((UNITS, SUBLANES, LANES), F32),
            pltpu.VMEM((2, max(n_ctx, n_x) // 2, MLSTM_WIDTH, CHUNK), F32),
        ],
        compiler_params=_params("parallel"),
        name="mlstm",
    )(*ctx_in, *x_in)


def _shift_conv(y, w_ref, lo, width, pos):
    n = y.shape[0]
    acc = jnp.zeros(y.shape, F32)
    for j in range(CONV_K):
        d = j - CONV_R
        w = w_ref[j:j + 1, lo:lo + y.shape[1]]
        if d == 0:
            acc = acc + w * y
        else:
            sh = pltpu.roll(y, (-d) % n, axis=0)
            valid = (pos >= -d) if d < 0 else (pos < width - d)
            acc = acc + w * jnp.where(valid, sh, 0.0)
    return acc


def _mix_kernel(x_ref, mod_ref, os_ref, h_ref, z_ref, mg_ref, slg_ref, slb_ref, sw_ref, sb_ref,
                cw_ref, cb_ref, clg_ref, clb_ref, wo_ref, out_ref, *scratch, seq, tm, grid_mode):
    t = pl.program_id(1)
    r0 = pl.multiple_of(t * tm, tm)
    rows = pl.ds(r0, tm)
    half = CONV_WIDTH // 2
    pad = CONV_R * GRID_W

    if grid_mode:
        (ycol_ref,) = scratch

        @pl.when(t == 0)
        def _():
            ycol_ref[0:pad, :] = jnp.zeros((pad, half), F32)
            ycol_ref[pad + seq:pad + seq + pad, :] = jnp.zeros((pad, half), F32)
            ycol_ref[pad:pad + seq, :] = z_ref[0, :, half:CONV_WIDTH] * _sigmoid(
                z_ref[0, :, CONV_WIDTH + half:2 * CONV_WIDTH])

        pos = lax.broadcasted_iota(jnp.int32, (tm, half), 0) & (GRID_W - 1)
        y_row = z_ref[0, rows, 0:half] * _sigmoid(z_ref[0, rows, CONV_WIDTH:CONV_WIDTH + half])
        conv_r = _shift_conv(y_row, cw_ref, 0, GRID_W, pos)
        conv_c = jnp.zeros((tm, half), F32)
        for j in range(CONV_K):
            start = pl.multiple_of(r0 + GRID_W * j, GRID_W)
            conv_c = conv_c + cw_ref[j:j + 1, half:CONV_WIDTH] * ycol_ref[pl.ds(start, tm), :]
        conv = jnp.concatenate([conv_r, conv_c], axis=1)
    else:
        pos = lax.broadcasted_iota(jnp.int32, (tm, CONV_WIDTH), 0)
        y = z_ref[0, :, 0:CONV_WIDTH] * _sigmoid(z_ref[0, :, CONV_WIDTH:2 * CONV_WIDTH])
        conv = _shift_conv(y, cw_ref, 0, tm, pos)
    y_c = _silu(_layernorm(conv + cb_ref[...], clg_ref[...], clb_ref[...]))

    zs = _gelu_tanh(os_ref[0, :, MLSTM_WIDTH:MLSTM_WIDTH + 2 * SGU_WIDTH])
    u = zs[:, 0:SGU_WIDTH]
    vn = _layernorm(zs[:, SGU_WIDTH:2 * SGU_WIDTH], slg_ref[...], slb_ref[...]).astype(BF16)
    lane_group = lax.broadcasted_iota(jnp.int32, (CHUNK, SGU_WIDTH), 1) // (SGU_WIDTH // SGU_GROUPS)
    mixed = []
    for ch in range(tm // CHUNK):
        vc = vn[ch * CHUNK:(ch + 1) * CHUNK]
        m = sb_ref[...]
        for g in range(SGU_GROUPS):
            m = m + jnp.where(lane_group == g, jnp.dot(sw_ref[g], vc, preferred_element_type=F32), 0.0)
        mixed.append(m)
    y_b = u * jnp.concatenate(mixed, axis=0)

    y_a = h_ref[0] * mg_ref[...] * _sigmoid(os_ref[0, :, 0:MLSTM_WIDTH])

    o_a, o_b = MLSTM_WIDTH, MLSTM_WIDTH + SGU_WIDTH
    proj = jnp.dot(y_a.astype(BF16), wo_ref[0:o_a, :], preferred_element_type=F32)
    proj = proj + jnp.dot(y_b.astype(BF16), wo_ref[o_a:o_b, :], preferred_element_type=F32)
    proj = proj + jnp.dot(y_c.astype(BF16), wo_ref[o_b:, :], preferred_element_type=F32)
    out_ref[0] = x_ref[0] + mod_ref[0, 2:3, :] * proj


def _mix(x, mod, mod_row, rest, h, mg, slg, slb, sw, sbias, cw, cb, clg, clb, wo, tm, grid_mode):
    bsz, seq, _ = x.shape
    const2 = lambda b, t: (0, 0)
    const3 = lambda b, t: (0, 0, 0)
    scratch = []
    if grid_mode:
        scratch.append(pltpu.VMEM((seq + 2 * CONV_R * GRID_W, CONV_WIDTH // 2), F32))
    return pl.pallas_call(
        functools.partial(_mix_kernel, seq=seq, tm=tm, grid_mode=grid_mode),
        grid=(bsz, seq // tm),
        in_specs=[
            pl.BlockSpec((1, tm, D_MODEL), lambda b, t: (b, t, 0)),
            pl.BlockSpec((1, 6, D_MODEL), mod_row),
            pl.BlockSpec((1, tm, MLSTM_WIDTH + 2 * SGU_WIDTH), lambda b, t: (b, t, 0)),
            pl.BlockSpec((1, tm, MLSTM_WIDTH), lambda b, t: (b, t, 0)),
            pl.BlockSpec((1, seq, 2 * CONV_WIDTH), lambda b, t: (b, 0, 2)),
            pl.BlockSpec((1, MLSTM_WIDTH), const2),
            pl.BlockSpec((1, SGU_WIDTH), const2),
            pl.BlockSpec((1, SGU_WIDTH), const2),
            pl.BlockSpec((SGU_GROUPS, CHUNK, CHUNK), const3),
            pl.BlockSpec((CHUNK, SGU_WIDTH), const2),
            pl.BlockSpec((CONV_K, CONV_WIDTH), const2),
            pl.BlockSpec((1, CONV_WIDTH), const2),
            pl.BlockSpec((1, CONV_WIDTH), const2),
            pl.BlockSpec((1, CONV_WIDTH), const2),
            pl.BlockSpec((D_MODEL, D_MODEL), const2),
        ],
        out_specs=pl.BlockSpec((1, tm, D_MODEL), lambda b, t: (b, t, 0)),
        out_shape=jax.ShapeDtypeStruct((bsz, seq, D_MODEL), F32),
        scratch_shapes=scratch,
        compiler_params=_params("parallel", "arbitrary"),
        name="mix",
    )(x, mod, rest, h, rest, mg, slg, slb, sw, sbias, cw, cb, clg, clb, wo)


def _ffn_kernel(x_ref, mod_ref, ng_ref, wgu_ref, wd_ref, fg_ref, out_ref, *, final):
    x = x_ref[0]
    xm = _modulated_rmsnorm(x, ng_ref[...], mod_ref[0, 4:5, :], mod_ref[0, 3:4, :]).astype(BF16)
    acc = jnp.zeros(x.shape, F32)
    for ch in range(FFN_HIDDEN // FFN_CHUNK):
        lo = ch * FFN_CHUNK
        gate = jnp.dot(xm, wgu_ref[:, lo:lo + FFN_CHUNK], preferred_element_type=F32)
        up = jnp.dot(xm, wgu_ref[:, FFN_HIDDEN + lo:FFN_HIDDEN + lo + FFN_CHUNK], preferred_element_type=F32)
        act = (_silu(gate) * up).astype(BF16)
        acc = acc + jnp.dot(act, wd_ref[lo:lo + FFN_CHUNK, :], preferred_element_type=F32)
    y = x + mod_ref[0, 5:6, :] * acc
    if final:
        y = _rmsnorm(y, fg_ref[...])
    out_ref[0] = y


def _ffn(x, mod, mod_row, ng, wgu, wd, fg, tm, final):
    bsz, seq, _ = x.shape
    const = lambda b, t: (0, 0)
    return pl.pallas_call(
        functools.partial(_ffn_kernel, final=final),
        grid=(bsz, seq // tm),
        in_specs=[
            pl.BlockSpec((1, tm, D_MODEL), lambda b, t: (b, t, 0)),
            pl.BlockSpec((1, 6, D_MODEL), mod_row),
            pl.BlockSpec((1, D_MODEL), const),
            pl.BlockSpec((D_MODEL, 2 * FFN_HIDDEN), const),
            pl.BlockSpec((FFN_HIDDEN, D_MODEL), const),
            pl.BlockSpec((1, D_MODEL), const),
        ],
        out_specs=pl.BlockSpec((1, tm, D_MODEL), lambda b, t: (b, t, 0)),
        out_shape=jax.ShapeDtypeStruct((bsz, seq, D_MODEL), F32),
        compiler_params=_params("parallel", "parallel"),
        name="ffn",
    )(x, mod, ng, wgu, wd, fg)


def _gate_weights(w_in_l, b_in_l):
    wg = w_in_l[:, GATE_OFF:SGU_OFF]
    bg = b_in_l[GATE_OFF:SGU_OFF]
    order = [wg[:, 0:HEADS], wg[:, 2 * HEADS:3 * HEADS], wg[:, HEADS:2 * HEADS], wg[:, 3 * HEADS:4 * HEADS]]
    w = jnp.pad(jnp.concatenate(order, axis=1), ((0, 0), (0, LANES - 2 * UNITS)))
    b_order = [bg[0:HEADS], bg[2 * HEADS:3 * HEADS], bg[HEADS:2 * HEADS], bg[3 * HEADS:4 * HEADS]]
    b = jnp.pad(jnp.concatenate(b_order), (0, LANES - 2 * UNITS))[None, :]
    return jnp.concatenate(_split_bf16(w, 2), axis=1), b


def kernel(x, c, ctx, c_ctx, w_mod, b_mod, norm1_g, w_in, b_in, mlstm_g, sgu_ln_g, sgu_ln_b, sgu_w, sgu_b,
           conv_w, conv_b, conv_ln_g, conv_ln_b, w_out, norm2_g, w_gu, w_down, final_g):
    bsz = x.shape[0]
    lc = ctx.shape[1]
    assert bsz + 1 <= MOD_ROWS
    c_all = jnp.zeros((MOD_ROWS, D_MODEL), F32).at[:bsz].set(c).at[bsz].set(c_ctx)
    mod_all = _modulation(c_all, w_mod, b_mod).reshape(DEPTH, MOD_ROWS, 6, D_MODEL)
    row_x = lambda b, t: (b, 0, 0)
    row_c = lambda b, t: (bsz, 0, 0)
    tm_x, tm_c = TOKEN_TILE, lc

    xc = ctx
    for l in range(DEPTH):
        last = l == DEPTH - 1
        mod = mod_all[l]
        w_main = jnp.concatenate([w_in[l][:, :GATE_OFF], w_in[l][:, SGU_OFF:]], axis=1).astype(BF16)
        b_main = jnp.concatenate([b_in[l][:GATE_OFF], b_in[l][SGU_OFF:]])[None, :]
        w_gate, b_gate = _gate_weights(w_in[l], b_in[l])
        ng1, ng2 = norm1_g[l][None, :], norm2_g[l][None, :]
        mix_w = (mlstm_g[l][None, :], sgu_ln_g[l][None, :], sgu_ln_b[l][None, :], sgu_w[l].astype(BF16),
                 jnp.repeat(sgu_b[l].T, SGU_WIDTH // SGU_GROUPS, axis=1),
                 conv_w[l], conv_b[l][None, :], conv_ln_g[l][None, :], conv_ln_b[l][None, :],
                 w_out[l].astype(BF16))
        wgu, wd = w_gu[l].astype(BF16), w_down[l].astype(BF16)
        fg = final_g[None, :]

        k_x, qt_x, kt_x, vt_x, rest_x, rows_x, ecol_x = _inproj(
            x, mod, row_x, ng1, w_main, b_main, w_gate, b_gate, tm_x)
        ctx_proj = _inproj(xc.reshape(1, bsz * lc, D_MODEL), mod, row_c, ng1, w_main, b_main, w_gate, b_gate,
                           tm_x)
        k_c, qt_c, kt_c, vt_c, rest_c, rows_c, ecol_c = (
            a.reshape((bsz, a.shape[1] // bsz) + a.shape[2:]) for a in ctx_proj)
        hs = _mlstm((k_c, qt_c, kt_c, vt_c, rows_c, ecol_c), (k_x, qt_x, kt_x, vt_x, rows_x, ecol_x),
                    ctx_out=not last)
        x = _mix(x, mod, row_x, rest_x, hs[-1], *mix_w, tm=tm_x, grid_mode=True)
        x = _ffn(x, mod, row_x, ng2, wgu, wd, fg, tm_x, final=last)
        if not last:
            xc = _mix(xc, mod, row_c, rest_c, hs[0], *mix_w, tm=tm_c, grid_mode=False)
            xc = _ffn(xc.reshape(1, bsz * lc, D_MODEL), mod, row_c, ng2, wgu, wd, fg, tm_x,
                      final=False).reshape(bsz, lc, D_MODEL)
    return x
```

```python
import functools

import jax
import jax.numpy as jnp
from jax import lax
from jax.experimental import pallas as pl
from jax.experimental.pallas import tpu as pltpu

D_MODEL = 1024
DEPTH = 2
GRID_W = 64
EPS = 1e-6

HEADS = 4
HEAD_DIM = 128
MLSTM_WIDTH = HEADS * HEAD_DIM
CHUNK = 128
UNITS = 2 * HEADS
SGU_GROUPS = 4
SGU_WIDTH = 256
CONV_WIDTH = 256
CONV_K = 31
CONV_R = CONV_K // 2
FFN_HIDDEN = 2816
FFN_CHUNK = 256

GATE_OFF = 4 * MLSTM_WIDTH
N_GATES = 4 * HEADS
SGU_OFF = GATE_OFF + N_GATES
QKV_COLS = 3 * MLSTM_WIDTH
QK_COLS = 2 * MLSTM_WIDTH
REST_COLS = MLSTM_WIDTH + 2 * SGU_WIDTH + 2 * CONV_WIDTH
MAIN_COLS = QKV_COLS + REST_COLS
LANES = 128
SUBLANES = 8
BF16_ROWS = 16
MOD_ROWS = 32
MOD_TILE = 1536
TOKEN_TILE = 512
N_ROWQ = 5
AUG = HEAD_DIM + BF16_ROWS

VMEM_LIMIT = 56 * 1024 * 1024

F32 = jnp.float32
BF16 = jnp.bfloat16
NT = (((1,), (1,)), ((), ()))

GELU_C = 0.7978845608028654
GELU_A = 0.044715


def _sigmoid(t):
    return 0.5 * jnp.tanh(0.5 * t) + 0.5


def _silu(t):
    return t * _sigmoid(t)


def _gelu_tanh(t):
    inner = t * (GELU_C * GELU_A * (t * t) + GELU_C)
    return t * (0.5 * jnp.tanh(inner) + 0.5)


def _log_sigmoid(t):
    return jnp.minimum(t, 0.0) - jnp.log1p(jnp.exp(-jnp.abs(t)))


def _layernorm(t, g, b):
    mu = jnp.mean(t, axis=-1, keepdims=True)
    d = t - mu
    var = jnp.mean(d * d, axis=-1, keepdims=True)
    return d * lax.rsqrt(var + EPS) * g + b


def _rmsnorm(t, g):
    return t * lax.rsqrt(jnp.mean(t * t, axis=-1, keepdims=True) + EPS) * g


def _modulated_rmsnorm(t, g, scale, shift):
    return t * lax.rsqrt(jnp.mean(t * t, axis=-1, keepdims=True) + EPS) * (g * (1.0 + scale)) + shift


def _bf16_rows(row, n):
    tile = jnp.broadcast_to(row, (BF16_ROWS, row.shape[1])).astype(BF16)
    return jnp.concatenate([tile] * (n // BF16_ROWS), axis=0)


def _params(*sem):
    return pltpu.CompilerParams(dimension_semantics=sem, vmem_limit_bytes=VMEM_LIMIT)


def _split_bf16(t, terms):
    parts = []
    for _ in range(terms):
        parts.append(t.astype(BF16))
        t = t - parts[-1].astype(F32)
    return parts


def _mod_kernel(c_ref, w_ref, b_ref, o_ref):
    a = _split_bf16(_silu(c_ref[...]), 3)
    w_hi, w_lo = _split_bf16(w_ref[0], 2)
    hi = jnp.dot(jnp.concatenate(a, axis=0), w_hi, preferred_element_type=F32)
    lo = jnp.dot(jnp.concatenate(a[:2], axis=0), w_lo, preferred_element_type=F32)
    n = MOD_ROWS
    o_ref[0] = hi[0:n] + hi[n:2 * n] + hi[2 * n:3 * n] + lo[0:n] + lo[n:2 * n] + b_ref[0]


def _modulation(c_all, w_mod, b_mod):
    tn = MOD_TILE
    n = w_mod.shape[-1]
    return pl.pallas_call(
        _mod_kernel,
        grid=(DEPTH, n // tn),
        in_specs=[
            pl.BlockSpec((MOD_ROWS, D_MODEL), lambda l, j: (0, 0)),
            pl.BlockSpec((1, D_MODEL, tn), lambda l, j: (l, 0, j)),
            pl.BlockSpec((1, 1, tn), lambda l, j: (l, 0, j)),
        ],
        out_specs=pl.BlockSpec((1, MOD_ROWS, tn), lambda l, j: (l, 0, j)),
        out_shape=jax.ShapeDtypeStruct((DEPTH, MOD_ROWS, n), F32),
        compiler_params=_params("parallel", "parallel"),
        name="modulation",
    )(c_all, w_mod, b_mod.reshape(DEPTH, 1, n))


def _lane_scan(x, op, lane, forward):
    n = x.shape[1]
    k = 1
    while k < n:
        if forward:
            x = jnp.where(lane >= k, op(x, pltpu.roll(x, k, axis=1)), x)
        else:
            x = jnp.where(lane < n - k, op(x, pltpu.roll(x, n - k, axis=1)), x)
        k *= 2
    return x


def _inproj_kernel(x_ref, mod_ref, ng_ref, w_ref, b_ref, wg_ref, bg_ref, *outs, tm, state_only):
    if state_only:
        kt_ref, vt_ref, rows_ref = outs
    else:
        k_ref, qt_ref, kt_ref, vt_ref, rest_ref, rows_ref, ecol_ref = outs
    x = x_ref[0]
    xh = _modulated_rmsnorm(x, ng_ref[...], mod_ref[0, 1:2, :], mod_ref[0, 0:1, :]).astype(BF16)

    g2 = jnp.dot(xh, wg_ref[...], preferred_element_type=F32)
    g = g2[:, 0:LANES] + g2[:, LANES:2 * LANES] + bg_ref[...]
    lane = lax.broadcasted_iota(jnp.int32, (UNITS, CHUNK), 1)
    fwd_row = lax.broadcasted_iota(jnp.int32, (UNITS, CHUNK), 0) < HEADS
    pad_rows = jnp.zeros((CHUNK - UNITS, CHUNK), F32)
    for ch in range(tm // CHUNK):
        sl = slice(ch * CHUNK, (ch + 1) * CHUNK)
        gt = g[sl].T
        it = gt[0:UNITS]
        lsf = _log_sigmoid(gt[UNITS:2 * UNITS])
        bt = jnp.where(fwd_row, _lane_scan(lsf, jnp.add, lane, True), _lane_scan(lsf, jnp.add, lane, False))
        et = it - bt
        cme = jnp.where(fwd_row, _lane_scan(et, jnp.maximum, lane, True),
                        _lane_scan(et, jnp.maximum, lane, False))
        b_last = jnp.where(fwd_row, jnp.broadcast_to(bt[:, CHUNK - 1:CHUNK], bt.shape),
                           jnp.broadcast_to(bt[:, 0:1], bt.shape))
        cme_last = jnp.broadcast_to(jnp.max(et, axis=1, keepdims=True), et.shape)
        rows_ref[0, ch] = jnp.concatenate([et, cme, bt, b_last, cme_last], axis=0)
        if not state_only:
            ecol_ref[0, sl, :] = jnp.concatenate([et, pad_rows], axis=0).T

    first = MLSTM_WIDTH if state_only else 0
    last = QKV_COLS if state_only else MAIN_COLS
    p = jnp.dot(xh, w_ref[:, first:last], preferred_element_type=F32) + b_ref[:, first:last]
    pk = p[:, MLSTM_WIDTH - first:QK_COLS - first] * (HEAD_DIM ** -0.5)
    pv = p[:, QK_COLS - first:QKV_COLS - first]
    if not state_only:
        pq = p[:, 0:MLSTM_WIDTH]
        k_ref[0] = pk.astype(BF16)
        rest_ref[0] = p[:, QKV_COLS:MAIN_COLS]
    for ch in range(tm // CHUNK):
        sl = slice(ch * CHUNK, (ch + 1) * CHUNK)
        for head in range(HEADS):
            hs = slice(head * HEAD_DIM, (head + 1) * HEAD_DIM)
            if not state_only:
                qt_ref[0, ch, hs, :] = pq[sl, hs].T.astype(BF16)
            kt_ref[0, ch, hs, :] = pk[sl, hs].T.astype(BF16)
            vt_ref[0, ch, hs, :] = pv[sl, hs].T.astype(BF16)


def _inproj(x, mod, mod_row, ng, w_main, b_main, w_gate, b_gate, tm, state_only=False):
    bsz, seq, _ = x.shape
    nch = seq // CHUNK
    tch = tm // CHUNK
    const = lambda b, t: (0, 0)
    keep = (2, 3, 5) if state_only else tuple(range(7))
    return pl.pallas_call(
        functools.partial(_inproj_kernel, tm=tm, state_only=state_only),
        grid=(bsz, seq // tm),
        in_specs=[
            pl.BlockSpec((1, tm, D_MODEL), lambda b, t: (b, t, 0)),
            pl.BlockSpec((1, 6, D_MODEL), mod_row),
            pl.BlockSpec((1, D_MODEL), const),
            pl.BlockSpec((D_MODEL, MAIN_COLS), const),
            pl.BlockSpec((1, MAIN_COLS), const),
            pl.BlockSpec((D_MODEL, 2 * LANES), const),
            pl.BlockSpec((1, LANES), const),
        ],
        out_specs=[spec for i, spec in enumerate([
            pl.BlockSpec((1, tm, MLSTM_WIDTH), lambda b, t: (b, t, 0)),
            pl.BlockSpec((1, tch, MLSTM_WIDTH, CHUNK), lambda b, t: (b, t, 0, 0)),
            pl.BlockSpec((1, tch, MLSTM_WIDTH, CHUNK), lambda b, t: (b, t, 0, 0)),
            pl.BlockSpec((1, tch, MLSTM_WIDTH, CHUNK), lambda b, t: (b, t, 0, 0)),
            pl.BlockSpec((1, tm, REST_COLS), lambda b, t: (b, t, 0)),
            pl.BlockSpec((1, tch, N_ROWQ * UNITS, CHUNK), lambda b, t: (b, t, 0, 0)),
            pl.BlockSpec((1, tm, LANES), lambda b, t: (b, t, 0)),
        ]) if i in keep],
        out_shape=[shape for i, shape in enumerate([
            jax.ShapeDtypeStruct((bsz, seq, MLSTM_WIDTH), BF16),
            jax.ShapeDtypeStruct((bsz, nch, MLSTM_WIDTH, CHUNK), BF16),
            jax.ShapeDtypeStruct((bsz, nch, MLSTM_WIDTH, CHUNK), BF16),
            jax.ShapeDtypeStruct((bsz, nch, MLSTM_WIDTH, CHUNK), BF16),
            jax.ShapeDtypeStruct((bsz, seq, REST_COLS), F32),
            jax.ShapeDtypeStruct((bsz, nch, N_ROWQ * UNITS, CHUNK), F32),
            jax.ShapeDtypeStruct((bsz, seq, LANES), F32),
        ]) if i in keep],
        compiler_params=_params("parallel", "parallel"),
        name="inproj",
    )(x, mod, ng, w_main, b_main, w_gate, b_gate)


def _mlstm_kernel(*refs, ctx_out, n_ctx, n_x):
    if ctx_out:
        crefs, xrefs = refs[0:6], refs[6:12]
        hc_ref, hx_ref, ct_ref, m_ref, part_ref = refs[12:]
    else:
        crefs, xrefs = (None, None) + refs[0:3] + (None,), refs[3:9]
        hx_ref, ct_ref, m_ref, part_ref = refs[9:]
        hc_ref = None
    ct_ref[...] = jnp.zeros(ct_ref.shape, F32)
    m_ref[...] = jnp.zeros(m_ref.shape, F32)
    r = lax.broadcasted_iota(jnp.int32, (CHUNK, CHUNK), 0)
    c = lax.broadcasted_iota(jnp.int32, (CHUNK, CHUNK), 1)
    masks = (r <= c, r >= c)
    ones = jnp.ones((BF16_ROWS, CHUNK), BF16)
    zeros = jnp.zeros((CHUNK, CHUNK), BF16)

    def block_diag(a, b):
        return jnp.concatenate([jnp.concatenate([a, zeros], axis=1),
                                jnp.concatenate([zeros, b], axis=1)], axis=0)

    def pair(refs, j, hp, direction, out_ref, mode, slot):
        k_ref, qt_ref, kt_ref, vt_ref, row_ref, ec_ref = refs
        heads = (2 * hp, 2 * hp + 1)
        units = tuple(direction * HEADS + h for h in heads)
        sp = direction * (HEADS // 2) + hp
        toks = slice(j * CHUNK, (j + 1) * CHUNK)
        hsl = [slice(h * HEAD_DIM, (h + 1) * HEAD_DIM) for h in heads]

        def row(q, u):
            return row_ref[0, j, q * UNITS + u:q * UNITS + u + 1, :]

        e, cme, b, b_last, cme_last = ([row(q, u) for u in units] for q in range(N_ROWQ))
        m_old = [m_ref[u, 0:1, :] for u in units]
        ct_old = ct_ref[sp]
        vt_aug = [jnp.concatenate([vt_ref[0, j, s, :], ones], axis=0) for s in hsl]

        if mode is not None:
            qt = [qt_ref[0, j, s, :] for s in hsl]
            k2 = k_ref[0, toks, 2 * hp * HEAD_DIM:(2 * hp + 2) * HEAD_DIM]
            qkt = jnp.dot(k2, block_diag(qt[0], qt[1]), preferred_element_type=F32)

        g_last = [jnp.maximum(m_old[i], cme_last[i]) for i in range(2)]
        kst = [kt_ref[0, j, hsl[i], :] * _bf16_rows(jnp.exp(e[i] - g_last[i]), CHUNK) for i in range(2)]
        upd = lax.dot_general(jnp.concatenate(vt_aug, axis=1), block_diag(kst[0], kst[1]), NT,
                              preferred_element_type=F32)
        keep = jnp.concatenate([jnp.exp(m_old[i] - g_last[i]) for i in range(2)], axis=1)
        ct_ref[sp] = keep * ct_old + upd
        for i in range(2):
            m_ref[units[i]] = jnp.broadcast_to(b_last[i] + g_last[i], m_ref.shape[1:])

        yield
        if mode is None:
            return
        ct_b = ct_old.astype(BF16)
        for i in range(2):
            u, cols = units[i], hsl[i]
            g = jnp.maximum(m_old[i], cme[i])
            dl = jnp.where(masks[direction], ec_ref[0, toks, u:u + 1] - g, -jnp.inf)
            st = (qkt[:, i * CHUNK:(i + 1) * CHUNK] * jnp.exp(dl)).astype(BF16)
            qs = qt[i] * _bf16_rows(jnp.exp(m_old[i] - g), CHUNK)
            res = jnp.dot(jnp.concatenate([vt_aug[i], ct_b[:, i * HEAD_DIM:(i + 1) * HEAD_DIM]], axis=1),
                          jnp.concatenate([st, qs], axis=0), preferred_element_type=F32)
            den = jnp.maximum(jnp.abs(res[HEAD_DIM:HEAD_DIM + 1, :]), jnp.exp(-(b[i] + g)))
            ht = res[0:HEAD_DIM, :] * (1.0 / den)
            if mode == "first":
                part_ref[direction, slot, cols, :] = ht
            else:
                tot = part_ref[1 - direction, slot, cols, :] + ht
                hn = tot * lax.rsqrt(jnp.mean(tot * tot, axis=0, keepdims=True) + EPS)
                out_ref[0, toks, cols] = hn.T

    def step(refs, j, n, out_ref, mode):
        slot = None if mode is None else (j if mode == "first" else n - 1 - j)
        pairs = []
        for hp in range(HEADS // 2):
            pairs.append(pair(refs, j, hp, 0, out_ref, mode, slot))
            pairs.append(pair(refs, n - 1 - j, hp, 1, out_ref, mode, slot))
        for phase in range(2):
            for p in pairs:
                next(p, None)

    for j in range(n_ctx):
        mode = None if not ctx_out else ("first" if j < n_ctx // 2 else "second")
        step(crefs, j, n_ctx, hc_ref, mode)
    for j in range(n_x):
        step(xrefs, j, n_x, hx_ref, "first" if j < n_x // 2 else "second")


def _mlstm(ctx_in, x_in, ctx_out):
    bsz, lx, _ = x_in[0].shape
    n_ctx, n_x = ctx_in[4 if ctx_out else 2].shape[1], lx // CHUNK
    lc = n_ctx * CHUNK
    assert n_ctx % 2 == 0 and n_x % 2 == 0
    i3 = lambda b: (b, 0, 0)
    i4 = lambda b: (b, 0, 0, 0)

    def specs(seq, nch):
        return [
            pl.BlockSpec((1, seq, MLSTM_WIDTH), i3),
            pl.BlockSpec((1, nch, MLSTM_WIDTH, CHUNK), i4),
            pl.BlockSpec((1, nch, MLSTM_WIDTH, CHUNK), i4),
            pl.BlockSpec((1, nch, MLSTM_WIDTH, CHUNK), i4),
            pl.BlockSpec((1, nch, N_ROWQ * UNITS, CHUNK), i4),
            pl.BlockSpec((1, seq, LANES), i3),
        ]

    out_specs = [pl.BlockSpec((1, lx, MLSTM_WIDTH), i3)]
    out_shape = [jax.ShapeDtypeStruct((bsz, lx, MLSTM_WIDTH), F32)]
    if ctx_out:
        out_specs.insert(0, pl.BlockSpec((1, lc, MLSTM_WIDTH), i3))
        out_shape.insert(0, jax.ShapeDtypeStruct((bsz, lc, MLSTM_WIDTH), F32))
    return pl.pallas_call(
        functools.partial(_mlstm_kernel, ctx_out=ctx_out, n_ctx=n_ctx, n_x=n_x),
        grid=(bsz,),
        in_specs=(specs(lc, n_ctx) if ctx_out else specs(lc, n_ctx)[2:5]) + specs(lx, n_x),
        out_specs=out_specs,
        out_shape=out_shape,
        scratch_shapes=[
            pltpu.VMEM((UNITS // 2, AUG, 2 * HEAD_DIM), F32),
            pltpu.VMEM((UNITS, SUBLANES, LANES), F32),
            pltpu.VMEM((2, max(n_ctx, n_x) // 2, MLSTM_WIDTH, CHUNK), F32),
        ],
        compiler_params=_params("parallel"),
        name="mlstm",
    )(*ctx_in, *x_in)


def _shift_conv(y, w_ref, lo, width, pos):
    n = y.shape[0]
    acc = jnp.zeros(y.shape, F32)
    for j in range(CONV_K):
        d = j - CONV_R
        w = w_ref[j:j + 1, lo:lo + y.shape[1]]
        if d == 0:
            acc = acc + w * y
        else:
            sh = pltpu.roll(y, (-d) % n, axis=0)
            valid = (pos >= -d) if d < 0 else (pos < width - d)
            acc = acc + w * jnp.where(valid, sh, 0.0)
    return acc


def _mix_kernel(x_ref, mod_ref, os_ref, h_ref, z_ref, mg_ref, slg_ref, slb_ref, sw_ref, sb_ref,
                cw_ref, cb_ref, clg_ref, clb_ref, wo_ref, out_ref, *scratch, seq, tm, grid_mode):
    t = pl.program_id(1)
    r0 = pl.multiple_of(t * tm, tm)
    rows = pl.ds(r0, tm)
    half = CONV_WIDTH // 2
    pad = CONV_R * GRID_W

    if grid_mode:
        (ycol_ref,) = scratch

        @pl.when(t == 0)
        def _():
            ycol_ref[0:pad, :] = jnp.zeros((pad, half), F32)
            ycol_ref[pad + seq:pad + seq + pad, :] = jnp.zeros((pad, half), F32)
            ycol_ref[pad:pad + seq, :] = z_ref[0, :, half:CONV_WIDTH] * _sigmoid(
                z_ref[0, :, CONV_WIDTH + half:2 * CONV_WIDTH])

        pos = lax.broadcasted_iota(jnp.int32, (tm, half), 0) & (GRID_W - 1)
        y_row = z_ref[0, rows, 0:half] * _sigmoid(z_ref[0, rows, CONV_WIDTH:CONV_WIDTH + half])
        conv_r = _shift_conv(y_row, cw_ref, 0, GRID_W, pos)
        conv_c = jnp.zeros((tm, half), F32)
        for j in range(CONV_K):
            start = pl.multiple_of(r0 + GRID_W * j, GRID_W)
            conv_c = conv_c + cw_ref[j:j + 1, half:CONV_WIDTH] * ycol_ref[pl.ds(start, tm), :]
        conv = jnp.concatenate([conv_r, conv_c], axis=1)
    else:
        pos = lax.broadcasted_iota(jnp.int32, (tm, CONV_WIDTH), 0)
        y = z_ref[0, :, 0:CONV_WIDTH] * _sigmoid(z_ref[0, :, CONV_WIDTH:2 * CONV_WIDTH])
        conv = _shift_conv(y, cw_ref, 0, tm, pos)
    y_c = _silu(_layernorm(conv + cb_ref[...], clg_ref[...], clb_ref[...]))

    zs = _gelu_tanh(os_ref[0, :, MLSTM_WIDTH:MLSTM_WIDTH + 2 * SGU_WIDTH])
    u = zs[:, 0:SGU_WIDTH]
    vn = _layernorm(zs[:, SGU_WIDTH:2 * SGU_WIDTH], slg_ref[...], slb_ref[...]).astype(BF16)
    lane_group = lax.broadcasted_iota(jnp.int32, (CHUNK, SGU_WIDTH), 1) // (SGU_WIDTH // SGU_GROUPS)
    mixed = []
    for ch in range(tm // CHUNK):
        vc = vn[ch * CHUNK:(ch + 1) * CHUNK]
        m = sb_ref[...]
        for g in range(SGU_GROUPS):
            m = m + jnp.where(lane_group == g, jnp.dot(sw_ref[g], vc, preferred_element_type=F32), 0.0)
        mixed.append(m)
    y_b = u * jnp.concatenate(mixed, axis=0)

    y_a = h_ref[0] * mg_ref[...] * _sigmoid(os_ref[0, :, 0:MLSTM_WIDTH])

    o_a, o_b = MLSTM_WIDTH, MLSTM_WIDTH + SGU_WIDTH
    proj = jnp.dot(y_a.astype(BF16), wo_ref[0:o_a, :], preferred_element_type=F32)
    proj = proj + jnp.dot(y_b.astype(BF16), wo_ref[o_a:o_b, :], preferred_element_type=F32)
    proj = proj + jnp.dot(y_c.astype(BF16), wo_ref[o_b:, :], preferred_element_type=F32)
    out_ref[0] = x_ref[0] + mod_ref[0, 2:3, :] * proj


def _mix(x, mod, mod_row, rest, h, mg, slg, slb, sw, sbias, cw, cb, clg, clb, wo, tm, grid_mode):
    bsz, seq, _ = x.shape
    const2 = lambda b, t: (0, 0)
    const3 = lambda b, t: (0, 0, 0)
    scratch = []
    if grid_mode:
        scratch.append(pltpu.VMEM((seq + 2 * CONV_R * GRID_W, CONV_WIDTH // 2), F32))
    return pl.pallas_call(
        functools.partial(_mix_kernel, seq=seq, tm=tm, grid_mode=grid_mode),
        grid=(bsz, seq // tm),
        in_specs=[
            pl.BlockSpec((1, tm, D_MODEL), lambda b, t: (b, t, 0)),
            pl.BlockSpec((1, 6, D_MODEL), mod_row),
            pl.BlockSpec((1, tm, MLSTM_WIDTH + 2 * SGU_WIDTH), lambda b, t: (b, t, 0)),
            pl.BlockSpec((1, tm, MLSTM_WIDTH), lambda b, t: (b, t, 0)),
            pl.BlockSpec((1, seq, 2 * CONV_WIDTH), lambda b, t: (b, 0, 2)),
            pl.BlockSpec((1, MLSTM_WIDTH), const2),
            pl.BlockSpec((1, SGU_WIDTH), const2),
            pl.BlockSpec((1, SGU_WIDTH), const2),
            pl.BlockSpec((SGU_GROUPS, CHUNK, CHUNK), const3),
            pl.BlockSpec((CHUNK, SGU_WIDTH), const2),
            pl.BlockSpec((CONV_K, CONV_WIDTH), const2),
            pl.BlockSpec((1, CONV_WIDTH), const2),
            pl.BlockSpec((1, CONV_WIDTH), const2),
            pl.BlockSpec((1, CONV_WIDTH), const2),
            pl.BlockSpec((D_MODEL, D_MODEL), const2),
        ],
        out_specs=pl.BlockSpec((1, tm, D_MODEL), lambda b, t: (b, t, 0)),
        out_shape=jax.ShapeDtypeStruct((bsz, seq, D_MODEL), F32),
        scratch_shapes=scratch,
        compiler_params=_params("parallel", "arbitrary"),
        name="mix",
    )(x, mod, rest, h, rest, mg, slg, slb, sw, sbias, cw, cb, clg, clb, wo)


def _ffn_kernel(x_ref, mod_ref, ng_ref, wgu_ref, wd_ref, fg_ref, out_ref, *, final):
    x = x_ref[0]
    xm = _modulated_rmsnorm(x, ng_ref[...], mod_ref[0, 4:5, :], mod_ref[0, 3:4, :]).astype(BF16)
    acc = jnp.zeros(x.shape, F32)
    for ch in range(FFN_HIDDEN // FFN_CHUNK):
        lo = ch * FFN_CHUNK
        gate = jnp.dot(xm, wgu_ref[:, lo:lo + FFN_CHUNK], preferred_element_type=F32)
        up = jnp.dot(xm, wgu_ref[:, FFN_HIDDEN + lo:FFN_HIDDEN + lo + FFN_CHUNK], preferred_element_type=F32)
        act = (_silu(gate) * up).astype(BF16)
        acc = acc + jnp.dot(act, wd_ref[lo:lo + FFN_CHUNK, :], preferred_element_type=F32)
    y = x + mod_ref[0, 5:6, :] * acc
    if final:
        y = _rmsnorm(y, fg_ref[...])
    out_ref[0] = y


def _ffn(x, mod, mod_row, ng, wgu, wd, fg, tm, final):
    bsz, seq, _ = x.shape
    const = lambda b, t: (0, 0)
    return pl.pallas_call(
        functools.partial(_ffn_kernel, final=final),
        grid=(bsz, seq // tm),
        in_specs=[
            pl.BlockSpec((1, tm, D_MODEL), lambda b, t: (b, t, 0)),
            pl.BlockSpec((1, 6, D_MODEL), mod_row),
            pl.BlockSpec((1, D_MODEL), const),
            pl.BlockSpec((D_MODEL, 2 * FFN_HIDDEN), const),
            pl.BlockSpec((FFN_HIDDEN, D_MODEL), const),
            pl.BlockSpec((1, D_MODEL), const),
        ],
        out_specs=pl.BlockSpec((1, tm, D_MODEL), lambda b, t: (b, t, 0)),
        out_shape=jax.ShapeDtypeStruct((bsz, seq, D_MODEL), F32),
        compiler_params=_params("parallel", "parallel"),
        name="ffn",
    )(x, mod, ng, wgu, wd, fg)


def _gate_weights(w_in_l, b_in_l):
    wg = w_in_l[:, GATE_OFF:SGU_OFF]
    bg = b_in_l[GATE_OFF:SGU_OFF]
    order = [wg[:, 0:HEADS], wg[:, 2 * HEADS:3 * HEADS], wg[:, HEADS:2 * HEADS], wg[:, 3 * HEADS:4 * HEADS]]
    w = jnp.pad(jnp.concatenate(order, axis=1), ((0, 0), (0, LANES - 2 * UNITS)))
    b_order = [bg[0:HEADS], bg[2 * HEADS:3 * HEADS], bg[HEADS:2 * HEADS], bg[3 * HEADS:4 * HEADS]]
    b = jnp.pad(jnp.concatenate(b_order), (0, LANES - 2 * UNITS))[None, :]
    return jnp.concatenate(_split_bf16(w, 2), axis=1), b


def kernel(x, c, ctx, c_ctx, w_mod, b_mod, norm1_g, w_in, b_in, mlstm_g, sgu_ln_g, sgu_ln_b, sgu_w, sgu_b,
           conv_w, conv_b, conv_ln_g, conv_ln_b, w_out, norm2_g, w_gu, w_down, final_g):
    bsz = x.shape[0]
    lc = ctx.shape[1]
    assert bsz + 1 <= MOD_ROWS
    c_all = jnp.zeros((MOD_ROWS, D_MODEL), F32).at[:bsz].set(c).at[bsz].set(c_ctx)
    mod_all = _modulation(c_all, w_mod, b_mod).reshape(DEPTH, MOD_ROWS, 6, D_MODEL)
    row_x = lambda b, t: (b, 0, 0)
    row_c = lambda b, t: (bsz, 0, 0)
    tm_x, tm_c = TOKEN_TILE, lc

    xc = ctx
    for l in range(DEPTH):
        last = l == DEPTH - 1
        mod = mod_all[l]
        w_main = jnp.concatenate([w_in[l][:, :GATE_OFF], w_in[l][:, SGU_OFF:]], axis=1).astype(BF16)
        b_main = jnp.concatenate([b_in[l][:GATE_OFF], b_in[l][SGU_OFF:]])[None, :]
        w_gate, b_gate = _gate_weights(w_in[l], b_in[l])
        ng1, ng2 = norm1_g[l][None, :], norm2_g[l][None, :]
        mix_w = (mlstm_g[l][None, :], sgu_ln_g[l][None, :], sgu_ln_b[l][None, :], sgu_w[l].astype(BF16),
                 jnp.repeat(sgu_b[l].T, SGU_WIDTH // SGU_GROUPS, axis=1),
                 conv_w[l], conv_b[l][None, :], conv_ln_g[l][None, :], conv_ln_b[l][None, :],
                 w_out[l].astype(BF16))
        wgu, wd = w_gu[l].astype(BF16), w_down[l].astype(BF16)
        fg = final_g[None, :]

        k_x, qt_x, kt_x, vt_x, rest_x, rows_x, ecol_x = _inproj(
            x, mod, row_x, ng1, w_main, b_main, w_gate, b_gate, tm_x)
        ctx_proj = _inproj(xc.reshape(1, bsz * lc, D_MODEL), mod, row_c, ng1, w_main, b_main, w_gate, b_gate,
                           tm_x, state_only=last)
        ctx_proj = tuple(a.reshape((bsz, a.shape[1] // bsz) + a.shape[2:]) for a in ctx_proj)
        if not last:
            k_c, qt_c, kt_c, vt_c, rest_c, rows_c, ecol_c = ctx_proj
            ctx_proj = (k_c, qt_c, kt_c, vt_c, rows_c, ecol_c)
        hs = _mlstm(ctx_proj, (k_x, qt_x, kt_x, vt_x, rows_x, ecol_x), ctx_out=not last)
        x = _mix(x, mod, row_x, rest_x, hs[-1], *mix_w, tm=tm_x, grid_mode=True)
        x = _ffn(x, mod, row_x, ng2, wgu, wd, fg, tm_x, final=last)
        if not last:
            xc = _mix(xc, mod, row_c, rest_c, hs[0], *mix_w, tm=tm_c, grid_mode=False)
            xc = _ffn(xc.reshape(1, bsz * lc, D_MODEL), mod, row_c, ng2, wgu, wd, fg, tm_x,
                      final=False).reshape(bsz, lc, D_MODEL)
    return x
```

```python
import functools

import jax
import jax.numpy as jnp
from jax import lax
from jax.experimental import pallas as pl
from jax.experimental.pallas import tpu as pltpu

D_MODEL = 1024
DEPTH = 2
GRID_W = 64
EPS = 1e-6

HEADS = 4
HEAD_DIM = 128
MLSTM_WIDTH = HEADS * HEAD_DIM
CHUNK = 128
UNITS = 2 * HEADS
SGU_GROUPS = 4
SGU_WIDTH = 256
CONV_WIDTH = 256
CONV_K = 31
CONV_R = CONV_K // 2
FFN_HIDDEN = 2816
FFN_CHUNK = 256

GATE_OFF = 4 * MLSTM_WIDTH
N_GATES = 4 * HEADS
SGU_OFF = GATE_OFF + N_GATES
QKV_COLS = 3 * MLSTM_WIDTH
QK_COLS = 2 * MLSTM_WIDTH
REST_COLS = MLSTM_WIDTH + 2 * SGU_WIDTH + 2 * CONV_WIDTH
MAIN_COLS = QKV_COLS + REST_COLS
LANES = 128
SUBLANES = 8
BF16_ROWS = 16
MOD_ROWS = 32
MOD_TILE = 1536
TOKEN_TILE = 512
N_ROWQ = 5
AUG = HEAD_DIM + BF16_ROWS

VMEM_LIMIT = 56 * 1024 * 1024

F32 = jnp.float32
BF16 = jnp.bfloat16
NT = (((1,), (1,)), ((), ()))

GELU_C = 0.7978845608028654
GELU_A = 0.044715


def _sigmoid(t):
    return 0.5 * jnp.tanh(0.5 * t) + 0.5


def _silu(t):
    return t * _sigmoid(t)


def _gelu_tanh(t):
    inner = t * (GELU_C * GELU_A * (t * t) + GELU_C)
    return t * (0.5 * jnp.tanh(inner) + 0.5)


def _log_sigmoid(t):
    return jnp.minimum(t, 0.0) - jnp.log1p(jnp.exp(-jnp.abs(t)))


def _layernorm(t, g, b):
    mu = jnp.mean(t, axis=-1, keepdims=True)
    d = t - mu
    var = jnp.mean(d * d, axis=-1, keepdims=True)
    return d * lax.rsqrt(var + EPS) * g + b


def _rmsnorm(t, g):
    return t * lax.rsqrt(jnp.mean(t * t, axis=-1, keepdims=True) + EPS) * g


def _modulated_rmsnorm(t, g, scale, shift):
    return t * lax.rsqrt(jnp.mean(t * t, axis=-1, keepdims=True) + EPS) * (g * (1.0 + scale)) + shift


def _bf16_rows(row, n):
    tile = jnp.broadcast_to(row, (BF16_ROWS, row.shape[1])).astype(BF16)
    return jnp.concatenate([tile] * (n // BF16_ROWS), axis=0)


def _params(*sem):
    return pltpu.CompilerParams(dimension_semantics=sem, vmem_limit_bytes=VMEM_LIMIT)


def _split_bf16(t, terms):
    parts = []
    for _ in range(terms):
        parts.append(t.astype(BF16))
        t = t - parts[-1].astype(F32)
    return parts


def _mod_kernel(c_ref, w_ref, b_ref, o_ref):
    a = _split_bf16(_silu(c_ref[...]), 3)
    w_hi, w_lo = _split_bf16(w_ref[0], 2)
    hi = jnp.dot(jnp.concatenate(a, axis=0), w_hi, preferred_element_type=F32)
    lo = jnp.dot(jnp.concatenate(a[:2], axis=0), w_lo, preferred_element_type=F32)
    n = MOD_ROWS
    o_ref[0] = hi[0:n] + hi[n:2 * n] + hi[2 * n:3 * n] + lo[0:n] + lo[n:2 * n] + b_ref[0]


def _modulation(c_all, w_mod, b_mod):
    tn = MOD_TILE
    n = w_mod.shape[-1]
    return pl.pallas_call(
        _mod_kernel,
        grid=(DEPTH, n // tn),
        in_specs=[
            pl.BlockSpec((MOD_ROWS, D_MODEL), lambda l, j: (0, 0)),
            pl.BlockSpec((1, D_MODEL, tn), lambda l, j: (l, 0, j)),
            pl.BlockSpec((1, 1, tn), lambda l, j: (l, 0, j)),
        ],
        out_specs=pl.BlockSpec((1, MOD_ROWS, tn), lambda l, j: (l, 0, j)),
        out_shape=jax.ShapeDtypeStruct((DEPTH, MOD_ROWS, n), F32),
        compiler_params=_params("parallel", "parallel"),
        name="modulation",
    )(c_all, w_mod, b_mod.reshape(DEPTH, 1, n))


def _lane_scan(x, op, lane, forward):
    n = x.shape[1]
    k = 1
    while k < n:
        if forward:
            x = jnp.where(lane >= k, op(x, pltpu.roll(x, k, axis=1)), x)
        else:
            x = jnp.where(lane < n - k, op(x, pltpu.roll(x, n - k, axis=1)), x)
        k *= 2
    return x


def _inproj_kernel(x_ref, mod_ref, ng_ref, w_ref, b_ref, wg_ref, bg_ref, *outs, tm, state_only):
    if state_only:
        kt_ref, vt_ref, rows_ref = outs
    else:
        k_ref, qt_ref, kt_ref, vt_ref, rest_ref, rows_ref, ecol_ref = outs
    x = x_ref[0]
    xh = _modulated_rmsnorm(x, ng_ref[...], mod_ref[0, 1:2, :], mod_ref[0, 0:1, :]).astype(BF16)

    g2 = jnp.dot(xh, wg_ref[...], preferred_element_type=F32)
    g = g2[:, 0:LANES] + g2[:, LANES:2 * LANES] + bg_ref[...]
    lane = lax.broadcasted_iota(jnp.int32, (UNITS, CHUNK), 1)
    fwd_row = lax.broadcasted_iota(jnp.int32, (UNITS, CHUNK), 0) < HEADS
    pad_rows = jnp.zeros((CHUNK - UNITS, CHUNK), F32)
    for ch in range(tm // CHUNK):
        sl = slice(ch * CHUNK, (ch + 1) * CHUNK)
        gt = g[sl].T
        it = gt[0:UNITS]
        lsf = _log_sigmoid(gt[UNITS:2 * UNITS])
        bt = jnp.where(fwd_row, _lane_scan(lsf, jnp.add, lane, True), _lane_scan(lsf, jnp.add, lane, False))
        et = it - bt
        cme = jnp.where(fwd_row, _lane_scan(et, jnp.maximum, lane, True),
                        _lane_scan(et, jnp.maximum, lane, False))
        b_last = jnp.where(fwd_row, jnp.broadcast_to(bt[:, CHUNK - 1:CHUNK], bt.shape),
                           jnp.broadcast_to(bt[:, 0:1], bt.shape))
        cme_last = jnp.broadcast_to(jnp.max(et, axis=1, keepdims=True), et.shape)
        rows_ref[0, ch] = jnp.concatenate([et, cme, bt, b_last, cme_last], axis=0)
        if not state_only:
            ecol_ref[0, sl, :] = jnp.concatenate([et, pad_rows], axis=0).T

    first = MLSTM_WIDTH if state_only else 0
    last = QKV_COLS if state_only else MAIN_COLS
    p = jnp.dot(xh, w_ref[:, first:last], preferred_element_type=F32) + b_ref[:, first:last]
    pk = p[:, MLSTM_WIDTH - first:QK_COLS - first] * (HEAD_DIM ** -0.5)
    pv = p[:, QK_COLS - first:QKV_COLS - first]
    if not state_only:
        pq = p[:, 0:MLSTM_WIDTH]
        k_ref[0] = pk.astype(BF16)
        rest_ref[0] = p[:, QKV_COLS:MAIN_COLS]
    for ch in range(tm // CHUNK):
        sl = slice(ch * CHUNK, (ch + 1) * CHUNK)
        for head in range(HEADS):
            hs = slice(head * HEAD_DIM, (head + 1) * HEAD_DIM)
            if not state_only:
                qt_ref[0, ch, hs, :] = pq[sl, hs].T.astype(BF16)
            kt_ref[0, ch, hs, :] = pk[sl, hs].T.astype(BF16)
            vt_ref[0, ch, hs, :] = pv[sl, hs].T.astype(BF16)


def _inproj(x, mod, mod_row, ng, layer, w_main, b_main, w_gate, b_gate, tm, state_only=False):
    bsz, seq, _ = x.shape
    nch = seq // CHUNK
    tch = tm // CHUNK
    const = lambda b, t: (0, 0)
    of_layer = lambda b, t: (layer, 0, 0)
    keep = (2, 3, 5) if state_only else tuple(range(7))
    return pl.pallas_call(
        functools.partial(_inproj_kernel, tm=tm, state_only=state_only),
        grid=(bsz, seq // tm),
        in_specs=[
            pl.BlockSpec((1, tm, D_MODEL), lambda b, t: (b, t, 0)),
            pl.BlockSpec((1, 6, D_MODEL), mod_row),
            pl.BlockSpec((1, D_MODEL), const),
            pl.BlockSpec((None, D_MODEL, MAIN_COLS), of_layer),
            pl.BlockSpec((None, 1, MAIN_COLS), of_layer),
            pl.BlockSpec((None, D_MODEL, 2 * LANES), of_layer),
            pl.BlockSpec((None, 1, LANES), of_layer),
        ],
        out_specs=[spec for i, spec in enumerate([
            pl.BlockSpec((1, tm, MLSTM_WIDTH), lambda b, t: (b, t, 0)),
            pl.BlockSpec((1, tch, MLSTM_WIDTH, CHUNK), lambda b, t: (b, t, 0, 0)),
            pl.BlockSpec((1, tch, MLSTM_WIDTH, CHUNK), lambda b, t: (b, t, 0, 0)),
            pl.BlockSpec((1, tch, MLSTM_WIDTH, CHUNK), lambda b, t: (b, t, 0, 0)),
            pl.BlockSpec((1, tm, REST_COLS), lambda b, t: (b, t, 0)),
            pl.BlockSpec((1, tch, N_ROWQ * UNITS, CHUNK), lambda b, t: (b, t, 0, 0)),
            pl.BlockSpec((1, tm, LANES), lambda b, t: (b, t, 0)),
        ]) if i in keep],
        out_shape=[shape for i, shape in enumerate([
            jax.ShapeDtypeStruct((bsz, seq, MLSTM_WIDTH), BF16),
            jax.ShapeDtypeStruct((bsz, nch, MLSTM_WIDTH, CHUNK), BF16),
            jax.ShapeDtypeStruct((bsz, nch, MLSTM_WIDTH, CHUNK), BF16),
            jax.ShapeDtypeStruct((bsz, nch, MLSTM_WIDTH, CHUNK), BF16),
            jax.ShapeDtypeStruct((bsz, seq, REST_COLS), F32),
            jax.ShapeDtypeStruct((bsz, nch, N_ROWQ * UNITS, CHUNK), F32),
            jax.ShapeDtypeStruct((bsz, seq, LANES), F32),
        ]) if i in keep],
        compiler_params=_params("parallel", "parallel"),
        name="inproj",
    )(x, mod, ng, w_main, b_main, w_gate, b_gate)


def _mlstm_kernel(*refs, ctx_out, n_ctx, n_x):
    if ctx_out:
        crefs, xrefs = refs[0:6], refs[6:12]
        hc_ref, hx_ref, ct_ref, m_ref, part_ref = refs[12:]
    else:
        crefs, xrefs = (None, None) + refs[0:3] + (None,), refs[3:9]
        hx_ref, ct_ref, m_ref, part_ref = refs[9:]
        hc_ref = None
    ct_ref[...] = jnp.zeros(ct_ref.shape, F32)
    m_ref[...] = jnp.zeros(m_ref.shape, F32)
    r = lax.broadcasted_iota(jnp.int32, (CHUNK, CHUNK), 0)
    c = lax.broadcasted_iota(jnp.int32, (CHUNK, CHUNK), 1)
    masks = (r <= c, r >= c)
    ones = jnp.ones((BF16_ROWS, CHUNK), BF16)
    zeros = jnp.zeros((CHUNK, CHUNK), BF16)

    def block_diag(a, b):
        return jnp.concatenate([jnp.concatenate([a, zeros], axis=1),
                                jnp.concatenate([zeros, b], axis=1)], axis=0)

    def pair(refs, j, hp, direction, out_ref, mode, slot):
        k_ref, qt_ref, kt_ref, vt_ref, row_ref, ec_ref = refs
        heads = (2 * hp, 2 * hp + 1)
        units = tuple(direction * HEADS + h for h in heads)
        sp = direction * (HEADS // 2) + hp
        toks = slice(j * CHUNK, (j + 1) * CHUNK)
        hsl = [slice(h * HEAD_DIM, (h + 1) * HEAD_DIM) for h in heads]

        def row(q, u):
            return row_ref[0, j, q * UNITS + u:q * UNITS + u + 1, :]

        e, cme, b, b_last, cme_last = ([row(q, u) for u in units] for q in range(N_ROWQ))
        m_old = [m_ref[u, 0:1, :] for u in units]
        ct_old = ct_ref[sp]
        vt_aug = [jnp.concatenate([vt_ref[0, j, s, :], ones], axis=0) for s in hsl]

        if mode is not None:
            qt = [qt_ref[0, j, s, :] for s in hsl]
            k2 = k_ref[0, toks, 2 * hp * HEAD_DIM:(2 * hp + 2) * HEAD_DIM]
            qkt = jnp.dot(k2, block_diag(qt[0], qt[1]), preferred_element_type=F32)

        g_last = [jnp.maximum(m_old[i], cme_last[i]) for i in range(2)]
        kst = [kt_ref[0, j, hsl[i], :] * _bf16_rows(jnp.exp(e[i] - g_last[i]), CHUNK) for i in range(2)]
        upd = lax.dot_general(jnp.concatenate(vt_aug, axis=1), block_diag(kst[0], kst[1]), NT,
                              preferred_element_type=F32)
        keep = jnp.concatenate([jnp.exp(m_old[i] - g_last[i]) for i in range(2)], axis=1)
        ct_ref[sp] = keep * ct_old + upd
        for i in range(2):
            m_ref[units[i]] = jnp.broadcast_to(b_last[i] + g_last[i], m_ref.shape[1:])

        yield
        if mode is None:
            return
        ct_b = ct_old.astype(BF16)
        for i in range(2):
            u, cols = units[i], hsl[i]
            g = jnp.maximum(m_old[i], cme[i])
            dl = jnp.where(masks[direction], ec_ref[0, toks, u:u + 1] - g, -jnp.inf)
            st = (qkt[:, i * CHUNK:(i + 1) * CHUNK] * jnp.exp(dl)).astype(BF16)
            qs = qt[i] * _bf16_rows(jnp.exp(m_old[i] - g), CHUNK)
            res = jnp.dot(jnp.concatenate([vt_aug[i], ct_b[:, i * HEAD_DIM:(i + 1) * HEAD_DIM]], axis=1),
                          jnp.concatenate([st, qs], axis=0), preferred_element_type=F32)
            den = jnp.maximum(jnp.abs(res[HEAD_DIM:HEAD_DIM + 1, :]), jnp.exp(-(b[i] + g)))
            ht = res[0:HEAD_DIM, :] * (1.0 / den)
            if mode == "first":
                part_ref[direction, slot, cols, :] = ht
            else:
                tot = part_ref[1 - direction, slot, cols, :] + ht
                hn = tot * lax.rsqrt(jnp.mean(tot * tot, axis=0, keepdims=True) + EPS)
                out_ref[0, toks, cols] = hn.T

    def step(refs, j, n, out_ref, mode):
        slot = None if mode is None else (j if mode == "first" else n - 1 - j)
        pairs = []
        for hp in range(HEADS // 2):
            pairs.append(pair(refs, j, hp, 0, out_ref, mode, slot))
            pairs.append(pair(refs, n - 1 - j, hp, 1, out_ref, mode, slot))
        for phase in range(2):
            for p in pairs:
                next(p, None)

    for j in range(n_ctx):
        mode = None if not ctx_out else ("first" if j < n_ctx // 2 else "second")
        step(crefs, j, n_ctx, hc_ref, mode)
    for j in range(n_x):
        step(xrefs, j, n_x, hx_ref, "first" if j < n_x // 2 else "second")


def _mlstm(ctx_in, x_in, ctx_out):
    bsz, lx, _ = x_in[0].shape
    n_ctx, n_x = ctx_in[4 if ctx_out else 2].shape[1], lx // CHUNK
    lc = n_ctx * CHUNK
    assert n_ctx % 2 == 0 and n_x % 2 == 0
    i3 = lambda b: (b, 0, 0)
    i4 = lambda b: (b, 0, 0, 0)

    def specs(seq, nch):
        return [
            pl.BlockSpec((1, seq, MLSTM_WIDTH), i3),
            pl.BlockSpec((1, nch, MLSTM_WIDTH, CHUNK), i4),
            pl.BlockSpec((1, nch, MLSTM_WIDTH, CHUNK), i4),
            pl.BlockSpec((1, nch, MLSTM_WIDTH, CHUNK), i4),
            pl.BlockSpec((1, nch, N_ROWQ * UNITS, CHUNK), i4),
            pl.BlockSpec((1, seq, LANES), i3),
        ]

    out_specs = [pl.BlockSpec((1, lx, MLSTM_WIDTH), i3)]
    out_shape = [jax.ShapeDtypeStruct((bsz, lx, MLSTM_WIDTH), F32)]
    if ctx_out:
        out_specs.insert(0, pl.BlockSpec((1, lc, MLSTM_WIDTH), i3))
        out_shape.insert(0, jax.ShapeDtypeStruct((bsz, lc, MLSTM_WIDTH), F32))
    return pl.pallas_call(
        functools.partial(_mlstm_kernel, ctx_out=ctx_out, n_ctx=n_ctx, n_x=n_x),
        grid=(bsz,),
        in_specs=(specs(lc, n_ctx) if ctx_out else specs(lc, n_ctx)[2:5]) + specs(lx, n_x),
        out_specs=out_specs,
        out_shape=out_shape,
        scratch_shapes=[
            pltpu.VMEM((UNITS // 2, AUG, 2 * HEAD_DIM), F32),
            pltpu.VMEM((UNITS, SUBLANES, LANES), F32),
            pltpu.VMEM((2, max(n_ctx, n_x) // 2, MLSTM_WIDTH, CHUNK), F32),
        ],
        compiler_params=_params("parallel"),
        name="mlstm",
    )(*ctx_in, *x_in)


def _shift_conv(y, w_ref, lo, width, pos):
    n = y.shape[0]
    acc = jnp.zeros(y.shape, F32)
    for j in range(CONV_K):
        d = j - CONV_R
        w = w_ref[j:j + 1, lo:lo + y.shape[1]]
        if d == 0:
            acc = acc + w * y
        else:
            sh = pltpu.roll(y, (-d) % n, axis=0)
            valid = (pos >= -d) if d < 0 else (pos < width - d)
            acc = acc + w * jnp.where(valid, sh, 0.0)
    return acc


def _mix_kernel(x_ref, mod_ref, os_ref, h_ref, z_ref, mg_ref, slg_ref, slb_ref, sw_ref, sb_ref,
                cw_ref, cb_ref, clg_ref, clb_ref, wo_ref, out_ref, *scratch, seq, tm, grid_mode):
    t = pl.program_id(1)
    r0 = pl.multiple_of(t * tm, tm)
    rows = pl.ds(r0, tm)
    half = CONV_WIDTH // 2
    pad = CONV_R * GRID_W

    if grid_mode:
        (ycol_ref,) = scratch

        @pl.when(t == 0)
        def _():
            ycol_ref[0:pad, :] = jnp.zeros((pad, half), F32)
            ycol_ref[pad + seq:pad + seq + pad, :] = jnp.zeros((pad, half), F32)
            ycol_ref[pad:pad + seq, :] = z_ref[0, :, half:CONV_WIDTH] * _sigmoid(
                z_ref[0, :, CONV_WIDTH + half:2 * CONV_WIDTH])

        pos = lax.broadcasted_iota(jnp.int32, (tm, half), 0) & (GRID_W - 1)
        y_row = z_ref[0, rows, 0:half] * _sigmoid(z_ref[0, rows, CONV_WIDTH:CONV_WIDTH + half])
        conv_r = _shift_conv(y_row, cw_ref, 0, GRID_W, pos)
        conv_c = jnp.zeros((tm, half), F32)
        for j in range(CONV_K):
            start = pl.multiple_of(r0 + GRID_W * j, GRID_W)
            conv_c = conv_c + cw_ref[j:j + 1, half:CONV_WIDTH] * ycol_ref[pl.ds(start, tm), :]
        conv = jnp.concatenate([conv_r, conv_c], axis=1)
    else:
        pos = lax.broadcasted_iota(jnp.int32, (tm, CONV_WIDTH), 0)
        y = z_ref[0, :, 0:CONV_WIDTH] * _sigmoid(z_ref[0, :, CONV_WIDTH:2 * CONV_WIDTH])
        conv = _shift_conv(y, cw_ref, 0, tm, pos)
    y_c = _silu(_layernorm(conv + cb_ref[...], clg_ref[...], clb_ref[...]))

    zs = _gelu_tanh(os_ref[0, :, MLSTM_WIDTH:MLSTM_WIDTH + 2 * SGU_WIDTH])
    u = zs[:, 0:SGU_WIDTH]
    vn = _layernorm(zs[:, SGU_WIDTH:2 * SGU_WIDTH], slg_ref[...], slb_ref[...]).astype(BF16)
    lane_group = lax.broadcasted_iota(jnp.int32, (CHUNK, SGU_WIDTH), 1) // (SGU_WIDTH // SGU_GROUPS)
    mixed = []
    for ch in range(tm // CHUNK):
        vc = vn[ch * CHUNK:(ch + 1) * CHUNK]
        m = sb_ref[...]
        for g in range(SGU_GROUPS):
            m = m + jnp.where(lane_group == g, jnp.dot(sw_ref[g], vc, preferred_element_type=F32), 0.0)
        mixed.append(m)
    y_b = u * jnp.concatenate(mixed, axis=0)

    y_a = h_ref[0] * mg_ref[...] * _sigmoid(os_ref[0, :, 0:MLSTM_WIDTH])

    o_a, o_b = MLSTM_WIDTH, MLSTM_WIDTH + SGU_WIDTH
    proj = jnp.dot(y_a.astype(BF16), wo_ref[0:o_a, :], preferred_element_type=F32)
    proj = proj + jnp.dot(y_b.astype(BF16), wo_ref[o_a:o_b, :], preferred_element_type=F32)
    proj = proj + jnp.dot(y_c.astype(BF16), wo_ref[o_b:, :], preferred_element_type=F32)
    out_ref[0] = x_ref[0] + mod_ref[0, 2:3, :] * proj


def _mix(x, mod, mod_row, rest, h, mg, slg, slb, sw, sbias, cw, cb, clg, clb, layer, wo, tm, grid_mode):
    bsz, seq, _ = x.shape
    const2 = lambda b, t: (0, 0)
    const3 = lambda b, t: (0, 0, 0)
    scratch = []
    if grid_mode:
        scratch.append(pltpu.VMEM((seq + 2 * CONV_R * GRID_W, CONV_WIDTH // 2), F32))
    return pl.pallas_call(
        functools.partial(_mix_kernel, seq=seq, tm=tm, grid_mode=grid_mode),
        grid=(bsz, seq // tm),
        in_specs=[
            pl.BlockSpec((1, tm, D_MODEL), lambda b, t: (b, t, 0)),
            pl.BlockSpec((1, 6, D_MODEL), mod_row),
            pl.BlockSpec((1, tm, MLSTM_WIDTH + 2 * SGU_WIDTH), lambda b, t: (b, t, 0)),
            pl.BlockSpec((1, tm, MLSTM_WIDTH), lambda b, t: (b, t, 0)),
            pl.BlockSpec((1, seq, 2 * CONV_WIDTH), lambda b, t: (b, 0, 2)),
            pl.BlockSpec((1, MLSTM_WIDTH), const2),
            pl.BlockSpec((1, SGU_WIDTH), const2),
            pl.BlockSpec((1, SGU_WIDTH), const2),
            pl.BlockSpec((SGU_GROUPS, CHUNK, CHUNK), const3),
            pl.BlockSpec((CHUNK, SGU_WIDTH), const2),
            pl.BlockSpec((CONV_K, CONV_WIDTH), const2),
            pl.BlockSpec((1, CONV_WIDTH), const2),
            pl.BlockSpec((1, CONV_WIDTH), const2),
            pl.BlockSpec((1, CONV_WIDTH), const2),
            pl.BlockSpec((None, D_MODEL, D_MODEL), lambda b, t: (layer, 0, 0)),
        ],
        out_specs=pl.BlockSpec((1, tm, D_MODEL), lambda b, t: (b, t, 0)),
        out_shape=jax.ShapeDtypeStruct((bsz, seq, D_MODEL), F32),
        scratch_shapes=scratch,
        compiler_params=_params("parallel", "arbitrary"),
        name="mix",
    )(x, mod, rest, h, rest, mg, slg, slb, sw, sbias, cw, cb, clg, clb, wo)


def _ffn_kernel(x_ref, mod_ref, ng_ref, wgu_ref, wd_ref, fg_ref, out_ref, *, final):
    x = x_ref[0]
    xm = _modulated_rmsnorm(x, ng_ref[...], mod_ref[0, 4:5, :], mod_ref[0, 3:4, :]).astype(BF16)
    acc = jnp.zeros(x.shape, F32)
    for ch in range(FFN_HIDDEN // FFN_CHUNK):
        lo = ch * FFN_CHUNK
        gate = jnp.dot(xm, wgu_ref[:, lo:lo + FFN_CHUNK], preferred_element_type=F32)
        up = jnp.dot(xm, wgu_ref[:, FFN_HIDDEN + lo:FFN_HIDDEN + lo + FFN_CHUNK], preferred_element_type=F32)
        act = (_silu(gate) * up).astype(BF16)
        acc = acc + jnp.dot(act, wd_ref[lo:lo + FFN_CHUNK, :], preferred_element_type=F32)
    y = x + mod_ref[0, 5:6, :] * acc
    if final:
        y = _rmsnorm(y, fg_ref[...])
    out_ref[0] = y


def _ffn(x, mod, mod_row, ng, layer, wgu, wd, fg, tm, final):
    bsz, seq, _ = x.shape
    const = lambda b, t: (0, 0)
    return pl.pallas_call(
        functools.partial(_ffn_kernel, final=final),
        grid=(bsz, seq // tm),
        in_specs=[
            pl.BlockSpec((1, tm, D_MODEL), lambda b, t: (b, t, 0)),
            pl.BlockSpec((1, 6, D_MODEL), mod_row),
            pl.BlockSpec((1, D_MODEL), const),
            pl.BlockSpec((None, D_MODEL, 2 * FFN_HIDDEN), lambda b, t: (layer, 0, 0)),
            pl.BlockSpec((None, FFN_HIDDEN, D_MODEL), lambda b, t: (layer, 0, 0)),
            pl.BlockSpec((1, D_MODEL), const),
        ],
        out_specs=pl.BlockSpec((1, tm, D_MODEL), lambda b, t: (b, t, 0)),
        out_shape=jax.ShapeDtypeStruct((bsz, seq, D_MODEL), F32),
        compiler_params=_params("parallel", "parallel"),
        name="ffn",
    )(x, mod, ng, wgu, wd, fg)


def _gate_weights(w_in, b_in):
    def regroup(t):
        g = t[..., GATE_OFF:SGU_OFF]
        order = [g[..., 0:HEADS], g[..., 2 * HEADS:3 * HEADS], g[..., HEADS:2 * HEADS],
                 g[..., 3 * HEADS:4 * HEADS]]
        pad = [(0, 0)] * (t.ndim - 1) + [(0, LANES - 2 * UNITS)]
        return jnp.pad(jnp.concatenate(order, axis=-1), pad)

    return jnp.concatenate(_split_bf16(regroup(w_in), 2), axis=-1), regroup(b_in)[:, None, :]


def kernel(x, c, ctx, c_ctx, w_mod, b_mod, norm1_g, w_in, b_in, mlstm_g, sgu_ln_g, sgu_ln_b, sgu_w, sgu_b,
           conv_w, conv_b, conv_ln_g, conv_ln_b, w_out, norm2_g, w_gu, w_down, final_g):
    bsz = x.shape[0]
    lc = ctx.shape[1]
    assert bsz + 1 <= MOD_ROWS
    c_all = jnp.zeros((MOD_ROWS, D_MODEL), F32).at[:bsz].set(c).at[bsz].set(c_ctx)
    mod_all = _modulation(c_all, w_mod, b_mod).reshape(DEPTH, MOD_ROWS, 6, D_MODEL)
    row_x = lambda b, t: (b, 0, 0)
    row_c = lambda b, t: (bsz, 0, 0)
    tm_x, tm_c = TOKEN_TILE, lc

    w_main = jnp.concatenate([w_in[:, :, :GATE_OFF], w_in[:, :, SGU_OFF:]], axis=2).astype(BF16)
    b_main = jnp.concatenate([b_in[:, :GATE_OFF], b_in[:, SGU_OFF:]], axis=1)[:, None, :]
    w_gate, b_gate = _gate_weights(w_in, b_in)
    proj_w = (w_main, b_main, w_gate, b_gate)
    wo, wgu, wd = w_out.astype(BF16), w_gu.astype(BF16), w_down.astype(BF16)
    fg = final_g[None, :]

    xc = ctx
    for l in range(DEPTH):
        last = l == DEPTH - 1
        mod = mod_all[l]
        ng1, ng2 = norm1_g[l][None, :], norm2_g[l][None, :]
        mix_w = (mlstm_g[l][None, :], sgu_ln_g[l][None, :], sgu_ln_b[l][None, :], sgu_w[l].astype(BF16),
                 jnp.repeat(sgu_b[l].T, SGU_WIDTH // SGU_GROUPS, axis=1),
                 conv_w[l], conv_b[l][None, :], conv_ln_g[l][None, :], conv_ln_b[l][None, :], l, wo)

        k_x, qt_x, kt_x, vt_x, rest_x, rows_x, ecol_x = _inproj(x, mod, row_x, ng1, l, *proj_w, tm_x)
        ctx_proj = _inproj(xc.reshape(1, bsz * lc, D_MODEL), mod, row_c, ng1, l, *proj_w, tm_x, state_only=last)
        ctx_proj = tuple(a.reshape((bsz, a.shape[1] // bsz) + a.shape[2:]) for a in ctx_proj)
        if not last:
            k_c, qt_c, kt_c, vt_c, rest_c, rows_c, ecol_c = ctx_proj
            ctx_proj = (k_c, qt_c, kt_c, vt_c, rows_c, ecol_c)
        hs = _mlstm(ctx_proj, (k_x, qt_x, kt_x, vt_x, rows_x, ecol_x), ctx_out=not last)
        x = _mix(x, mod, row_x, rest_x, hs[-1], *mix_w, tm=tm_x, grid_mode=True)
        x = _ffn(x, mod, row_x, ng2, l, wgu, wd, fg, tm_x, final=last)
        if not last:
            xc = _mix(xc, mod, row_c, rest_c, hs[0], *mix_w, tm=tm_c, grid_mode=False)
            xc = _ffn(xc.reshape(1, bsz * lc, D_MODEL), mod, row_c, ng2, l, wgu, wd, fg, tm_x,
                      final=False).reshape(bsz, lc, D_MODEL)
    return x
```

```python
import functools

import jax
import jax.numpy as jnp
from jax import lax
from jax.experimental import pallas as pl
from jax.experimental.pallas import tpu as pltpu

D_MODEL = 1024
DEPTH = 2
GRID_W = 64
EPS = 1e-6

HEADS = 4
HEAD_DIM = 128
MLSTM_WIDTH = HEADS * HEAD_DIM
CHUNK = 128
UNITS = 2 * HEADS
SGU_GROUPS = 4
SGU_WIDTH = 256
CONV_WIDTH = 256
CONV_K = 31
CONV_R = CONV_K // 2
FFN_HIDDEN = 2816
FFN_CHUNK = 256

GATE_OFF = 4 * MLSTM_WIDTH
N_GATES = 4 * HEADS
SGU_OFF = GATE_OFF + N_GATES
QKV_COLS = 3 * MLSTM_WIDTH
QK_COLS = 2 * MLSTM_WIDTH
REST_COLS = MLSTM_WIDTH + 2 * SGU_WIDTH + 2 * CONV_WIDTH
MAIN_COLS = QKV_COLS + REST_COLS
LANES = 128
SUBLANES = 8
BF16_ROWS = 16
MOD_ROWS = 32
MOD_TILE = 1536
TOKEN_TILE = 512
N_ROWQ = 5
AUG = HEAD_DIM + BF16_ROWS

VMEM_LIMIT = 56 * 1024 * 1024

F32 = jnp.float32
BF16 = jnp.bfloat16
NT = (((1,), (1,)), ((), ()))

GELU_C = 0.7978845608028654
GELU_A = 0.044715


def _sigmoid(t):
    return 0.5 * jnp.tanh(0.5 * t) + 0.5


def _silu(t):
    return t * _sigmoid(t)


def _gelu_tanh(t):
    inner = t * (GELU_C * GELU_A * (t * t) + GELU_C)
    return t * (0.5 * jnp.tanh(inner) + 0.5)


def _log_sigmoid(t):
    return jnp.minimum(t, 0.0) - jnp.log1p(jnp.exp(-jnp.abs(t)))


def _layernorm(t, g, b):
    mu = jnp.mean(t, axis=-1, keepdims=True)
    d = t - mu
    var = jnp.mean(d * d, axis=-1, keepdims=True)
    return d * lax.rsqrt(var + EPS) * g + b


def _rmsnorm(t, g):
    return t * lax.rsqrt(jnp.mean(t * t, axis=-1, keepdims=True) + EPS) * g


def _modulated_rmsnorm(t, g, scale, shift):
    return t * lax.rsqrt(jnp.mean(t * t, axis=-1, keepdims=True) + EPS) * (g * (1.0 + scale)) + shift


def _bf16_rows(row, n):
    tile = jnp.broadcast_to(row, (BF16_ROWS, row.shape[1])).astype(BF16)
    return jnp.concatenate([tile] * (n // BF16_ROWS), axis=0)


def _params(*sem):
    return pltpu.CompilerParams(dimension_semantics=sem, vmem_limit_bytes=VMEM_LIMIT)


def _split_bf16(t, terms):
    parts = []
    for _ in range(terms):
        parts.append(t.astype(BF16))
        t = t - parts[-1].astype(F32)
    return parts


def _mod_kernel(c_ref, w_ref, b_ref, o_ref):
    a = _split_bf16(_silu(c_ref[...]), 3)
    w_hi, w_lo = _split_bf16(w_ref[0], 2)
    hi = jnp.dot(jnp.concatenate(a, axis=0), w_hi, preferred_element_type=F32)
    lo = jnp.dot(jnp.concatenate(a[:2], axis=0), w_lo, preferred_element_type=F32)
    n = MOD_ROWS
    o_ref[0] = hi[0:n] + hi[n:2 * n] + hi[2 * n:3 * n] + lo[0:n] + lo[n:2 * n] + b_ref[0]


def _modulation(c_all, w_mod, b_mod):
    tn = MOD_TILE
    n = w_mod.shape[-1]
    return pl.pallas_call(
        _mod_kernel,
        grid=(DEPTH, n // tn),
        in_specs=[
            pl.BlockSpec((MOD_ROWS, D_MODEL), lambda l, j: (0, 0)),
            pl.BlockSpec((1, D_MODEL, tn), lambda l, j: (l, 0, j)),
            pl.BlockSpec((1, 1, tn), lambda l, j: (l, 0, j)),
        ],
        out_specs=pl.BlockSpec((1, MOD_ROWS, tn), lambda l, j: (l, 0, j)),
        out_shape=jax.ShapeDtypeStruct((DEPTH, MOD_ROWS, n), F32),
        compiler_params=_params("parallel", "parallel"),
        name="modulation",
    )(c_all, w_mod, b_mod.reshape(DEPTH, 1, n))


def _lane_scan(x, op, lane, forward):
    n = x.shape[1]
    k = 1
    while k < n:
        if forward:
            x = jnp.where(lane >= k, op(x, pltpu.roll(x, k, axis=1)), x)
        else:
            x = jnp.where(lane < n - k, op(x, pltpu.roll(x, n - k, axis=1)), x)
        k *= 2
    return x


def _inproj_kernel(x_ref, mod_ref, ng_ref, w_ref, b_ref, wg_ref, bg_ref, *outs, tm, state_only):
    if state_only:
        kt_ref, vt_ref, rows_ref = outs
    else:
        k_ref, qt_ref, kt_ref, vt_ref, rest_ref, rows_ref, ecol_ref = outs
    x = x_ref[0]
    xh = _modulated_rmsnorm(x, ng_ref[...], mod_ref[0, 1:2, :], mod_ref[0, 0:1, :]).astype(BF16)

    g2 = jnp.dot(xh, wg_ref[...], preferred_element_type=F32)
    g = g2[:, 0:LANES] + g2[:, LANES:2 * LANES] + bg_ref[...]
    lane = lax.broadcasted_iota(jnp.int32, (UNITS, CHUNK), 1)
    fwd_row = lax.broadcasted_iota(jnp.int32, (UNITS, CHUNK), 0) < HEADS
    pad_rows = jnp.zeros((CHUNK - UNITS, CHUNK), F32)
    for ch in range(tm // CHUNK):
        sl = slice(ch * CHUNK, (ch + 1) * CHUNK)
        gt = g[sl].T
        it = gt[0:UNITS]
        lsf = _log_sigmoid(gt[UNITS:2 * UNITS])
        bt = jnp.where(fwd_row, _lane_scan(lsf, jnp.add, lane, True), _lane_scan(lsf, jnp.add, lane, False))
        et = it - bt
        cme = jnp.where(fwd_row, _lane_scan(et, jnp.maximum, lane, True),
                        _lane_scan(et, jnp.maximum, lane, False))
        b_last = jnp.where(fwd_row, jnp.broadcast_to(bt[:, CHUNK - 1:CHUNK], bt.shape),
                           jnp.broadcast_to(bt[:, 0:1], bt.shape))
        cme_last = jnp.broadcast_to(jnp.max(et, axis=1, keepdims=True), et.shape)
        rows_ref[0, ch] = jnp.concatenate([et, cme, bt, b_last, cme_last], axis=0)
        if not state_only:
            ecol_ref[0, sl, :] = jnp.concatenate([et, pad_rows], axis=0).T

    first = MLSTM_WIDTH if state_only else 0
    last = QKV_COLS if state_only else MAIN_COLS
    p = jnp.dot(xh, w_ref[:, first:last], preferred_element_type=F32) + b_ref[:, first:last]
    pk = p[:, MLSTM_WIDTH - first:QK_COLS - first] * (HEAD_DIM ** -0.5)
    pv = p[:, QK_COLS - first:QKV_COLS - first]
    if not state_only:
        pq = p[:, 0:MLSTM_WIDTH]
        k_ref[0] = pk.astype(BF16)
        rest_ref[0] = p[:, QKV_COLS:MAIN_COLS]
    for ch in range(tm // CHUNK):
        sl = slice(ch * CHUNK, (ch + 1) * CHUNK)
        for head in range(HEADS):
            hs = slice(head * HEAD_DIM, (head + 1) * HEAD_DIM)
            if not state_only:
                qt_ref[0, ch, hs, :] = pq[sl, hs].T.astype(BF16)
            kt_ref[0, ch, hs, :] = pk[sl, hs].T.astype(BF16)
            vt_ref[0, ch, hs, :] = pv[sl, hs].T.astype(BF16)


def _inproj(x, mod, mod_row, ng, layer, w_main, b_main, w_gate, b_gate, tm, state_only=False):
    bsz, seq, _ = x.shape
    nch = seq // CHUNK
    tch = tm // CHUNK
    const = lambda b, t: (0, 0)
    of_layer = lambda b, t: (layer, 0, 0)
    keep = (2, 3, 5) if state_only else tuple(range(7))
    return pl.pallas_call(
        functools.partial(_inproj_kernel, tm=tm, state_only=state_only),
        grid=(bsz, seq // tm),
        in_specs=[
            pl.BlockSpec((1, tm, D_MODEL), lambda b, t: (b, t, 0)),
            pl.BlockSpec((1, 6, D_MODEL), mod_row),
            pl.BlockSpec((1, D_MODEL), const),
            pl.BlockSpec((None, D_MODEL, MAIN_COLS), of_layer),
            pl.BlockSpec((None, 1, MAIN_COLS), of_layer),
            pl.BlockSpec((None, D_MODEL, 2 * LANES), of_layer),
            pl.BlockSpec((None, 1, LANES), of_layer),
        ],
        out_specs=[spec for i, spec in enumerate([
            pl.BlockSpec((1, tm, MLSTM_WIDTH), lambda b, t: (b, t, 0)),
            pl.BlockSpec((1, tch, MLSTM_WIDTH, CHUNK), lambda b, t: (b, t, 0, 0)),
            pl.BlockSpec((1, tch, MLSTM_WIDTH, CHUNK), lambda b, t: (b, t, 0, 0)),
            pl.BlockSpec((1, tch, MLSTM_WIDTH, CHUNK), lambda b, t: (b, t, 0, 0)),
            pl.BlockSpec((1, tm, REST_COLS), lambda b, t: (b, t, 0)),
            pl.BlockSpec((1, tch, N_ROWQ * UNITS, CHUNK), lambda b, t: (b, t, 0, 0)),
            pl.BlockSpec((1, tm, LANES), lambda b, t: (b, t, 0)),
        ]) if i in keep],
        out_shape=[shape for i, shape in enumerate([
            jax.ShapeDtypeStruct((bsz, seq, MLSTM_WIDTH), BF16),
            jax.ShapeDtypeStruct((bsz, nch, MLSTM_WIDTH, CHUNK), BF16),
            jax.ShapeDtypeStruct((bsz, nch, MLSTM_WIDTH, CHUNK), BF16),
            jax.ShapeDtypeStruct((bsz, nch, MLSTM_WIDTH, CHUNK), BF16),
            jax.ShapeDtypeStruct((bsz, seq, REST_COLS), F32),
            jax.ShapeDtypeStruct((bsz, nch, N_ROWQ * UNITS, CHUNK), F32),
            jax.ShapeDtypeStruct((bsz, seq, LANES), F32),
        ]) if i in keep],
        compiler_params=_params("parallel", "parallel"),
        name="inproj",
    )(x, mod, ng, w_main, b_main, w_gate, b_gate)


def _mlstm_kernel(*refs, ctx_out, n_ctx, n_x):
    if ctx_out:
        crefs, xrefs = refs[0:6], refs[6:12]
        hc_ref, hx_ref, ct_ref, m_ref, part_ref = refs[12:]
    else:
        crefs, xrefs = (None, None) + refs[0:3] + (None,), refs[3:9]
        hx_ref, ct_ref, m_ref, part_ref = refs[9:]
        hc_ref = None
    ct_ref[...] = jnp.zeros(ct_ref.shape, F32)
    m_ref[...] = jnp.zeros(m_ref.shape, F32)
    r = lax.broadcasted_iota(jnp.int32, (CHUNK, CHUNK), 0)
    c = lax.broadcasted_iota(jnp.int32, (CHUNK, CHUNK), 1)
    masks = (r <= c, r >= c)
    ones = jnp.ones((BF16_ROWS, CHUNK), BF16)
    zeros = jnp.zeros((CHUNK, CHUNK), BF16)

    def block_diag(a, b):
        return jnp.concatenate([jnp.concatenate([a, zeros], axis=1),
                                jnp.concatenate([zeros, b], axis=1)], axis=0)

    def pair(refs, j, hp, direction, out_ref, mode, slot):
        k_ref, qt_ref, kt_ref, vt_ref, row_ref, ec_ref = refs
        heads = (2 * hp, 2 * hp + 1)
        units = tuple(direction * HEADS + h for h in heads)
        sp = direction * (HEADS // 2) + hp
        toks = slice(j * CHUNK, (j + 1) * CHUNK)
        hsl = [slice(h * HEAD_DIM, (h + 1) * HEAD_DIM) for h in heads]

        def row(q, u):
            return row_ref[0, j, q * UNITS + u:q * UNITS + u + 1, :]

        e, cme, b, b_last, cme_last = ([row(q, u) for u in units] for q in range(N_ROWQ))
        m_old = [m_ref[u, 0:1, :] for u in units]
        ct_old = ct_ref[sp]
        vt_aug = [jnp.concatenate([vt_ref[0, j, s, :], ones], axis=0) for s in hsl]

        if mode is not None:
            qt = [qt_ref[0, j, s, :] for s in hsl]
            k2 = k_ref[0, toks, 2 * hp * HEAD_DIM:(2 * hp + 2) * HEAD_DIM]
            qkt = jnp.dot(k2, block_diag(qt[0], qt[1]), preferred_element_type=F32)

        g_last = [jnp.maximum(m_old[i], cme_last[i]) for i in range(2)]
        kst = [kt_ref[0, j, hsl[i], :] * _bf16_rows(jnp.exp(e[i] - g_last[i]), CHUNK) for i in range(2)]
        upd = lax.dot_general(jnp.concatenate(vt_aug, axis=1), block_diag(kst[0], kst[1]), NT,
                              preferred_element_type=F32)
        keep = jnp.concatenate([jnp.exp(m_old[i] - g_last[i]) for i in range(2)], axis=1)
        ct_ref[sp] = keep * ct_old + upd
        for i in range(2):
            m_ref[units[i]] = jnp.broadcast_to(b_last[i] + g_last[i], m_ref.shape[1:])

        yield
        if mode is None:
            return
        ct_b = ct_old.astype(BF16)
        for i in range(2):
            u, cols = units[i], hsl[i]
            g = jnp.maximum(m_old[i], cme[i])
            dl = jnp.where(masks[direction], ec_ref[0, toks, u:u + 1] - g, -jnp.inf)
            st = (qkt[:, i * CHUNK:(i + 1) * CHUNK] * jnp.exp(dl)).astype(BF16)
            qs = qt[i] * _bf16_rows(jnp.exp(m_old[i] - g), CHUNK)
            res = jnp.dot(jnp.concatenate([vt_aug[i], ct_b[:, i * HEAD_DIM:(i + 1) * HEAD_DIM]], axis=1),
                          jnp.concatenate([st, qs], axis=0), preferred_element_type=F32)
            den = jnp.maximum(jnp.abs(res[HEAD_DIM:HEAD_DIM + 1, :]), jnp.exp(-(b[i] + g)))
            ht = res[0:HEAD_DIM, :] * (1.0 / den)
            if mode == "first":
                part_ref[direction, slot, cols, :] = ht
            else:
                tot = part_ref[1 - direction, slot, cols, :] + ht
                hn = tot * lax.rsqrt(jnp.mean(tot * tot, axis=0, keepdims=True) + EPS)
                out_ref[0, toks, cols] = hn.T

    def step(refs, j, n, out_ref, mode):
        slot = None if mode is None else (j if mode == "first" else n - 1 - j)
        pairs = []
        for hp in range(HEADS // 2):
            pairs.append(pair(refs, j, hp, 0, out_ref, mode, slot))
            pairs.append(pair(refs, n - 1 - j, hp, 1, out_ref, mode, slot))
        for phase in range(2):
            for p in pairs:
                next(p, None)

    for j in range(n_ctx):
        mode = None if not ctx_out else ("first" if j < n_ctx // 2 else "second")
        step(crefs, j, n_ctx, hc_ref, mode)
    for j in range(n_x):
        step(xrefs, j, n_x, hx_ref, "first" if j < n_x // 2 else "second")


def _mlstm(ctx_in, x_in, ctx_out):
    bsz, lx, _ = x_in[0].shape
    n_ctx, n_x = ctx_in[4 if ctx_out else 2].shape[1], lx // CHUNK
    lc = n_ctx * CHUNK
    assert n_ctx % 2 == 0 and n_x % 2 == 0
    i3 = lambda b: (b, 0, 0)
    i4 = lambda b: (b, 0, 0, 0)

    def specs(seq, nch):
        return [
            pl.BlockSpec((1, seq, MLSTM_WIDTH), i3),
            pl.BlockSpec((1, nch, MLSTM_WIDTH, CHUNK), i4),
            pl.BlockSpec((1, nch, MLSTM_WIDTH, CHUNK), i4),
            pl.BlockSpec((1, nch, MLSTM_WIDTH, CHUNK), i4),
            pl.BlockSpec((1, nch, N_ROWQ * UNITS, CHUNK), i4),
            pl.BlockSpec((1, seq, LANES), i3),
        ]

    out_specs = [pl.BlockSpec((1, lx, MLSTM_WIDTH), i3)]
    out_shape = [jax.ShapeDtypeStruct((bsz, lx, MLSTM_WIDTH), F32)]
    if ctx_out:
        out_specs.insert(0, pl.BlockSpec((1, lc, MLSTM_WIDTH), i3))
        out_shape.insert(0, jax.ShapeDtypeStruct((bsz, lc, MLSTM_WIDTH), F32))
    return pl.pallas_call(
        functools.partial(_mlstm_kernel, ctx_out=ctx_out, n_ctx=n_ctx, n_x=n_x),
        grid=(bsz,),
        in_specs=(specs(lc, n_ctx) if ctx_out else specs(lc, n_ctx)[2:5]) + specs(lx, n_x),
        out_specs=out_specs,
        out_shape=out_shape,
        scratch_shapes=[
            pltpu.VMEM((UNITS // 2, AUG, 2 * HEAD_DIM), F32),
            pltpu.VMEM((UNITS, SUBLANES, LANES), F32),
            pltpu.VMEM((2, max(n_ctx, n_x) // 2, MLSTM_WIDTH, CHUNK), F32),
        ],
        compiler_params=_params("parallel"),
        name="mlstm",
    )(*ctx_in, *x_in)


def _shift_conv(y, w_ref, lo, width, pos):
    n = y.shape[0]
    acc = jnp.zeros(y.shape, F32)
    for j in range(CONV_K):
        d = j - CONV_R
        w = w_ref[j:j + 1, lo:lo + y.shape[1]]
        if d == 0:
            acc = acc + w * y
        else:
            sh = pltpu.roll(y, (-d) % n, axis=0)
            valid = (pos >= -d) if d < 0 else (pos < width - d)
            acc = acc + w * jnp.where(valid, sh, 0.0)
    return acc


def _mix_kernel(x_ref, mod_ref, os_ref, h_ref, z_ref, mg_ref, slg_ref, slb_ref, sw_ref, sb_ref,
                cw_ref, cb_ref, clg_ref, clb_ref, wo_ref, out_ref, *scratch, seq, tm, grid_mode):
    t = pl.program_id(1)
    r0 = pl.multiple_of(t * tm, tm)
    rows = pl.ds(r0, tm)
    half = CONV_WIDTH // 2
    pad = CONV_R * GRID_W

    if grid_mode:
        (ycol_ref,) = scratch

        @pl.when(t == 0)
        def _():
            ycol_ref[0:pad, :] = jnp.zeros((pad, half), F32)
            ycol_ref[pad + seq:pad + seq + pad, :] = jnp.zeros((pad, half), F32)
            ycol_ref[pad:pad + seq, :] = z_ref[0, :, half:CONV_WIDTH] * _sigmoid(
                z_ref[0, :, CONV_WIDTH + half:2 * CONV_WIDTH])

        pos = lax.broadcasted_iota(jnp.int32, (tm, half), 0) & (GRID_W - 1)
        y_row = z_ref[0, rows, 0:half] * _sigmoid(z_ref[0, rows, CONV_WIDTH:CONV_WIDTH + half])
        conv_r = _shift_conv(y_row, cw_ref, 0, GRID_W, pos)
        conv_c = jnp.zeros((tm, half), F32)
        for j in range(CONV_K):
            start = pl.multiple_of(r0 + GRID_W * j, GRID_W)
            conv_c = conv_c + cw_ref[j:j + 1, half:CONV_WIDTH] * ycol_ref[pl.ds(start, tm), :]
        conv = jnp.concatenate([conv_r, conv_c], axis=1)
    else:
        pos = lax.broadcasted_iota(jnp.int32, (tm, CONV_WIDTH), 0)
        y = z_ref[0, :, 0:CONV_WIDTH] * _sigmoid(z_ref[0, :, CONV_WIDTH:2 * CONV_WIDTH])
        conv = _shift_conv(y, cw_ref, 0, tm, pos)
    y_c = _silu(_layernorm(conv + cb_ref[...], clg_ref[...], clb_ref[...]))

    zs = _gelu_tanh(os_ref[0, :, MLSTM_WIDTH:MLSTM_WIDTH + 2 * SGU_WIDTH])
    u = zs[:, 0:SGU_WIDTH]
    vn = _layernorm(zs[:, SGU_WIDTH:2 * SGU_WIDTH], slg_ref[...], slb_ref[...]).astype(BF16)
    lane_group = lax.broadcasted_iota(jnp.int32, (CHUNK, SGU_WIDTH), 1) // (SGU_WIDTH // SGU_GROUPS)
    mixed = []
    for ch in range(tm // CHUNK):
        vc = vn[ch * CHUNK:(ch + 1) * CHUNK]
        m = sb_ref[...]
        for g in range(SGU_GROUPS):
            m = m + jnp.where(lane_group == g, jnp.dot(sw_ref[g], vc, preferred_element_type=F32), 0.0)
        mixed.append(m)
    y_b = u * jnp.concatenate(mixed, axis=0)

    y_a = h_ref[0] * mg_ref[...] * _sigmoid(os_ref[0, :, 0:MLSTM_WIDTH])

    o_a, o_b = MLSTM_WIDTH, MLSTM_WIDTH + SGU_WIDTH
    proj = jnp.dot(y_a.astype(BF16), wo_ref[0:o_a, :], preferred_element_type=F32)
    proj = proj + jnp.dot(y_b.astype(BF16), wo_ref[o_a:o_b, :], preferred_element_type=F32)
    proj = proj + jnp.dot(y_c.astype(BF16), wo_ref[o_b:, :], preferred_element_type=F32)
    out_ref[0] = x_ref[0] + mod_ref[0, 2:3, :] * proj


def _mix(x, mod, mod_row, rest, h, mg, slg, slb, sw, sbias, cw, cb, clg, clb, layer, wo, tm, grid_mode):
    bsz, seq, _ = x.shape
    const2 = lambda b, t: (0, 0)
    const3 = lambda b, t: (0, 0, 0)
    scratch = []
    if grid_mode:
        scratch.append(pltpu.VMEM((seq + 2 * CONV_R * GRID_W, CONV_WIDTH // 2), F32))
    return pl.pallas_call(
        functools.partial(_mix_kernel, seq=seq, tm=tm, grid_mode=grid_mode),
        grid=(bsz, seq // tm),
        in_specs=[
            pl.BlockSpec((1, tm, D_MODEL), lambda b, t: (b, t, 0)),
            pl.BlockSpec((1, 6, D_MODEL), mod_row),
            pl.BlockSpec((1, tm, MLSTM_WIDTH + 2 * SGU_WIDTH), lambda b, t: (b, t, 0)),
            pl.BlockSpec((1, tm, MLSTM_WIDTH), lambda b, t: (b, t, 0)),
            pl.BlockSpec((1, seq, 2 * CONV_WIDTH), lambda b, t: (b, 0, 2)),
            pl.BlockSpec((1, MLSTM_WIDTH), const2),
            pl.BlockSpec((1, SGU_WIDTH), const2),
            pl.BlockSpec((1, SGU_WIDTH), const2),
            pl.BlockSpec((SGU_GROUPS, CHUNK, CHUNK), const3),
            pl.BlockSpec((CHUNK, SGU_WIDTH), const2),
            pl.BlockSpec((CONV_K, CONV_WIDTH), const2),
            pl.BlockSpec((1, CONV_WIDTH), const2),
            pl.BlockSpec((1, CONV_WIDTH), const2),
            pl.BlockSpec((1, CONV_WIDTH), const2),
            pl.BlockSpec((None, D_MODEL, D_MODEL), lambda b, t: (layer, 0, 0)),
        ],
        out_specs=pl.BlockSpec((1, tm, D_MODEL), lambda b, t: (b, t, 0)),
        out_shape=jax.ShapeDtypeStruct((bsz, seq, D_MODEL), F32),
        scratch_shapes=scratch,
        compiler_params=_params("parallel", "arbitrary"),
        name="mix",
    )(x, mod, rest, h, rest, mg, slg, slb, sw, sbias, cw, cb, clg, clb, wo)


def _ffn_kernel(x_ref, mod_ref, ng_ref, wgu_ref, wd_ref, fg_ref, out_ref, *, final):
    x = x_ref[0]
    xm = _modulated_rmsnorm(x, ng_ref[...], mod_ref[0, 4:5, :], mod_ref[0, 3:4, :]).astype(BF16)
    acts = []
    for ch in range(FFN_HIDDEN // FFN_CHUNK):
        lo = ch * FFN_CHUNK
        gate = jnp.dot(xm, wgu_ref[:, lo:lo + FFN_CHUNK], preferred_element_type=F32)
        up = jnp.dot(xm, wgu_ref[:, FFN_HIDDEN + lo:FFN_HIDDEN + lo + FFN_CHUNK], preferred_element_type=F32)
        acts.append((_silu(gate) * up).astype(BF16))
    acc = jnp.dot(jnp.concatenate(acts, axis=1), wd_ref[...], preferred_element_type=F32)
    y = x + mod_ref[0, 5:6, :] * acc
    if final:
        y = _rmsnorm(y, fg_ref[...])
    out_ref[0] = y


def _ffn(x, mod, mod_row, ng, layer, wgu, wd, fg, tm, final):
    bsz, seq, _ = x.shape
    const = lambda b, t: (0, 0)
    return pl.pallas_call(
        functools.partial(_ffn_kernel, final=final),
        grid=(bsz, seq // tm),
        in_specs=[
            pl.BlockSpec((1, tm, D_MODEL), lambda b, t: (b, t, 0)),
            pl.BlockSpec((1, 6, D_MODEL), mod_row),
            pl.BlockSpec((1, D_MODEL), const),
            pl.BlockSpec((None, D_MODEL, 2 * FFN_HIDDEN), lambda b, t: (layer, 0, 0)),
            pl.BlockSpec((None, FFN_HIDDEN, D_MODEL), lambda b, t: (layer, 0, 0)),
            pl.BlockSpec((1, D_MODEL), const),
        ],
        out_specs=pl.BlockSpec((1, tm, D_MODEL), lambda b, t: (b, t, 0)),
        out_shape=jax.ShapeDtypeStruct((bsz, seq, D_MODEL), F32),
        compiler_params=_params("parallel", "parallel"),
        name="ffn",
    )(x, mod, ng, wgu, wd, fg)


def _gate_weights(w_in, b_in):
    def regroup(t):
        g = t[..., GATE_OFF:SGU_OFF]
        order = [g[..., 0:HEADS], g[..., 2 * HEADS:3 * HEADS], g[..., HEADS:2 * HEADS],
                 g[..., 3 * HEADS:4 * HEADS]]
        pad = [(0, 0)] * (t.ndim - 1) + [(0, LANES - 2 * UNITS)]
        return jnp.pad(jnp.concatenate(order, axis=-1), pad)

    return jnp.concatenate(_split_bf16(regroup(w_in), 2), axis=-1), regroup(b_in)[:, None, :]


def kernel(x, c, ctx, c_ctx, w_mod, b_mod, norm1_g, w_in, b_in, mlstm_g, sgu_ln_g, sgu_ln_b, sgu_w, sgu_b,
           conv_w, conv_b, conv_ln_g, conv_ln_b, w_out, norm2_g, w_gu, w_down, final_g):
    bsz = x.shape[0]
    lc = ctx.shape[1]
    assert bsz + 1 <= MOD_ROWS
    c_all = jnp.zeros((MOD_ROWS, D_MODEL), F32).at[:bsz].set(c).at[bsz].set(c_ctx)
    mod_all = _modulation(c_all, w_mod, b_mod).reshape(DEPTH, MOD_ROWS, 6, D_MODEL)
    row_x = lambda b, t: (b, 0, 0)
    row_c = lambda b, t: (bsz, 0, 0)
    tm_x, tm_c = TOKEN_TILE, lc

    w_main = jnp.concatenate([w_in[:, :, :GATE_OFF], w_in[:, :, SGU_OFF:]], axis=2).astype(BF16)
    b_main = jnp.concatenate([b_in[:, :GATE_OFF], b_in[:, SGU_OFF:]], axis=1)[:, None, :]
    w_gate, b_gate = _gate_weights(w_in, b_in)
    proj_w = (w_main, b_main, w_gate, b_gate)
    wo, wgu, wd = w_out.astype(BF16), w_gu.astype(BF16), w_down.astype(BF16)
    fg = final_g[None, :]

    xc = ctx
    for l in range(DEPTH):
        last = l == DEPTH - 1
        mod = mod_all[l]
        ng1, ng2 = norm1_g[l][None, :], norm2_g[l][None, :]
        mix_w = (mlstm_g[l][None, :], sgu_ln_g[l][None, :], sgu_ln_b[l][None, :], sgu_w[l].astype(BF16),
                 jnp.repeat(sgu_b[l].T, SGU_WIDTH // SGU_GROUPS, axis=1),
                 conv_w[l], conv_b[l][None, :], conv_ln_g[l][None, :], conv_ln_b[l][None, :], l, wo)

        k_x, qt_x, kt_x, vt_x, rest_x, rows_x, ecol_x = _inproj(x, mod, row_x, ng1, l, *proj_w, tm_x)
        ctx_proj = _inproj(xc.reshape(1, bsz * lc, D_MODEL), mod, row_c, ng1, l, *proj_w, tm_x, state_only=last)
        ctx_proj = tuple(a.reshape((bsz, a.shape[1] // bsz) + a.shape[2:]) for a in ctx_proj)
        if not last:
            k_c, qt_c, kt_c, vt_c, rest_c, rows_c, ecol_c = ctx_proj
            ctx_proj = (k_c, qt_c, kt_c, vt_c, rows_c, ecol_c)
        hs = _mlstm(ctx_proj, (k_x, qt_x, kt_x, vt_x, rows_x, ecol_x), ctx_out=not last)
        x = _mix(x, mod, row_x, rest_x, hs[-1], *mix_w, tm=tm_x, grid_mode=True)
        x = _ffn(x, mod, row_x, ng2, l, wgu, wd, fg, tm_x, final=last)
        if not last:
            xc = _mix(xc, mod, row_c, rest_c, hs[0], *mix_w, tm=tm_c, grid_mode=False)
            xc = _ffn(xc.reshape(1, bsz * lc, D_MODEL), mod, row_c, ng2, l, wgu, wd, fg, tm_x,
                      final=False).reshape(bsz, lc, D_MODEL)
    return x
```

```python
import functools

import jax
import jax.numpy as jnp
from jax import lax
from jax.experimental import pallas as pl
from jax.experimental.pallas import tpu as pltpu

D_MODEL = 1024
DEPTH = 2
GRID_W = 64
EPS = 1e-6

HEADS = 4
HEAD_DIM = 128
MLSTM_WIDTH = HEADS * HEAD_DIM
CHUNK = 128
UNITS = 2 * HEADS
SGU_GROUPS = 4
SGU_WIDTH = 256
CONV_WIDTH = 256
CONV_K = 31
CONV_R = CONV_K // 2
FFN_HIDDEN = 2816
FFN_CHUNK = 256

GATE_OFF = 4 * MLSTM_WIDTH
N_GATES = 4 * HEADS
SGU_OFF = GATE_OFF + N_GATES
QKV_COLS = 3 * MLSTM_WIDTH
QK_COLS = 2 * MLSTM_WIDTH
REST_COLS = MLSTM_WIDTH + 2 * SGU_WIDTH + 2 * CONV_WIDTH
MAIN_COLS = QKV_COLS + REST_COLS
LANES = 128
SUBLANES = 8
BF16_ROWS = 16
MOD_ROWS = 32
MOD_TILE = 1536
TOKEN_TILE = 512
FFN_TILE = 1024
N_ROWQ = 5
AUG = HEAD_DIM + BF16_ROWS

VMEM_LIMIT = 56 * 1024 * 1024

F32 = jnp.float32
BF16 = jnp.bfloat16
NT = (((1,), (1,)), ((), ()))

GELU_C = 0.7978845608028654
GELU_A = 0.044715


def _sigmoid(t):
    return 0.5 * jnp.tanh(0.5 * t) + 0.5


def _silu(t):
    return t * _sigmoid(t)


def _gelu_tanh(t):
    inner = t * (GELU_C * GELU_A * (t * t) + GELU_C)
    return t * (0.5 * jnp.tanh(inner) + 0.5)


def _log_sigmoid(t):
    return jnp.minimum(t, 0.0) - jnp.log1p(jnp.exp(-jnp.abs(t)))


def _layernorm(t, g, b):
    mu = jnp.mean(t, axis=-1, keepdims=True)
    d = t - mu
    var = jnp.mean(d * d, axis=-1, keepdims=True)
    return d * lax.rsqrt(var + EPS) * g + b


def _rmsnorm(t, g):
    return t * lax.rsqrt(jnp.mean(t * t, axis=-1, keepdims=True) + EPS) * g


def _modulated_rmsnorm(t, g, scale, shift):
    return t * lax.rsqrt(jnp.mean(t * t, axis=-1, keepdims=True) + EPS) * (g * (1.0 + scale)) + shift


def _bf16_rows(row, n):
    tile = jnp.broadcast_to(row, (BF16_ROWS, row.shape[1])).astype(BF16)
    return jnp.concatenate([tile] * (n // BF16_ROWS), axis=0)


def _params(*sem):
    return pltpu.CompilerParams(dimension_semantics=sem, vmem_limit_bytes=VMEM_LIMIT)


def _split_bf16(t, terms):
    parts = []
    for _ in range(terms):
        parts.append(t.astype(BF16))
        t = t - parts[-1].astype(F32)
    return parts


def _mod_kernel(c_ref, w_ref, b_ref, o_ref):
    a = _split_bf16(_silu(c_ref[...]), 3)
    w_hi, w_lo = _split_bf16(w_ref[0], 2)
    hi = jnp.dot(jnp.concatenate(a, axis=0), w_hi, preferred_element_type=F32)
    lo = jnp.dot(jnp.concatenate(a[:2], axis=0), w_lo, preferred_element_type=F32)
    n = MOD_ROWS
    o_ref[0] = hi[0:n] + hi[n:2 * n] + hi[2 * n:3 * n] + lo[0:n] + lo[n:2 * n] + b_ref[0]


def _modulation(c_all, w_mod, b_mod):
    tn = MOD_TILE
    n = w_mod.shape[-1]
    return pl.pallas_call(
        _mod_kernel,
        grid=(DEPTH, n // tn),
        in_specs=[
            pl.BlockSpec((MOD_ROWS, D_MODEL), lambda l, j: (0, 0)),
            pl.BlockSpec((1, D_MODEL, tn), lambda l, j: (l, 0, j)),
            pl.BlockSpec((1, 1, tn), lambda l, j: (l, 0, j)),
        ],
        out_specs=pl.BlockSpec((1, MOD_ROWS, tn), lambda l, j: (l, 0, j)),
        out_shape=jax.ShapeDtypeStruct((DEPTH, MOD_ROWS, n), F32),
        compiler_params=_params("parallel", "parallel"),
        name="modulation",
    )(c_all, w_mod, b_mod.reshape(DEPTH, 1, n))


def _lane_scan(x, op, lane, forward):
    n = x.shape[1]
    k = 1
    while k < n:
        if forward:
            x = jnp.where(lane >= k, op(x, pltpu.roll(x, k, axis=1)), x)
        else:
            x = jnp.where(lane < n - k, op(x, pltpu.roll(x, n - k, axis=1)), x)
        k *= 2
    return x


def _inproj_kernel(x_ref, mod_ref, ng_ref, w_ref, b_ref, wg_ref, bg_ref, *outs, tm, state_only):
    if state_only:
        kt_ref, vt_ref, rows_ref = outs
    else:
        k_ref, qt_ref, kt_ref, vt_ref, rest_ref, rows_ref, ecol_ref = outs
    x = x_ref[0]
    xh = _modulated_rmsnorm(x, ng_ref[...], mod_ref[0, 1:2, :], mod_ref[0, 0:1, :]).astype(BF16)

    g2 = jnp.dot(xh, wg_ref[...], preferred_element_type=F32)
    g = g2[:, 0:LANES] + g2[:, LANES:2 * LANES] + bg_ref[...]
    lane = lax.broadcasted_iota(jnp.int32, (UNITS, CHUNK), 1)
    fwd_row = lax.broadcasted_iota(jnp.int32, (UNITS, CHUNK), 0) < HEADS
    pad_rows = jnp.zeros((CHUNK - UNITS, CHUNK), F32)
    for ch in range(tm // CHUNK):
        sl = slice(ch * CHUNK, (ch + 1) * CHUNK)
        gt = g[sl].T
        it = gt[0:UNITS]
        lsf = _log_sigmoid(gt[UNITS:2 * UNITS])
        bt = jnp.where(fwd_row, _lane_scan(lsf, jnp.add, lane, True), _lane_scan(lsf, jnp.add, lane, False))
        et = it - bt
        cme = jnp.where(fwd_row, _lane_scan(et, jnp.maximum, lane, True),
                        _lane_scan(et, jnp.maximum, lane, False))
        b_last = jnp.where(fwd_row, jnp.broadcast_to(bt[:, CHUNK - 1:CHUNK], bt.shape),
                           jnp.broadcast_to(bt[:, 0:1], bt.shape))
        cme_last = jnp.broadcast_to(jnp.max(et, axis=1, keepdims=True), et.shape)
        rows_ref[0, ch] = jnp.concatenate([et, cme, bt, b_last, cme_last], axis=0)
        if not state_only:
            ecol_ref[0, sl, :] = jnp.concatenate([et, pad_rows], axis=0).T

    first = MLSTM_WIDTH if state_only else 0
    last = QKV_COLS if state_only else MAIN_COLS
    p = jnp.dot(xh, w_ref[:, first:last], preferred_element_type=F32) + b_ref[:, first:last]
    pk = p[:, MLSTM_WIDTH - first:QK_COLS - first] * (HEAD_DIM ** -0.5)
    pv = p[:, QK_COLS - first:QKV_COLS - first]
    if not state_only:
        pq = p[:, 0:MLSTM_WIDTH]
        k_ref[0] = pk.astype(BF16)
        rest_ref[0] = p[:, QKV_COLS:MAIN_COLS]
    for ch in range(tm // CHUNK):
        sl = slice(ch * CHUNK, (ch + 1) * CHUNK)
        for head in range(HEADS):
            hs = slice(head * HEAD_DIM, (head + 1) * HEAD_DIM)
            if not state_only:
                qt_ref[0, ch, hs, :] = pq[sl, hs].T.astype(BF16)
            kt_ref[0, ch, hs, :] = pk[sl, hs].T.astype(BF16)
            vt_ref[0, ch, hs, :] = pv[sl, hs].T.astype(BF16)


def _inproj(x, mod, mod_row, ng, layer, w_main, b_main, w_gate, b_gate, tm, state_only=False):
    bsz, seq, _ = x.shape
    nch = seq // CHUNK
    tch = tm // CHUNK
    const = lambda b, t: (0, 0)
    of_layer = lambda b, t: (layer, 0, 0)
    keep = (2, 3, 5) if state_only else tuple(range(7))
    return pl.pallas_call(
        functools.partial(_inproj_kernel, tm=tm, state_only=state_only),
        grid=(bsz, seq // tm),
        in_specs=[
            pl.BlockSpec((1, tm, D_MODEL), lambda b, t: (b, t, 0)),
            pl.BlockSpec((1, 6, D_MODEL), mod_row),
            pl.BlockSpec((1, D_MODEL), const),
            pl.BlockSpec((None, D_MODEL, MAIN_COLS), of_layer),
            pl.BlockSpec((None, 1, MAIN_COLS), of_layer),
            pl.BlockSpec((None, D_MODEL, 2 * LANES), of_layer),
            pl.BlockSpec((None, 1, LANES), of_layer),
        ],
        out_specs=[spec for i, spec in enumerate([
            pl.BlockSpec((1, tm, MLSTM_WIDTH), lambda b, t: (b, t, 0)),
            pl.BlockSpec((1, tch, MLSTM_WIDTH, CHUNK), lambda b, t: (b, t, 0, 0)),
            pl.BlockSpec((1, tch, MLSTM_WIDTH, CHUNK), lambda b, t: (b, t, 0, 0)),
            pl.BlockSpec((1, tch, MLSTM_WIDTH, CHUNK), lambda b, t: (b, t, 0, 0)),
            pl.BlockSpec((1, tm, REST_COLS), lambda b, t: (b, t, 0)),
            pl.BlockSpec((1, tch, N_ROWQ * UNITS, CHUNK), lambda b, t: (b, t, 0, 0)),
            pl.BlockSpec((1, tm, LANES), lambda b, t: (b, t, 0)),
        ]) if i in keep],
        out_shape=[shape for i, shape in enumerate([
            jax.ShapeDtypeStruct((bsz, seq, MLSTM_WIDTH), BF16),
            jax.ShapeDtypeStruct((bsz, nch, MLSTM_WIDTH, CHUNK), BF16),
            jax.ShapeDtypeStruct((bsz, nch, MLSTM_WIDTH, CHUNK), BF16),
            jax.ShapeDtypeStruct((bsz, nch, MLSTM_WIDTH, CHUNK), BF16),
            jax.ShapeDtypeStruct((bsz, seq, REST_COLS), F32),
            jax.ShapeDtypeStruct((bsz, nch, N_ROWQ * UNITS, CHUNK), F32),
            jax.ShapeDtypeStruct((bsz, seq, LANES), F32),
        ]) if i in keep],
        compiler_params=_params("parallel", "parallel"),
        name="inproj",
    )(x, mod, ng, w_main, b_main, w_gate, b_gate)


def _mlstm_kernel(*refs, ctx_out, n_ctx, n_x):
    if ctx_out:
        crefs, xrefs = refs[0:6], refs[6:12]
        hc_ref, hx_ref, ct_ref, m_ref, part_ref = refs[12:]
    else:
        crefs, xrefs = (None, None) + refs[0:3] + (None,), refs[3:9]
        hx_ref, ct_ref, m_ref, part_ref = refs[9:]
        hc_ref = None
    ct_ref[...] = jnp.zeros(ct_ref.shape, F32)
    m_ref[...] = jnp.zeros(m_ref.shape, F32)
    r = lax.broadcasted_iota(jnp.int32, (CHUNK, CHUNK), 0)
    c = lax.broadcasted_iota(jnp.int32, (CHUNK, CHUNK), 1)
    masks = (r <= c, r >= c)
    ones = jnp.ones((BF16_ROWS, CHUNK), BF16)
    zeros = jnp.zeros((CHUNK, CHUNK), BF16)

    def block_diag(a, b):
        return jnp.concatenate([jnp.concatenate([a, zeros], axis=1),
                                jnp.concatenate([zeros, b], axis=1)], axis=0)

    def pair(refs, j, hp, direction, out_ref, mode, slot):
        k_ref, qt_ref, kt_ref, vt_ref, row_ref, ec_ref = refs
        heads = (2 * hp, 2 * hp + 1)
        units = tuple(direction * HEADS + h for h in heads)
        sp = direction * (HEADS // 2) + hp
        toks = slice(j * CHUNK, (j + 1) * CHUNK)
        hsl = [slice(h * HEAD_DIM, (h + 1) * HEAD_DIM) for h in heads]

        def row(q, u):
            return row_ref[0, j, q * UNITS + u:q * UNITS + u + 1, :]

        e, cme, b, b_last, cme_last = ([row(q, u) for u in units] for q in range(N_ROWQ))
        m_old = [m_ref[u, 0:1, :] for u in units]
        ct_old = ct_ref[sp]
        vt_aug = [jnp.concatenate([vt_ref[0, j, s, :], ones], axis=0) for s in hsl]

        if mode is not None:
            qt = [qt_ref[0, j, s, :] for s in hsl]
            k2 = k_ref[0, toks, 2 * hp * HEAD_DIM:(2 * hp + 2) * HEAD_DIM]
            qkt = jnp.dot(k2, block_diag(qt[0], qt[1]), preferred_element_type=F32)

        g_last = [jnp.maximum(m_old[i], cme_last[i]) for i in range(2)]
        kst = [kt_ref[0, j, hsl[i], :] * _bf16_rows(jnp.exp(e[i] - g_last[i]), CHUNK) for i in range(2)]
        upd = lax.dot_general(jnp.concatenate(vt_aug, axis=1), block_diag(kst[0], kst[1]), NT,
                              preferred_element_type=F32)
        keep = jnp.concatenate([jnp.exp(m_old[i] - g_last[i]) for i in range(2)], axis=1)
        ct_ref[sp] = keep * ct_old + upd
        for i in range(2):
            m_ref[units[i]] = jnp.broadcast_to(b_last[i] + g_last[i], m_ref.shape[1:])

        yield
        if mode is None:
            return
        ct_b = ct_old.astype(BF16)
        for i in range(2):
            u, cols = units[i], hsl[i]
            g = jnp.maximum(m_old[i], cme[i])
            dl = jnp.where(masks[direction], ec_ref[0, toks, u:u + 1] - g, -jnp.inf)
            st = (qkt[:, i * CHUNK:(i + 1) * CHUNK] * jnp.exp(dl)).astype(BF16)
            qs = qt[i] * _bf16_rows(jnp.exp(m_old[i] - g), CHUNK)
            res = jnp.dot(jnp.concatenate([vt_aug[i], ct_b[:, i * HEAD_DIM:(i + 1) * HEAD_DIM]], axis=1),
                          jnp.concatenate([st, qs], axis=0), preferred_element_type=F32)
            den = jnp.maximum(jnp.abs(res[HEAD_DIM:HEAD_DIM + 1, :]), jnp.exp(-(b[i] + g)))
            ht = res[0:HEAD_DIM, :] * (1.0 / den)
            if mode == "first":
                part_ref[direction, slot, cols, :] = ht
            else:
                tot = part_ref[1 - direction, slot, cols, :] + ht
                hn = tot * lax.rsqrt(jnp.mean(tot * tot, axis=0, keepdims=True) + EPS)
                out_ref[0, toks, cols] = hn.T

    def step(refs, j, n, out_ref, mode):
        slot = None if mode is None else (j if mode == "first" else n - 1 - j)
        pairs = []
        for hp in range(HEADS // 2):
            pairs.append(pair(refs, j, hp, 0, out_ref, mode, slot))
            pairs.append(pair(refs, n - 1 - j, hp, 1, out_ref, mode, slot))
        for phase in range(2):
            for p in pairs:
                next(p, None)

    for j in range(n_ctx):
        mode = None if not ctx_out else ("first" if j < n_ctx // 2 else "second")
        step(crefs, j, n_ctx, hc_ref, mode)
    for j in range(n_x):
        step(xrefs, j, n_x, hx_ref, "first" if j < n_x // 2 else "second")


def _mlstm(ctx_in, x_in, ctx_out):
    bsz, lx, _ = x_in[0].shape
    n_ctx, n_x = ctx_in[4 if ctx_out else 2].shape[1], lx // CHUNK
    lc = n_ctx * CHUNK
    assert n_ctx % 2 == 0 and n_x % 2 == 0
    i3 = lambda b: (b, 0, 0)
    i4 = lambda b: (b, 0, 0, 0)

    def specs(seq, nch):
        return [
            pl.BlockSpec((1, seq, MLSTM_WIDTH), i3),
            pl.BlockSpec((1, nch, MLSTM_WIDTH, CHUNK), i4),
            pl.BlockSpec((1, nch, MLSTM_WIDTH, CHUNK), i4),
            pl.BlockSpec((1, nch, MLSTM_WIDTH, CHUNK), i4),
            pl.BlockSpec((1, nch, N_ROWQ * UNITS, CHUNK), i4),
            pl.BlockSpec((1, seq, LANES), i3),
        ]

    out_specs = [pl.BlockSpec((1, lx, MLSTM_WIDTH), i3)]
    out_shape = [jax.ShapeDtypeStruct((bsz, lx, MLSTM_WIDTH), F32)]
    if ctx_out:
        out_specs.insert(0, pl.BlockSpec((1, lc, MLSTM_WIDTH), i3))
        out_shape.insert(0, jax.ShapeDtypeStruct((bsz, lc, MLSTM_WIDTH), F32))
    return pl.pallas_call(
        functools.partial(_mlstm_kernel, ctx_out=ctx_out, n_ctx=n_ctx, n_x=n_x),
        grid=(bsz,),
        in_specs=(specs(lc, n_ctx) if ctx_out else specs(lc, n_ctx)[2:5]) + specs(lx, n_x),
        out_specs=out_specs,
        out_shape=out_shape,
        scratch_shapes=[
            pltpu.VMEM((UNITS // 2, AUG, 2 * HEAD_DIM), F32),
            pltpu.VMEM((UNITS, SUBLANES, LANES), F32),
            pltpu.VMEM((2, max(n_ctx, n_x) // 2, MLSTM_WIDTH, CHUNK), F32),
        ],
        compiler_params=_params("parallel"),
        name="mlstm",
    )(*ctx_in, *x_in)


def _shift_conv(y, w_ref, lo, width, pos):
    n = y.shape[0]
    acc = jnp.zeros(y.shape, F32)
    for j in range(CONV_K):
        d = j - CONV_R
        w = w_ref[j:j + 1, lo:lo + y.shape[1]]
        if d == 0:
            acc = acc + w * y
        else:
            sh = pltpu.roll(y, (-d) % n, axis=0)
            valid = (pos >= -d) if d < 0 else (pos < width - d)
            acc = acc + w * jnp.where(valid, sh, 0.0)
    return acc


def _mix_kernel(x_ref, mod_ref, os_ref, h_ref, z_ref, mg_ref, slg_ref, slb_ref, sw_ref, sb_ref,
                cw_ref, cb_ref, clg_ref, clb_ref, wo_ref, out_ref, *scratch, seq, tm, grid_mode):
    t = pl.program_id(1)
    r0 = pl.multiple_of(t * tm, tm)
    rows = pl.ds(r0, tm)
    half = CONV_WIDTH // 2
    pad = CONV_R * GRID_W

    if grid_mode:
        (ycol_ref,) = scratch

        @pl.when(t == 0)
        def _():
            ycol_ref[0:pad, :] = jnp.zeros((pad, half), F32)
            ycol_ref[pad + seq:pad + seq + pad, :] = jnp.zeros((pad, half), F32)
            ycol_ref[pad:pad + seq, :] = z_ref[0, :, half:CONV_WIDTH] * _sigmoid(
                z_ref[0, :, CONV_WIDTH + half:2 * CONV_WIDTH])

        pos = lax.broadcasted_iota(jnp.int32, (tm, half), 0) & (GRID_W - 1)
        y_row = z_ref[0, rows, 0:half] * _sigmoid(z_ref[0, rows, CONV_WIDTH:CONV_WIDTH + half])
        conv_r = _shift_conv(y_row, cw_ref, 0, GRID_W, pos)
        conv_c = jnp.zeros((tm, half), F32)
        for j in range(CONV_K):
            start = pl.multiple_of(r0 + GRID_W * j, GRID_W)
            conv_c = conv_c + cw_ref[j:j + 1, half:CONV_WIDTH] * ycol_ref[pl.ds(start, tm), :]
        conv = jnp.concatenate([conv_r, conv_c], axis=1)
    else:
        pos = lax.broadcasted_iota(jnp.int32, (tm, CONV_WIDTH), 0)
        y = z_ref[0, :, 0:CONV_WIDTH] * _sigmoid(z_ref[0, :, CONV_WIDTH:2 * CONV_WIDTH])
        conv = _shift_conv(y, cw_ref, 0, tm, pos)
    y_c = _silu(_layernorm(conv + cb_ref[...], clg_ref[...], clb_ref[...]))

    zs = _gelu_tanh(os_ref[0, :, MLSTM_WIDTH:MLSTM_WIDTH + 2 * SGU_WIDTH])
    u = zs[:, 0:SGU_WIDTH]
    vn = _layernorm(zs[:, SGU_WIDTH:2 * SGU_WIDTH], slg_ref[...], slb_ref[...]).astype(BF16)
    lane_group = lax.broadcasted_iota(jnp.int32, (CHUNK, SGU_WIDTH), 1) // (SGU_WIDTH // SGU_GROUPS)
    mixed = []
    for ch in range(tm // CHUNK):
        vc = vn[ch * CHUNK:(ch + 1) * CHUNK]
        m = sb_ref[...]
        for g in range(SGU_GROUPS):
            m = m + jnp.where(lane_group == g, jnp.dot(sw_ref[g], vc, preferred_element_type=F32), 0.0)
        mixed.append(m)
    y_b = u * jnp.concatenate(mixed, axis=0)

    y_a = h_ref[0] * mg_ref[...] * _sigmoid(os_ref[0, :, 0:MLSTM_WIDTH])

    o_a, o_b = MLSTM_WIDTH, MLSTM_WIDTH + SGU_WIDTH
    proj = jnp.dot(y_a.astype(BF16), wo_ref[0:o_a, :], preferred_element_type=F32)
    proj = proj + jnp.dot(y_b.astype(BF16), wo_ref[o_a:o_b, :], preferred_element_type=F32)
    proj = proj + jnp.dot(y_c.astype(BF16), wo_ref[o_b:, :], preferred_element_type=F32)
    out_ref[0] = x_ref[0] + mod_ref[0, 2:3, :] * proj


def _mix(x, mod, mod_row, rest, h, mg, slg, slb, sw, sbias, cw, cb, clg, clb, layer, wo, tm, grid_mode):
    bsz, seq, _ = x.shape
    const2 = lambda b, t: (0, 0)
    const3 = lambda b, t: (0, 0, 0)
    scratch = []
    if grid_mode:
        scratch.append(pltpu.VMEM((seq + 2 * CONV_R * GRID_W, CONV_WIDTH // 2), F32))
    return pl.pallas_call(
        functools.partial(_mix_kernel, seq=seq, tm=tm, grid_mode=grid_mode),
        grid=(bsz, seq // tm),
        in_specs=[
            pl.BlockSpec((1, tm, D_MODEL), lambda b, t: (b, t, 0)),
            pl.BlockSpec((1, 6, D_MODEL), mod_row),
            pl.BlockSpec((1, tm, MLSTM_WIDTH + 2 * SGU_WIDTH), lambda b, t: (b, t, 0)),
            pl.BlockSpec((1, tm, MLSTM_WIDTH), lambda b, t: (b, t, 0)),
            pl.BlockSpec((1, seq, 2 * CONV_WIDTH), lambda b, t: (b, 0, 2)),
            pl.BlockSpec((1, MLSTM_WIDTH), const2),
            pl.BlockSpec((1, SGU_WIDTH), const2),
            pl.BlockSpec((1, SGU_WIDTH), const2),
            pl.BlockSpec((SGU_GROUPS, CHUNK, CHUNK), const3),
            pl.BlockSpec((CHUNK, SGU_WIDTH), const2),
            pl.BlockSpec((CONV_K, CONV_WIDTH), const2),
            pl.BlockSpec((1, CONV_WIDTH), const2),
            pl.BlockSpec((1, CONV_WIDTH), const2),
            pl.BlockSpec((1, CONV_WIDTH), const2),
            pl.BlockSpec((None, D_MODEL, D_MODEL), lambda b, t: (layer, 0, 0)),
        ],
        out_specs=pl.BlockSpec((1, tm, D_MODEL), lambda b, t: (b, t, 0)),
        out_shape=jax.ShapeDtypeStruct((bsz, seq, D_MODEL), F32),
        scratch_shapes=scratch,
        compiler_params=_params("parallel", "arbitrary"),
        name="mix",
    )(x, mod, rest, h, rest, mg, slg, slb, sw, sbias, cw, cb, clg, clb, wo)


def _ffn_kernel(x_ref, mod_ref, ng_ref, wgu_ref, wd_ref, fg_ref, out_ref, *, final):
    x = x_ref[0]
    xm = _modulated_rmsnorm(x, ng_ref[...], mod_ref[0, 4:5, :], mod_ref[0, 3:4, :]).astype(BF16)
    acts = []
    for ch in range(FFN_HIDDEN // FFN_CHUNK):
        lo = ch * FFN_CHUNK
        gate = jnp.dot(xm, wgu_ref[:, lo:lo + FFN_CHUNK], preferred_element_type=F32)
        up = jnp.dot(xm, wgu_ref[:, FFN_HIDDEN + lo:FFN_HIDDEN + lo + FFN_CHUNK], preferred_element_type=F32)
        acts.append((_silu(gate) * up).astype(BF16))
    acc = jnp.dot(jnp.concatenate(acts, axis=1), wd_ref[...], preferred_element_type=F32)
    y = x + mod_ref[0, 5:6, :] * acc
    if final:
        y = _rmsnorm(y, fg_ref[...])
    out_ref[0] = y


def _ffn(x, mod, mod_row, ng, layer, wgu, wd, fg, tm, final):
    bsz, seq, _ = x.shape
    const = lambda b, t: (0, 0)
    return pl.pallas_call(
        functools.partial(_ffn_kernel, final=final),
        grid=(bsz, seq // tm),
        in_specs=[
            pl.BlockSpec((1, tm, D_MODEL), lambda b, t: (b, t, 0)),
            pl.BlockSpec((1, 6, D_MODEL), mod_row),
            pl.BlockSpec((1, D_MODEL), const),
            pl.BlockSpec((None, D_MODEL, 2 * FFN_HIDDEN), lambda b, t: (layer, 0, 0)),
            pl.BlockSpec((None, FFN_HIDDEN, D_MODEL), lambda b, t: (layer, 0, 0)),
            pl.BlockSpec((1, D_MODEL), const),
        ],
        out_specs=pl.BlockSpec((1, tm, D_MODEL), lambda b, t: (b, t, 0)),
        out_shape=jax.ShapeDtypeStruct((bsz, seq, D_MODEL), F32),
        compiler_params=_params("parallel", "parallel"),
        name="ffn",
    )(x, mod, ng, wgu, wd, fg)


def _gate_weights(w_in, b_in):
    def regroup(t):
        g = t[..., GATE_OFF:SGU_OFF]
        order = [g[..., 0:HEADS], g[..., 2 * HEADS:3 * HEADS], g[..., HEADS:2 * HEADS],
                 g[..., 3 * HEADS:4 * HEADS]]
        pad = [(0, 0)] * (t.ndim - 1) + [(0, LANES - 2 * UNITS)]
        return jnp.pad(jnp.concatenate(order, axis=-1), pad)

    return jnp.concatenate(_split_bf16(regroup(w_in), 2), axis=-1), regroup(b_in)[:, None, :]


def kernel(x, c, ctx, c_ctx, w_mod, b_mod, norm1_g, w_in, b_in, mlstm_g, sgu_ln_g, sgu_ln_b, sgu_w, sgu_b,
           conv_w, conv_b, conv_ln_g, conv_ln_b, w_out, norm2_g, w_gu, w_down, final_g):
    bsz = x.shape[0]
    lc = ctx.shape[1]
    assert bsz + 1 <= MOD_ROWS
    c_all = jnp.zeros((MOD_ROWS, D_MODEL), F32).at[:bsz].set(c).at[bsz].set(c_ctx)
    mod_all = _modulation(c_all, w_mod, b_mod).reshape(DEPTH, MOD_ROWS, 6, D_MODEL)
    row_x = lambda b, t: (b, 0, 0)
    row_c = lambda b, t: (bsz, 0, 0)
    tm_x, tm_c = TOKEN_TILE, lc

    w_main = jnp.concatenate([w_in[:, :, :GATE_OFF], w_in[:, :, SGU_OFF:]], axis=2).astype(BF16)
    b_main = jnp.concatenate([b_in[:, :GATE_OFF], b_in[:, SGU_OFF:]], axis=1)[:, None, :]
    w_gate, b_gate = _gate_weights(w_in, b_in)
    proj_w = (w_main, b_main, w_gate, b_gate)
    wo, wgu, wd = w_out.astype(BF16), w_gu.astype(BF16), w_down.astype(BF16)
    fg = final_g[None, :]

    xc = ctx
    for l in range(DEPTH):
        last = l == DEPTH - 1
        mod = mod_all[l]
        ng1, ng2 = norm1_g[l][None, :], norm2_g[l][None, :]
        mix_w = (mlstm_g[l][None, :], sgu_ln_g[l][None, :], sgu_ln_b[l][None, :], sgu_w[l].astype(BF16),
                 jnp.repeat(sgu_b[l].T, SGU_WIDTH // SGU_GROUPS, axis=1),
                 conv_w[l], conv_b[l][None, :], conv_ln_g[l][None, :], conv_ln_b[l][None, :], l, wo)

        k_x, qt_x, kt_x, vt_x, rest_x, rows_x, ecol_x = _inproj(x, mod, row_x, ng1, l, *proj_w, tm_x)
        ctx_proj = _inproj(xc.reshape(1, bsz * lc, D_MODEL), mod, row_c, ng1, l, *proj_w, tm_x, state_only=last)
        ctx_proj = tuple(a.reshape((bsz, a.shape[1] // bsz) + a.shape[2:]) for a in ctx_proj)
        if not last:
            k_c, qt_c, kt_c, vt_c, rest_c, rows_c, ecol_c = ctx_proj
            ctx_proj = (k_c, qt_c, kt_c, vt_c, rows_c, ecol_c)
        hs = _mlstm(ctx_proj, (k_x, qt_x, kt_x, vt_x, rows_x, ecol_x), ctx_out=not last)
        x = _mix(x, mod, row_x, rest_x, hs[-1], *mix_w, tm=tm_x, grid_mode=True)
        x = _ffn(x, mod, row_x, ng2, l, wgu, wd, fg, FFN_TILE, final=last)
        if not last:
            xc = _mix(xc, mod, row_c, rest_c, hs[0], *mix_w, tm=tm_c, grid_mode=False)
            xc = _ffn(xc.reshape(1, bsz * lc, D_MODEL), mod, row_c, ng2, l, wgu, wd, fg, tm_x,
                      final=False).reshape(bsz, lc, D_MODEL)
    return x
```

```python
import functools

import jax
import jax.numpy as jnp
from jax import lax
from jax.experimental import pallas as pl
from jax.experimental.pallas import tpu as pltpu

D_MODEL = 1024
DEPTH = 2
GRID_W = 64
EPS = 1e-6

HEADS = 4
HEAD_DIM = 128
MLSTM_WIDTH = HEADS * HEAD_DIM
CHUNK = 128
UNITS = 2 * HEADS
SGU_GROUPS = 4
SGU_WIDTH = 256
CONV_WIDTH = 256
CONV_K = 31
CONV_R = CONV_K // 2
FFN_HIDDEN = 2816
FFN_CHUNK = 256

GATE_OFF = 4 * MLSTM_WIDTH
N_GATES = 4 * HEADS
SGU_OFF = GATE_OFF + N_GATES
QKV_COLS = 3 * MLSTM_WIDTH
QK_COLS = 2 * MLSTM_WIDTH
REST_COLS = MLSTM_WIDTH + 2 * SGU_WIDTH + 2 * CONV_WIDTH
MAIN_COLS = QKV_COLS + REST_COLS
LANES = 128
SUBLANES = 8
BF16_ROWS = 16
MOD_ROWS = 32
MOD_TILE = 1536
TOKEN_TILE = 512
FFN_TILE = 1024
N_ROWQ = 5
AUG = HEAD_DIM + BF16_ROWS

VMEM_LIMIT = 56 * 1024 * 1024

F32 = jnp.float32
BF16 = jnp.bfloat16
NT = (((1,), (1,)), ((), ()))

GELU_C = 0.7978845608028654
GELU_A = 0.044715


def _sigmoid(t):
    return 0.5 * jnp.tanh(0.5 * t) + 0.5


def _silu(t):
    return t * _sigmoid(t)


def _gelu_tanh(t):
    inner = t * (GELU_C * GELU_A * (t * t) + GELU_C)
    return t * (0.5 * jnp.tanh(inner) + 0.5)


def _log_sigmoid(t):
    return jnp.minimum(t, 0.0) - jnp.log1p(jnp.exp(-jnp.abs(t)))


def _layernorm(t, g, b):
    mu = jnp.mean(t, axis=-1, keepdims=True)
    d = t - mu
    var = jnp.mean(d * d, axis=-1, keepdims=True)
    return d * lax.rsqrt(var + EPS) * g + b


def _rmsnorm(t, g):
    return t * lax.rsqrt(jnp.mean(t * t, axis=-1, keepdims=True) + EPS) * g


def _modulated_rmsnorm(t, g, scale, shift):
    return t * lax.rsqrt(jnp.mean(t * t, axis=-1, keepdims=True) + EPS) * (g * (1.0 + scale)) + shift


def _bf16_rows(row, n):
    tile = jnp.broadcast_to(row, (BF16_ROWS, row.shape[1])).astype(BF16)
    return jnp.concatenate([tile] * (n // BF16_ROWS), axis=0)


def _params(*sem):
    return pltpu.CompilerParams(dimension_semantics=sem, vmem_limit_bytes=VMEM_LIMIT)


def _split_bf16(t, terms):
    parts = []
    for _ in range(terms):
        parts.append(t.astype(BF16))
        t = t - parts[-1].astype(F32)
    return parts


def _mod_kernel(c_ref, w_ref, b_ref, o_ref):
    a = _split_bf16(_silu(c_ref[...]), 3)
    w_hi, w_lo = _split_bf16(w_ref[0], 2)
    hi = jnp.dot(jnp.concatenate(a, axis=0), w_hi, preferred_element_type=F32)
    lo = jnp.dot(jnp.concatenate(a[:2], axis=0), w_lo, preferred_element_type=F32)
    n = MOD_ROWS
    o_ref[0] = hi[0:n] + hi[n:2 * n] + hi[2 * n:3 * n] + lo[0:n] + lo[n:2 * n] + b_ref[0]


def _modulation(c_all, w_mod, b_mod):
    tn = MOD_TILE
    n = w_mod.shape[-1]
    return pl.pallas_call(
        _mod_kernel,
        grid=(DEPTH, n // tn),
        in_specs=[
            pl.BlockSpec((MOD_ROWS, D_MODEL), lambda l, j: (0, 0)),
            pl.BlockSpec((1, D_MODEL, tn), lambda l, j: (l, 0, j)),
            pl.BlockSpec((1, 1, tn), lambda l, j: (l, 0, j)),
        ],
        out_specs=pl.BlockSpec((1, MOD_ROWS, tn), lambda l, j: (l, 0, j)),
        out_shape=jax.ShapeDtypeStruct((DEPTH, MOD_ROWS, n), F32),
        compiler_params=_params("parallel", "parallel"),
        name="modulation",
    )(c_all, w_mod, b_mod.reshape(DEPTH, 1, n))


def _lane_scan(x, op, lane, forward):
    n = x.shape[1]
    k = 1
    while k < n:
        if forward:
            x = jnp.where(lane >= k, op(x, pltpu.roll(x, k, axis=1)), x)
        else:
            x = jnp.where(lane < n - k, op(x, pltpu.roll(x, n - k, axis=1)), x)
        k *= 2
    return x


def _inproj_kernel(x_ref, mod_ref, ng_ref, w_ref, b_ref, wg_ref, bg_ref, *outs, tm, state_only):
    if state_only:
        kt_ref, vt_ref, rows_ref = outs
    else:
        k_ref, qt_ref, kt_ref, vt_ref, rest_ref, rows_ref, ecol_ref = outs
    x = x_ref[0]
    xh = _modulated_rmsnorm(x, ng_ref[...], mod_ref[0, 1:2, :], mod_ref[0, 0:1, :]).astype(BF16)

    g2 = jnp.dot(xh, wg_ref[...], preferred_element_type=F32)
    g = g2[:, 0:LANES] + g2[:, LANES:2 * LANES] + bg_ref[...]
    lane = lax.broadcasted_iota(jnp.int32, (UNITS, CHUNK), 1)
    fwd_row = lax.broadcasted_iota(jnp.int32, (UNITS, CHUNK), 0) < HEADS
    pad_rows = jnp.zeros((CHUNK - UNITS, CHUNK), F32)
    for ch in range(tm // CHUNK):
        sl = slice(ch * CHUNK, (ch + 1) * CHUNK)
        gt = g[sl].T
        it = gt[0:UNITS]
        lsf = _log_sigmoid(gt[UNITS:2 * UNITS])
        bt = jnp.where(fwd_row, _lane_scan(lsf, jnp.add, lane, True), _lane_scan(lsf, jnp.add, lane, False))
        et = it - bt
        cme = jnp.where(fwd_row, _lane_scan(et, jnp.maximum, lane, True),
                        _lane_scan(et, jnp.maximum, lane, False))
        b_last = jnp.where(fwd_row, jnp.broadcast_to(bt[:, CHUNK - 1:CHUNK], bt.shape),
                           jnp.broadcast_to(bt[:, 0:1], bt.shape))
        cme_last = jnp.broadcast_to(jnp.max(et, axis=1, keepdims=True), et.shape)
        rows_ref[0, ch] = jnp.concatenate([et, cme, bt, b_last, cme_last], axis=0)
        if not state_only:
            ecol_ref[0, sl, :] = jnp.concatenate([et, pad_rows], axis=0).T

    first = MLSTM_WIDTH if state_only else 0
    last = QKV_COLS if state_only else MAIN_COLS
    p = jnp.dot(xh, w_ref[:, first:last], preferred_element_type=F32) + b_ref[:, first:last]
    pk = p[:, MLSTM_WIDTH - first:QK_COLS - first] * (HEAD_DIM ** -0.5)
    pv = p[:, QK_COLS - first:QKV_COLS - first]
    if not state_only:
        pq = p[:, 0:MLSTM_WIDTH]
        k_ref[0] = pk.astype(BF16)
        rest_ref[0] = p[:, QKV_COLS:MAIN_COLS]
    for ch in range(tm // CHUNK):
        sl = slice(ch * CHUNK, (ch + 1) * CHUNK)
        for head in range(HEADS):
            hs = slice(head * HEAD_DIM, (head + 1) * HEAD_DIM)
            if not state_only:
                qt_ref[0, ch, hs, :] = pq[sl, hs].T.astype(BF16)
            kt_ref[0, ch, hs, :] = pk[sl, hs].T.astype(BF16)
            vt_ref[0, ch, hs, :] = pv[sl, hs].T.astype(BF16)


def _inproj(x, mod, mod_row, ng, layer, w_main, b_main, w_gate, b_gate, tm, state_only=False):
    bsz, seq, _ = x.shape
    nch = seq // CHUNK
    tch = tm // CHUNK
    const = lambda b, t: (0, 0)
    of_layer = lambda b, t: (layer, 0, 0)
    keep = (2, 3, 5) if state_only else tuple(range(7))
    return pl.pallas_call(
        functools.partial(_inproj_kernel, tm=tm, state_only=state_only),
        grid=(bsz, seq // tm),
        in_specs=[
            pl.BlockSpec((1, tm, D_MODEL), lambda b, t: (b, t, 0)),
            pl.BlockSpec((1, 6, D_MODEL), mod_row),
            pl.BlockSpec((1, D_MODEL), const),
            pl.BlockSpec((None, D_MODEL, MAIN_COLS), of_layer),
            pl.BlockSpec((None, 1, MAIN_COLS), of_layer),
            pl.BlockSpec((None, D_MODEL, 2 * LANES), of_layer),
            pl.BlockSpec((None, 1, LANES), of_layer),
        ],
        out_specs=[spec for i, spec in enumerate([
            pl.BlockSpec((1, tm, MLSTM_WIDTH), lambda b, t: (b, t, 0)),
            pl.BlockSpec((1, tch, MLSTM_WIDTH, CHUNK), lambda b, t: (b, t, 0, 0)),
            pl.BlockSpec((1, tch, MLSTM_WIDTH, CHUNK), lambda b, t: (b, t, 0, 0)),
            pl.BlockSpec((1, tch, MLSTM_WIDTH, CHUNK), lambda b, t: (b, t, 0, 0)),
            pl.BlockSpec((1, tm, REST_COLS), lambda b, t: (b, t, 0)),
            pl.BlockSpec((1, tch, N_ROWQ * UNITS, CHUNK), lambda b, t: (b, t, 0, 0)),
            pl.BlockSpec((1, tm, LANES), lambda b, t: (b, t, 0)),
        ]) if i in keep],
        out_shape=[shape for i, shape in enumerate([
            jax.ShapeDtypeStruct((bsz, seq, MLSTM_WIDTH), BF16),
            jax.ShapeDtypeStruct((bsz, nch, MLSTM_WIDTH, CHUNK), BF16),
            jax.ShapeDtypeStruct((bsz, nch, MLSTM_WIDTH, CHUNK), BF16),
            jax.ShapeDtypeStruct((bsz, nch, MLSTM_WIDTH, CHUNK), BF16),
            jax.ShapeDtypeStruct((bsz, seq, REST_COLS), F32),
            jax.ShapeDtypeStruct((bsz, nch, N_ROWQ * UNITS, CHUNK), F32),
            jax.ShapeDtypeStruct((bsz, seq, LANES), F32),
        ]) if i in keep],
        compiler_params=_params("parallel", "parallel"),
        name="inproj",
    )(x, mod, ng, w_main, b_main, w_gate, b_gate)


def _mlstm_kernel(*refs, ctx_out, n_ctx, n_x):
    if ctx_out:
        crefs, xrefs = refs[0:6], refs[6:12]
        hc_ref, hx_ref, ct_ref, m_ref, part_ref = refs[12:]
    else:
        crefs, xrefs = (None, None) + refs[0:3] + (None,), refs[3:9]
        hx_ref, ct_ref, m_ref, part_ref = refs[9:]
        hc_ref = None
    ct_ref[...] = jnp.zeros(ct_ref.shape, F32)
    m_ref[...] = jnp.zeros(m_ref.shape, F32)
    r = lax.broadcasted_iota(jnp.int32, (CHUNK, CHUNK), 0)
    c = lax.broadcasted_iota(jnp.int32, (CHUNK, CHUNK), 1)
    masks = (r <= c, r >= c)
    ones = jnp.ones((BF16_ROWS, CHUNK), BF16)
    zeros = jnp.zeros((CHUNK, CHUNK), BF16)

    def block_diag(a, b):
        return jnp.concatenate([jnp.concatenate([a, zeros], axis=1),
                                jnp.concatenate([zeros, b], axis=1)], axis=0)

    def pair(refs, j, hp, direction, out_ref, mode, slot):
        k_ref, qt_ref, kt_ref, vt_ref, row_ref, ec_ref = refs
        heads = (2 * hp, 2 * hp + 1)
        units = tuple(direction * HEADS + h for h in heads)
        sp = direction * (HEADS // 2) + hp
        toks = slice(j * CHUNK, (j + 1) * CHUNK)
        hsl = [slice(h * HEAD_DIM, (h + 1) * HEAD_DIM) for h in heads]

        def row(q, u):
            return row_ref[0, j, q * UNITS + u:q * UNITS + u + 1, :]

        e, cme, b, b_last, cme_last = ([row(q, u) for u in units] for q in range(N_ROWQ))
        m_old = [m_ref[u, 0:1, :] for u in units]
        ct_old = ct_ref[sp]
        vt_aug = [jnp.concatenate([vt_ref[0, j, s, :], ones], axis=0) for s in hsl]

        if mode is not None:
            qt = [qt_ref[0, j, s, :] for s in hsl]
            k2 = k_ref[0, toks, 2 * hp * HEAD_DIM:(2 * hp + 2) * HEAD_DIM]
            qkt = jnp.dot(k2, block_diag(qt[0], qt[1]), preferred_element_type=F32)

        g_last = [jnp.maximum(m_old[i], cme_last[i]) for i in range(2)]
        kst = [kt_ref[0, j, hsl[i], :] * _bf16_rows(jnp.exp(e[i] - g_last[i]), CHUNK) for i in range(2)]
        upd = lax.dot_general(jnp.concatenate(vt_aug, axis=1), block_diag(kst[0], kst[1]), NT,
                              preferred_element_type=F32)
        keep = jnp.concatenate([jnp.exp(m_old[i] - g_last[i]) for i in range(2)], axis=1)
        ct_ref[sp] = keep * ct_old + upd
        for i in range(2):
            m_ref[units[i]] = jnp.broadcast_to(b_last[i] + g_last[i], m_ref.shape[1:])

        yield
        if mode is None:
            return
        ct_b = ct_old.astype(BF16)
        for i in range(2):
            u, cols = units[i], hsl[i]
            g = jnp.maximum(m_old[i], cme[i])
            dl = jnp.where(masks[direction], ec_ref[0, toks, u:u + 1] - g, -jnp.inf)
            st = (qkt[:, i * CHUNK:(i + 1) * CHUNK] * jnp.exp(dl)).astype(BF16)
            qs = qt[i] * _bf16_rows(jnp.exp(m_old[i] - g), CHUNK)
            res = jnp.dot(jnp.concatenate([vt_aug[i], ct_b[:, i * HEAD_DIM:(i + 1) * HEAD_DIM]], axis=1),
                          jnp.concatenate([st, qs], axis=0), preferred_element_type=F32)
            den = jnp.maximum(jnp.abs(res[HEAD_DIM:HEAD_DIM + 1, :]), jnp.exp(-(b[i] + g)))
            ht = res[0:HEAD_DIM, :] * (1.0 / den)
            if mode == "first":
                part_ref[direction, slot, cols, :] = ht
            else:
                tot = part_ref[1 - direction, slot, cols, :] + ht
                hn = tot * lax.rsqrt(jnp.mean(tot * tot, axis=0, keepdims=True) + EPS)
                out_ref[0, toks, cols] = hn.T

    def step(refs, j, n, out_ref, mode):
        slot = None if mode is None else (j if mode == "first" else n - 1 - j)
        pairs = []
        for hp in range(HEADS // 2):
            pairs.append(pair(refs, j, hp, 0, out_ref, mode, slot))
            pairs.append(pair(refs, n - 1 - j, hp, 1, out_ref, mode, slot))
        for phase in range(2):
            for p in pairs:
                next(p, None)

    for j in range(n_ctx):
        mode = None if not ctx_out else ("first" if j < n_ctx // 2 else "second")
        step(crefs, j, n_ctx, hc_ref, mode)
    for j in range(n_x):
        step(xrefs, j, n_x, hx_ref, "first" if j < n_x // 2 else "second")


def _mlstm(ctx_in, x_in, ctx_out):
    bsz, lx, _ = x_in[0].shape
    n_ctx, n_x = ctx_in[4 if ctx_out else 2].shape[1], lx // CHUNK
    lc = n_ctx * CHUNK
    assert n_ctx % 2 == 0 and n_x % 2 == 0
    i3 = lambda b: (b, 0, 0)
    i4 = lambda b: (b, 0, 0, 0)

    def specs(seq, nch):
        return [
            pl.BlockSpec((1, seq, MLSTM_WIDTH), i3),
            pl.BlockSpec((1, nch, MLSTM_WIDTH, CHUNK), i4),
            pl.BlockSpec((1, nch, MLSTM_WIDTH, CHUNK), i4),
            pl.BlockSpec((1, nch, MLSTM_WIDTH, CHUNK), i4),
            pl.BlockSpec((1, nch, N_ROWQ * UNITS, CHUNK), i4),
            pl.BlockSpec((1, seq, LANES), i3),
        ]

    out_specs = [pl.BlockSpec((1, lx, MLSTM_WIDTH), i3)]
    out_shape = [jax.ShapeDtypeStruct((bsz, lx, MLSTM_WIDTH), F32)]
    if ctx_out:
        out_specs.insert(0, pl.BlockSpec((1, lc, MLSTM_WIDTH), i3))
        out_shape.insert(0, jax.ShapeDtypeStruct((bsz, lc, MLSTM_WIDTH), F32))
    return pl.pallas_call(
        functools.partial(_mlstm_kernel, ctx_out=ctx_out, n_ctx=n_ctx, n_x=n_x),
        grid=(bsz,),
        in_specs=(specs(lc, n_ctx) if ctx_out else specs(lc, n_ctx)[2:5]) + specs(lx, n_x),
        out_specs=out_specs,
        out_shape=out_shape,
        scratch_shapes=[
            pltpu.VMEM((UNITS // 2, AUG, 2 * HEAD_DIM), F32),
            pltpu.VMEM((UNITS, SUBLANES, LANES), F32),
            pltpu.VMEM((2, max(n_ctx, n_x) // 2, MLSTM_WIDTH, CHUNK), F32),
        ],
        compiler_params=_params("parallel"),
        name="mlstm",
    )(*ctx_in, *x_in)


def _shift_conv(y, w_ref, lo, width, pos):
    n = y.shape[0]
    acc = jnp.zeros(y.shape, F32)
    for j in range(CONV_K):
        d = j - CONV_R
        w = w_ref[j:j + 1, lo:lo + y.shape[1]]
        if d == 0:
            acc = acc + w * y
        else:
            sh = pltpu.roll(y, (-d) % n, axis=0)
            valid = (pos >= -d) if d < 0 else (pos < width - d)
            acc = acc + w * jnp.where(valid, sh, 0.0)
    return acc


def _mix_kernel(x_ref, mod_ref, os_ref, h_ref, z_ref, mg_ref, slg_ref, slb_ref, sw_ref, sb_ref,
                cw_ref, cb_ref, clg_ref, clb_ref, wo_ref, out_ref, *scratch, seq, tm, grid_mode):
    t = pl.program_id(1)
    r0 = pl.multiple_of(t * tm, tm)
    rows = pl.ds(r0, tm)
    half = CONV_WIDTH // 2
    pad = CONV_R * GRID_W

    if grid_mode:
        (ycol_ref,) = scratch

        @pl.when(t == 0)
        def _():
            ycol_ref[0:pad, :] = jnp.zeros((pad, half), F32)
            ycol_ref[pad + seq:pad + seq + pad, :] = jnp.zeros((pad, half), F32)
            ycol_ref[pad:pad + seq, :] = z_ref[0, :, half:CONV_WIDTH] * _sigmoid(
                z_ref[0, :, CONV_WIDTH + half:2 * CONV_WIDTH])

        pos = lax.broadcasted_iota(jnp.int32, (tm, half), 0) & (GRID_W - 1)
        y_row = z_ref[0, rows, 0:half] * _sigmoid(z_ref[0, rows, CONV_WIDTH:CONV_WIDTH + half])
        conv_r = _shift_conv(y_row, cw_ref, 0, GRID_W, pos)
        conv_c = jnp.zeros((tm, half), F32)
        for j in range(CONV_K):
            start = pl.multiple_of(r0 + GRID_W * j, GRID_W)
            conv_c = conv_c + cw_ref[j:j + 1, half:CONV_WIDTH] * ycol_ref[pl.ds(start, tm), :]
        conv = jnp.concatenate([conv_r, conv_c], axis=1)
    else:
        pos = lax.broadcasted_iota(jnp.int32, (tm, CONV_WIDTH), 0)
        y = z_ref[0, :, 0:CONV_WIDTH] * _sigmoid(z_ref[0, :, CONV_WIDTH:2 * CONV_WIDTH])
        conv = _shift_conv(y, cw_ref, 0, tm, pos)
    y_c = _silu(_layernorm(conv + cb_ref[...], clg_ref[...], clb_ref[...]))

    zs = _gelu_tanh(os_ref[0, :, MLSTM_WIDTH:MLSTM_WIDTH + 2 * SGU_WIDTH])
    u = zs[:, 0:SGU_WIDTH]
    vn = _layernorm(zs[:, SGU_WIDTH:2 * SGU_WIDTH], slg_ref[...], slb_ref[...]).astype(BF16)
    lane_group = lax.broadcasted_iota(jnp.int32, (CHUNK, SGU_WIDTH), 1) // (SGU_WIDTH // SGU_GROUPS)
    mixed = []
    for ch in range(tm // CHUNK):
        vc = vn[ch * CHUNK:(ch + 1) * CHUNK]
        m = sb_ref[...]
        for g in range(SGU_GROUPS):
            m = m + jnp.where(lane_group == g, jnp.dot(sw_ref[g], vc, preferred_element_type=F32), 0.0)
        mixed.append(m)
    y_b = u * jnp.concatenate(mixed, axis=0)

    y_a = h_ref[0] * mg_ref[...] * _sigmoid(os_ref[0, :, 0:MLSTM_WIDTH])

    o_a, o_b = MLSTM_WIDTH, MLSTM_WIDTH + SGU_WIDTH
    proj = jnp.dot(y_a.astype(BF16), wo_ref[0:o_a, :], preferred_element_type=F32)
    proj = proj + jnp.dot(y_b.astype(BF16), wo_ref[o_a:o_b, :], preferred_element_type=F32)
    proj = proj + jnp.dot(y_c.astype(BF16), wo_ref[o_b:, :], preferred_element_type=F32)
    out_ref[0] = x_ref[0] + mod_ref[0, 2:3, :] * proj


def _mix(x, mod, mod_row, rest, h, mg, slg, slb, sw, sbias, cw, cb, clg, clb, layer, wo, tm, grid_mode):
    bsz, seq, _ = x.shape
    const2 = lambda b, t: (0, 0)
    const3 = lambda b, t: (0, 0, 0)
    scratch = []
    if grid_mode:
        scratch.append(pltpu.VMEM((seq + 2 * CONV_R * GRID_W, CONV_WIDTH // 2), F32))
    return pl.pallas_call(
        functools.partial(_mix_kernel, seq=seq, tm=tm, grid_mode=grid_mode),
        grid=(bsz, seq // tm),
        in_specs=[
            pl.BlockSpec((1, tm, D_MODEL), lambda b, t: (b, t, 0)),
            pl.BlockSpec((1, 6, D_MODEL), mod_row),
            pl.BlockSpec((1, tm, MLSTM_WIDTH + 2 * SGU_WIDTH), lambda b, t: (b, t, 0)),
            pl.BlockSpec((1, tm, MLSTM_WIDTH), lambda b, t: (b, t, 0)),
            pl.BlockSpec((1, seq, 2 * CONV_WIDTH), lambda b, t: (b, 0, 2)),
            pl.BlockSpec((1, MLSTM_WIDTH), const2),
            pl.BlockSpec((1, SGU_WIDTH), const2),
            pl.BlockSpec((1, SGU_WIDTH), const2),
            pl.BlockSpec((SGU_GROUPS, CHUNK, CHUNK), const3),
            pl.BlockSpec((CHUNK, SGU_WIDTH), const2),
            pl.BlockSpec((CONV_K, CONV_WIDTH), const2),
            pl.BlockSpec((1, CONV_WIDTH), const2),
            pl.BlockSpec((1, CONV_WIDTH), const2),
            pl.BlockSpec((1, CONV_WIDTH), const2),
            pl.BlockSpec((None, D_MODEL, D_MODEL), lambda b, t: (layer, 0, 0)),
        ],
        out_specs=pl.BlockSpec((1, tm, D_MODEL), lambda b, t: (b, t, 0)),
        out_shape=jax.ShapeDtypeStruct((bsz, seq, D_MODEL), F32),
        scratch_shapes=scratch,
        compiler_params=_params("parallel", "arbitrary"),
        name="mix",
    )(x, mod, rest, h, rest, mg, slg, slb, sw, sbias, cw, cb, clg, clb, wo)


def _ffn_kernel(x_ref, mod_ref, ng_ref, wgu_ref, wd_ref, fg_ref, out_ref, *, final):
    x = x_ref[0]
    xm = _modulated_rmsnorm(x, ng_ref[...], mod_ref[0, 4:5, :], mod_ref[0, 3:4, :]).astype(BF16)
    acts = []
    for ch in range(FFN_HIDDEN // FFN_CHUNK):
        lo = ch * FFN_CHUNK
        gate = jnp.dot(xm, wgu_ref[:, lo:lo + FFN_CHUNK], preferred_element_type=F32)
        up = jnp.dot(xm, wgu_ref[:, FFN_HIDDEN + lo:FFN_HIDDEN + lo + FFN_CHUNK], preferred_element_type=F32)
        acts.append((_silu(gate) * up).astype(BF16))
    acc = jnp.dot(jnp.concatenate(acts, axis=1), wd_ref[...], preferred_element_type=F32)
    y = x + mod_ref[0, 5:6, :] * acc
    if final:
        y = _rmsnorm(y, fg_ref[...])
    out_ref[0] = y


def _ffn(x, mod, mod_row, ng, layer, wgu, wd, fg, tm, final):
    bsz, seq, _ = x.shape
    const = lambda b, t: (0, 0)
    return pl.pallas_call(
        functools.partial(_ffn_kernel, final=final),
        grid=(bsz, seq // tm),
        in_specs=[
            pl.BlockSpec((1, tm, D_MODEL), lambda b, t: (b, t, 0)),
            pl.BlockSpec((1, 6, D_MODEL), mod_row),
            pl.BlockSpec((1, D_MODEL), const),
            pl.BlockSpec((None, D_MODEL, 2 * FFN_HIDDEN), lambda b, t: (layer, 0, 0)),
            pl.BlockSpec((None, FFN_HIDDEN, D_MODEL), lambda b, t: (layer, 0, 0)),
            pl.BlockSpec((1, D_MODEL), const),
        ],
        out_specs=pl.BlockSpec((1, tm, D_MODEL), lambda b, t: (b, t, 0)),
        out_shape=jax.ShapeDtypeStruct((bsz, seq, D_MODEL), F32),
        compiler_params=_params("parallel", "parallel"),
        name="ffn",
    )(x, mod, ng, wgu, wd, fg)


def _gate_weights(w_in, b_in):
    def regroup(t):
        g = t[..., GATE_OFF:SGU_OFF]
        order = [g[..., 0:HEADS], g[..., 2 * HEADS:3 * HEADS], g[..., HEADS:2 * HEADS],
                 g[..., 3 * HEADS:4 * HEADS]]
        pad = [(0, 0)] * (t.ndim - 1) + [(0, LANES - 2 * UNITS)]
        return jnp.pad(jnp.concatenate(order, axis=-1), pad)

    return jnp.concatenate(_split_bf16(regroup(w_in), 2), axis=-1), regroup(b_in)[:, None, :]


def kernel(x, c, ctx, c_ctx, w_mod, b_mod, norm1_g, w_in, b_in, mlstm_g, sgu_ln_g, sgu_ln_b, sgu_w, sgu_b,
           conv_w, conv_b, conv_ln_g, conv_ln_b, w_out, norm2_g, w_gu, w_down, final_g):
    bsz = x.shape[0]
    lc = ctx.shape[1]
    assert bsz + 1 <= MOD_ROWS
    c_all = jnp.zeros((MOD_ROWS, D_MODEL), F32).at[:bsz].set(c).at[bsz].set(c_ctx)
    mod_all = _modulation(c_all, w_mod, b_mod).reshape(DEPTH, MOD_ROWS, 6, D_MODEL)
    row_x = lambda b, t: (b, 0, 0)
    row_c = lambda b, t: (bsz, 0, 0)
    tm_x, tm_c = TOKEN_TILE, lc

    w_main = jnp.concatenate([w_in[:, :, :GATE_OFF], w_in[:, :, SGU_OFF:]], axis=2).astype(BF16)
    b_main = jnp.concatenate([b_in[:, :GATE_OFF], b_in[:, SGU_OFF:]], axis=1)[:, None, :]
    w_gate, b_gate = _gate_weights(w_in, b_in)
    proj_w = (w_main, b_main, w_gate, b_gate)
    wo, wgu, wd = w_out.astype(BF16), w_gu.astype(BF16), w_down.astype(BF16)
    fg = final_g[None, :]

    xc = ctx
    for l in range(DEPTH):
        last = l == DEPTH - 1
        mod = mod_all[l]
        ng1, ng2 = norm1_g[l][None, :], norm2_g[l][None, :]
        mix_w = (mlstm_g[l][None, :], sgu_ln_g[l][None, :], sgu_ln_b[l][None, :], sgu_w[l].astype(BF16),
                 jnp.repeat(sgu_b[l].T, SGU_WIDTH // SGU_GROUPS, axis=1),
                 conv_w[l], conv_b[l][None, :], conv_ln_g[l][None, :], conv_ln_b[l][None, :], l, wo)

        k_x, qt_x, kt_x, vt_x, rest_x, rows_x, ecol_x = _inproj(x, mod, row_x, ng1, l, *proj_w, FFN_TILE)
        ctx_proj = _inproj(xc.reshape(1, bsz * lc, D_MODEL), mod, row_c, ng1, l, *proj_w, tm_x, state_only=last)
        ctx_proj = tuple(a.reshape((bsz, a.shape[1] // bsz) + a.shape[2:]) for a in ctx_proj)
        if not last:
            k_c, qt_c, kt_c, vt_c, rest_c, rows_c, ecol_c = ctx_proj
            ctx_proj = (k_c, qt_c, kt_c, vt_c, rows_c, ecol_c)
        hs = _mlstm(ctx_proj, (k_x, qt_x, kt_x, vt_x, rows_x, ecol_x), ctx_out=not last)
        x = _mix(x, mod, row_x, rest_x, hs[-1], *mix_w, tm=tm_x, grid_mode=True)
        x = _ffn(x, mod, row_x, ng2, l, wgu, wd, fg, FFN_TILE, final=last)
        if not last:
            xc = _mix(xc, mod, row_c, rest_c, hs[0], *mix_w, tm=tm_c, grid_mode=False)
            xc = _ffn(xc.reshape(1, bsz * lc, D_MODEL), mod, row_c, ng2, l, wgu, wd, fg, tm_x,
                      final=False).reshape(bsz, lc, D_MODEL)
    return x
```
